```python
import math
import jax, jax.numpy as jnp
from jax import lax
import numpy as np

D_MODEL = 1024
BATCH = 2
SEQ = 8192
DEPTH = 2

HEAD_DIM = 64
N_HEADS_DSWA = 8
DSWA_GROUPS = ((128, 1), (512, 4), (2048, 16))
N_HEADS_DIFF = 4
DIFF_VDIM = 2 * HEAD_DIM
WIDTH_DSWA = N_HEADS_DSWA * HEAD_DIM
WIDTH_DIFF = N_HEADS_DIFF * DIFF_VDIM
MIX_WIDTH = WIDTH_DSWA + WIDTH_DIFF
IN_SPLITS = (WIDTH_DSWA,) * 4 + (WIDTH_DIFF,) * 4
IN_COLS = sum(IN_SPLITS)
ROPE_THETA = 500000.0
ROPE_DIM = HEAD_DIM // 4
PLE_DIM = 256
Q_BLOCK = 128
RMS_EPS = 1e-6
SUBLN_EPS = 1e-5

kernel_name = "hymba_dswa_diffattn_ple"


def rmsnorm(x, g, eps=RMS_EPS):
    xf = x.astype(jnp.float32)
    y = xf * lax.rsqrt(jnp.mean(xf * xf, axis=-1, keepdims=True) + eps)
    return (y * g.astype(jnp.float32)).astype(x.dtype)


def rope_partial(x, pos):
    half = ROPE_DIM // 2
    inv = jnp.power(ROPE_THETA, -jnp.arange(half, dtype=jnp.float32) * (2.0 / ROPE_DIM))
    ang = pos.astype(jnp.float32)[:, None] * inv[None, :]
    cos, sin = jnp.cos(ang), jnp.sin(ang)
    xr = x[..., :ROPE_DIM].astype(jnp.float32)
    x1, x2 = xr[..., :half], xr[..., half:]
    rot = jnp.concatenate([x1 * cos - x2 * sin, x2 * cos + x1 * sin], axis=-1).astype(x.dtype)
    return jnp.concatenate([rot, x[..., ROPE_DIM:]], axis=-1)


def dilated_window_group(q, k, v, window, dilation):
    b, h, s, dh = q.shape
    w = window // dilation
    unit = w * dilation
    sp = -(-s // unit) * unit
    L = sp // dilation
    nb = L // w

    def to_blocks(t):
        t = jnp.pad(t, ((0, 0), (0, 0), (0, sp - s), (0, 0)))
        t = t.reshape(b, h, L, dilation, dh).swapaxes(2, 3)
        return t.reshape(b, h, dilation, nb, w, dh)

    def with_prev(t):
        prev = jnp.pad(t[:, :, :, :-1], ((0, 0), (0, 0), (0, 0), (1, 0), (0, 0), (0, 0)))
        return jnp.concatenate([prev, t], axis=4)

    qb = to_blocks(q)
    kc = with_prev(to_blocks(k))
    vc = with_prev(to_blocks(v))
    sc = jnp.einsum('bhrnqd,bhrnkd->bhrnqk', qb, kc).astype(jnp.float32) * (dh ** -0.5)
    i = jnp.arange(w)[:, None]
    j = jnp.arange(2 * w)[None, :]
    dist = w + i - j
    band = (dist >= 0) & (dist <= w)
    mask = band[None] & ((jnp.arange(nb)[:, None, None] > 0) | (j >= w)[None])
    sc = jnp.where(mask, sc, -jnp.inf)
    m = jnp.max(sc, axis=-1, keepdims=True)
    e = jnp.exp(sc - m)
    den = jnp.sum(e, axis=-1)
    o = jnp.einsum('bhrnqk,bhrnkd->bhrnqd', e, vc.astype(jnp.float32)) / den[..., None]
    lse = m[..., 0] + jnp.log(den)

    def from_blocks(t):
        rest = t.shape[5:]
        t = t.reshape((b, h, dilation, L) + rest).swapaxes(2, 3)
        return t.reshape((b, h, sp) + rest)[:, :, :s]

    return from_blocks(o), from_blocks(lse)


def dilated_attention(q, k, v):
    outs, lses = [], []
    for window, dilation in DSWA_GROUPS:
        o, lse = dilated_window_group(q, k, v, window, dilation)
        outs.append(o)
        lses.append(lse)
    wts = jax.nn.softmax(jnp.stack(lses, axis=0), axis=0)
    return jnp.sum(wts[..., None] * jnp.stack(outs, axis=0), axis=0).astype(q.dtype)


def diff_attention(q, k, v, lam, lambda_init, subln_gain):
    b, h, _, s, dh = q.shape
    nq = s // Q_BLOCK
    qb = q.reshape(b, h, 2, nq, Q_BLOCK, dh).transpose(3, 0, 1, 2, 4, 5)
    kpos = jnp.arange(s)

    def block(args):
        qblk, idx = args
        qpos = idx * Q_BLOCK + jnp.arange(Q_BLOCK)
        sc = jnp.einsum('bhcqd,bhckd->bhcqk', qblk, k).astype(jnp.float32) * (dh ** -0.5)
        causal = kpos[None, :] <= qpos[:, None]
        pr = jax.nn.softmax(jnp.where(causal, sc, -jnp.inf), axis=-1)
        a = pr[:, :, 0] - lam * pr[:, :, 1]
        return jnp.einsum('bhqk,bhkd->bhqd', a.astype(v.dtype), v)

    o = lax.map(block, (qb, jnp.arange(nq)))
    o = o.transpose(1, 2, 0, 3, 4).reshape(b, h, s, DIFF_VDIM)
    return rmsnorm(o, subln_gain, SUBLN_EPS) * (1.0 - lambda_init)


def hybrid_layer(h, p_i, layer_idx, norm_gain, w_in, w_out, lq1, lk1, lq2, lk2,
                 subln_gain, ple_norm_gain, w_ple_gate, w_ple):
    b, s, _ = h.shape
    pos = jnp.arange(s)
    u = rmsnorm(h, norm_gain) @ w_in
    qa, ka, va, ga, qd, kd, vd, gd = jnp.split(u, list(np.cumsum(IN_SPLITS)[:-1]), axis=-1)

    def heads_a(t):
        return t.reshape(b, s, N_HEADS_DSWA, HEAD_DIM).transpose(0, 2, 1, 3)
    qa_h = rope_partial(heads_a(qa), pos)
    ka_h = rope_partial(heads_a(ka), pos)
    oa = dilated_attention(qa_h, ka_h, heads_a(va))
    oa = oa.transpose(0, 2, 1, 3).reshape(b, s, WIDTH_DSWA)

    def heads_b(t):
        return t.reshape(b, s, N_HEADS_DIFF, 2, HEAD_DIM).transpose(0, 2, 3, 1, 4)
    qd_h = rope_partial(heads_b(qd), pos)
    kd_h = rope_partial(heads_b(kd), pos)
    vd_h = vd.reshape(b, s, N_HEADS_DIFF, DIFF_VDIM).transpose(0, 2, 1, 3)
    lambda_init = 0.8 - 0.6 * math.exp(-0.3 * layer_idx)
    lam = (jnp.exp(jnp.sum(lq1.astype(jnp.float32) * lk1.astype(jnp.float32)))
           - jnp.exp(jnp.sum(lq2.astype(jnp.float32) * lk2.astype(jnp.float32)))
           + lambda_init)
    od = diff_attention(qd_h, kd_h, vd_h, lam, lambda_init, subln_gain)
    od = od.transpose(0, 2, 1, 3).reshape(b, s, WIDTH_DIFF)

    y = jnp.concatenate([oa * jax.nn.silu(ga), od * jax.nn.silu(gd)], axis=-1) @ w_out
    h = h + y

    gate = jax.nn.sigmoid(rmsnorm(h, ple_norm_gain) @ w_ple_gate)
    return h + (p_i @ w_ple) * gate


def setup_inputs(seed: int = 0) -> dict:
    key = jax.random.key(seed)
    ks = jax.random.split(key, 16)
    f32 = jnp.float32
    nrm = lambda k, shape, scale: jax.random.normal(k, shape, f32) * scale
    return {
        "x": nrm(ks[0], (BATCH, SEQ, D_MODEL), 1.0),
        "p": nrm(ks[1], (DEPTH, BATCH, SEQ, PLE_DIM), 1.0),
        "attn_norm_gain": 1.0 + nrm(ks[2], (DEPTH, D_MODEL), 0.02),
        "w_in": nrm(ks[3], (DEPTH, D_MODEL, IN_COLS), D_MODEL ** -0.5),
        "w_out": nrm(ks[4], (DEPTH, MIX_WIDTH, D_MODEL), MIX_WIDTH ** -0.5),
        "lambda_q1": nrm(ks[5], (DEPTH, HEAD_DIM), 0.1),
        "lambda_k1": nrm(ks[6], (DEPTH, HEAD_DIM), 0.1),
        "lambda_q2": nrm(ks[7], (DEPTH, HEAD_DIM), 0.1),
        "lambda_k2": nrm(ks[8], (DEPTH, HEAD_DIM), 0.1),
        "subln_gain": 1.0 + nrm(ks[9], (DEPTH, DIFF_VDIM), 0.02),
        "ple_norm_gain": 1.0 + nrm(ks[10], (DEPTH, D_MODEL), 0.02),
        "w_ple_gate": nrm(ks[11], (DEPTH, D_MODEL, D_MODEL), D_MODEL ** -0.5),
        "w_ple": nrm(ks[12], (DEPTH, PLE_DIM, D_MODEL), 0.5 * PLE_DIM ** -0.5),
        "final_norm_gain": 1.0 + nrm(ks[13], (D_MODEL,), 0.02),
    }


def reference(x, p, attn_norm_gain, w_in, w_out, lambda_q1, lambda_k1, lambda_q2,
              lambda_k2, subln_gain, ple_norm_gain, w_ple_gate, w_ple, final_norm_gain):
    h = x
    for i in range(DEPTH):
        h = hybrid_layer(h, p[i], i, attn_norm_gain[i], w_in[i], w_out[i],
                         lambda_q1[i], lambda_k1[i], lambda_q2[i], lambda_k2[i],
                         subln_gain[i], ple_norm_gain[i], w_ple_gate[i], w_ple[i])
    return rmsnorm(h, final_norm_gain)
```

```python
import functools
import math

import jax
import jax.numpy as jnp
from jax import lax
from jax.experimental import pallas as pl
from jax.experimental.pallas import tpu as pltpu

HEAD_DIM = 64
LANES = 128
DSWA_W = 128
DSWA_DILATIONS = (16, 4, 1)
N_HEADS_DIFF = 4
WIDTH = 512
IN_COLS = 8 * WIDTH
ROPE_THETA = 500000.0
ROPE_DIM = HEAD_DIM // 4
RMS_EPS = 1e-6
SUBLN_EPS = 1e-5
NEG = -1e30
VMEM_LIMIT = 48 * 1024 * 1024

COL_QA, COL_KA, COL_VA, COL_GA, COL_QB, COL_KB, COL_VB, COL_GB = range(8)
ROPE_SPLITS = (COL_QA, COL_KA, COL_QB, COL_KB)

BF16 = jnp.bfloat16
F32 = jnp.float32


def _params(*sem):
    return pltpu.CompilerParams(dimension_semantics=sem, vmem_limit_bytes=VMEM_LIMIT)


def _proj_kernel(x_ref, g_ref, w_ref, cos_ref, sa_ref, sb_ref, o_ref):
    x = x_ref[...]
    ms = jnp.mean(x * x, axis=-1, keepdims=True)
    xn = (x * lax.rsqrt(ms + RMS_EPS) * g_ref[...]).astype(BF16)
    cos, sa, sb = cos_ref[...], sa_ref[...], sb_ref[...]
    for c in range(IN_COLS // WIDTH):
        acc = jnp.dot(xn, w_ref[:, c * WIDTH:(c + 1) * WIDTH], preferred_element_type=F32)
        if c in ROPE_SPLITS:
            for j in range(WIDTH // LANES):
                blk = acc[:, j * LANES:(j + 1) * LANES]
                rot = (blk * cos + pltpu.roll(blk, LANES - ROPE_DIM // 2, 1) * sa
                       + pltpu.roll(blk, ROPE_DIM // 2, 1) * sb)
                o_ref[:, c * WIDTH + j * LANES:c * WIDTH + (j + 1) * LANES] = rot.astype(BF16)
        else:
            o_ref[:, c * WIDTH:(c + 1) * WIDTH] = acc.astype(BF16)


def _proj(h, gain, w_b, cos, sa, sb, seq, tm=512):
    m, d = h.shape
    nt = seq // tm
    row = lambda i: (i, 0)
    tab = lambda i: (i % nt, 0)
    return pl.pallas_call(
        _proj_kernel,
        grid=(m // tm,),
        in_specs=[pl.BlockSpec((tm, d), row),
                  pl.BlockSpec((1, d), lambda i: (0, 0)),
                  pl.BlockSpec((d, IN_COLS), lambda i: (0, 0)),
                  pl.BlockSpec((tm, LANES), tab),
                  pl.BlockSpec((tm, LANES), tab),
                  pl.BlockSpec((tm, LANES), tab)],
        out_specs=pl.BlockSpec((tm, IN_COLS), row),
        out_shape=jax.ShapeDtypeStruct((m, IN_COLS), BF16),
        compiler_params=_params("arbitrary"),
        name="proj",
    )(h, gain.reshape(1, d), w_b, cos, sa, sb)


def _rope_tables(seq):
    half = ROPE_DIM // 2
    inv = jnp.power(ROPE_THETA, -jnp.arange(half, dtype=F32) * (2.0 / ROPE_DIM))
    ang = jnp.arange(seq).astype(F32)[:, None] * inv[None, :]
    cos, sin = jnp.cos(ang), jnp.sin(ang)
    ch = jnp.arange(LANES) % HEAD_DIM
    cos_l = jnp.take(cos, ch % half, axis=1)
    sin_l = jnp.take(sin, ch % half, axis=1)
    cos_t = jnp.where(ch < ROPE_DIM, cos_l, 1.0)
    sa = jnp.where(ch < half, -sin_l, 0.0)
    sb = jnp.where((ch >= half) & (ch < ROPE_DIM), sin_l, 0.0)
    return cos_t, sa, sb


def _dswa_kernel(*refs, lc, has_prev, is_last):
    q_ref, kc_ref, kp_ref, vc_ref, vp_ref = refs[:5]
    rest = refs[5:]
    if has_prev:
        po_ref, pl_ref = rest[:2]
        rest = rest[2:]
    o_ref = rest[0]
    l_ref = None if is_last else rest[1]

    w = DSWA_W
    first_chunk = pl.program_id(2) == 0
    head_a = lax.broadcasted_iota(jnp.int32, (1, LANES), 1) < HEAD_DIM
    ri = lax.broadcasted_iota(jnp.int32, (2 * w, 2 * w), 0) & (w - 1)
    cj = lax.broadcasted_iota(jnp.int32, (2 * w, 2 * w), 1)
    dist = w + ri - cj
    band = (dist >= 0) & (dist <= w)
    band0 = band & ((cj >= w) | jnp.logical_not(first_chunk))

    def unit(row0, hp, kk, vv, mask):
        lanes = slice(hp * LANES, (hp + 1) * LANES)
        qb = q_ref[0, pl.ds(row0, w), lanes]
        zero = jnp.zeros_like(qb)
        qs = jnp.concatenate([jnp.where(head_a, qb, zero), jnp.where(head_a, zero, qb)], axis=0)
        s = lax.dot_general(qs, kk, (((1,), (1,)), ((), ())), preferred_element_type=F32)
        s = jnp.where(mask, s, NEG)
        m = jnp.max(s, axis=1, keepdims=True)
        p = jnp.exp(s - m)
        den = jnp.sum(p, axis=1, keepdims=True)
        o = jnp.dot(p.astype(BF16), vv, preferred_element_type=F32) / den
        lse = m + jnp.log(den)
        o_t = jnp.where(head_a, o[:w], o[w:])
        lse_t = jnp.where(head_a, lse[:w], lse[w:])
        if has_prev:
            o_p = po_ref[0, pl.ds(row0, w), lanes]
            lse_p = pl_ref[0, pl.ds(row0, w), lanes]
            top = jnp.maximum(lse_t, lse_p)
            wa, wb = jnp.exp(lse_t - top), jnp.exp(lse_p - top)
            tot = wa + wb
            o_t = (wa * o_t + wb * o_p) / tot
            lse_t = top + jnp.log(tot)
        o_ref[0, pl.ds(row0, w), lanes] = o_t.astype(o_ref.dtype)
        if not is_last:
            l_ref[0, pl.ds(row0, w), lanes] = lse_t

    for hp in range(WIDTH // LANES):
        lanes = slice(hp * LANES, (hp + 1) * LANES)
        kk0 = jnp.concatenate([kp_ref[0, :, lanes], kc_ref[0, 0:w, lanes]], axis=0)
        vv0 = jnp.concatenate([vp_ref[0, :, lanes], vc_ref[0, 0:w, lanes]], axis=0)
        unit(0, hp, kk0, vv0, band0)

    def body(n, carry):
        row0 = pl.multiple_of(n * w, w)
        prev0 = pl.multiple_of((n - 1) * w, w)
        for hp in range(WIDTH // LANES):
            lanes = slice(hp * LANES, (hp + 1) * LANES)
            unit(row0, hp, kc_ref[0, pl.ds(prev0, 2 * w), lanes],
                 vc_ref[0, pl.ds(prev0, 2 * w), lanes], band)
        return carry

    lax.fori_loop(1, lc // w, body, 0)


def _dswa_group(u3, prev, dilation, is_last):
    b, s, _ = u3.shape
    d = dilation
    sub = s // d
    lc = min(sub, 1024)
    nc = sub // lc
    per = lc // DSWA_W
    uv = u3.reshape(b, sub, d * IN_COLS)
    ncol = IN_COLS // WIDTH

    def cur(split):
        return pl.BlockSpec((1, lc, WIDTH), lambda bi, r, c: (bi, c, r * ncol + split))

    def prv(split):
        return pl.BlockSpec((1, DSWA_W, WIDTH),
                            lambda bi, r, c: (bi, jnp.maximum(c * per - 1, 0), r * ncol + split))

    nat = pl.BlockSpec((1, lc, WIDTH), lambda bi, r, c: (bi, c, r))
    in_specs = [cur(COL_QA), cur(COL_KA), prv(COL_KA), cur(COL_VA), prv(COL_VA)]
    args = [uv, uv, uv, uv, uv]
    if prev is not None:
        in_specs += [nat, nat]
        args += [prev[0].reshape(b, sub, d * WIDTH), prev[1].reshape(b, sub, d * WIDTH)]
    if is_last:
        out_specs = nat
        out_shape = jax.ShapeDtypeStruct((b, sub, d * WIDTH), BF16)
    else:
        out_specs = [nat, nat]
        out_shape = [jax.ShapeDtypeStruct((b, sub, d * WIDTH), F32)] * 2
    out = pl.pallas_call(
        functools.partial(_dswa_kernel, lc=lc, has_prev=prev is not None, is_last=is_last),
        grid=(b, d, nc),
        in_specs=in_specs,
        out_specs=out_specs,
        out_shape=out_shape,
        compiler_params=_params("arbitrary", "arbitrary", "arbitrary"),
        name=f"dswa_d{d}",
    )(*args)
    if is_last:
        return out.reshape(b, s, WIDTH)
    return out[0].reshape(b, s, WIDTH), out[1].reshape(b, s, WIDTH)


def _dswa(u3):
    prev = None
    for i, d in enumerate(DSWA_DILATIONS):
        prev = _dswa_group(u3, prev, d, is_last=i == len(DSWA_DILATIONS) - 1)
    return prev


def _diff_kernel(lam_ref, gain_ref, q_ref, k_ref, v_ref, o_ref, m_sc, l_sc, acc_sc, *, qb, lambda_init):
    qi = pl.program_id(2)
    head_a = lax.broadcasted_iota(jnp.int32, (1, LANES), 1) < HEAD_DIM
    q = q_ref[0]
    zero = jnp.zeros_like(q)
    qs = jnp.concatenate([jnp.where(head_a, q, zero), jnp.where(head_a, zero, q)], axis=0)

    m_sc[...] = jnp.full(m_sc.shape, NEG, F32)
    l_sc[...] = jnp.zeros(l_sc.shape, F32)
    acc_sc[...] = jnp.zeros(acc_sc.shape, F32)

    def step(off, mask):
        kb = k_ref[0, pl.ds(off, qb), :]
        vb = v_ref[0, pl.ds(off, qb), :]
        s = lax.dot_general(qs, kb, (((1,), (1,)), ((), ())), preferred_element_type=F32)
        if mask is not None:
            s = jnp.where(mask, s, NEG)
        m_old = m_sc[...]
        m_new = jnp.maximum(m_old, jnp.max(s, axis=1, keepdims=True))
        alpha = jnp.exp(m_old - m_new)
        p = jnp.exp(s - m_new)
        l_sc[...] = alpha * l_sc[...] + jnp.sum(p, axis=1, keepdims=True)
        acc_sc[...] = alpha * acc_sc[...] + jnp.dot(p.astype(BF16), vb, preferred_element_type=F32)
        m_sc[...] = m_new

    def body(ki, carry):
        step(pl.multiple_of(ki * qb, qb), None)
        return carry

    lax.fori_loop(0, qi, body, 0)
    ri = lax.broadcasted_iota(jnp.int32, (2 * qb, qb), 0) & (qb - 1)
    cj = lax.broadcasted_iota(jnp.int32, (2 * qb, qb), 1)
    step(pl.multiple_of(qi * qb, qb), cj <= ri)

    o_all = acc_sc[...] / l_sc[...]
    lam_p = lam_ref[...]
    lam = (jnp.exp(jnp.sum(lam_p[0:1] * lam_p[1:2], axis=1, keepdims=True))
           - jnp.exp(jnp.sum(lam_p[2:3] * lam_p[3:4], axis=1, keepdims=True)) + lambda_init)
    o = o_all[:qb] - lam * o_all[qb:]
    o = o * lax.rsqrt(jnp.mean(o * o, axis=-1, keepdims=True) + SUBLN_EPS) * gain_ref[...]
    o_ref[0] = (o * (1.0 - lambda_init)).astype(o_ref.dtype)


def _diff(u3, lam_p, subln_gain, lambda_init, qb=256):
    b, s, _ = u3.shape
    per = WIDTH // LANES
    full = lambda split: pl.BlockSpec((1, s, LANES), lambda bi, h, qi: (bi, 0, split * per + h))
    return pl.pallas_call(
        functools.partial(_diff_kernel, qb=qb, lambda_init=lambda_init),
        grid=(b, N_HEADS_DIFF, s // qb),
        in_specs=[pl.BlockSpec(lam_p.shape, lambda bi, h, qi: (0, 0)),
                  pl.BlockSpec((1, LANES), lambda bi, h, qi: (0, 0)),
                  pl.BlockSpec((1, qb, LANES), lambda bi, h, qi: (bi, qi, COL_QB * per + h)),
                  full(COL_KB), full(COL_VB)],
        out_specs=pl.BlockSpec((1, qb, LANES), lambda bi, h, qi: (bi, qi, h)),
        out_shape=jax.ShapeDtypeStruct((b, s, WIDTH), BF16),
        scratch_shapes=[pltpu.VMEM((2 * qb, 1), F32), pltpu.VMEM((2 * qb, 1), F32),
                        pltpu.VMEM((2 * qb, LANES), F32)],
        compiler_params=_params("arbitrary", "arbitrary", "arbitrary"),
        name="diff",
    )(lam_p, subln_gain.reshape(1, LANES), u3, u3, u3)


def _sigmoid(x):
    return 1.0 / (1.0 + jnp.exp(-x))


def _post_kernel(h_ref, ga_ref, gb_ref, oa_ref, ob_ref, p_ref, wo_ref, ng_ref, wg_ref, wp_ref, fg_ref,
                 o_ref, *, final):
    ga = ga_ref[...].astype(F32)
    gb = gb_ref[...].astype(F32)
    ya = (oa_ref[...].astype(F32) * (ga * _sigmoid(ga))).astype(BF16)
    yb = (ob_ref[...].astype(F32) * (gb * _sigmoid(gb))).astype(BF16)
    y = (jnp.dot(ya, wo_ref[0:WIDTH, :], preferred_element_type=F32)
         + jnp.dot(yb, wo_ref[WIDTH:2 * WIDTH, :], preferred_element_type=F32))
    h1 = h_ref[...] + y
    n = h1 * lax.rsqrt(jnp.mean(h1 * h1, axis=-1, keepdims=True) + RMS_EPS) * ng_ref[...]
    gate = _sigmoid(jnp.dot(n.astype(BF16), wg_ref[...], preferred_element_type=F32))
    ple = jnp.dot(p_ref[...].astype(BF16), wp_ref[...], preferred_element_type=F32)
    h2 = h1 + ple * gate
    if final:
        h2 = h2 * lax.rsqrt(jnp.mean(h2 * h2, axis=-1, keepdims=True) + RMS_EPS) * fg_ref[...]
    o_ref[...] = h2


def _post(h, u, oa, ob, p_i, wo_b, ple_gain, wg_b, wp_b, final_gain, final, tm=512):
    m, d = h.shape
    row = lambda i: (i, 0)
    const = lambda i: (0, 0)
    return pl.pallas_call(
        functools.partial(_post_kernel, final=final),
        grid=(m // tm,),
        in_specs=[pl.BlockSpec((tm, d), row),
                  pl.BlockSpec((tm, WIDTH), lambda i: (i, COL_GA)),
                  pl.BlockSpec((tm, WIDTH), lambda i: (i, COL_GB)),
                  pl.BlockSpec((tm, WIDTH), row),
                  pl.BlockSpec((tm, WIDTH), row),
                  pl.BlockSpec((tm, p_i.shape[1]), row),
                  pl.BlockSpec(wo_b.shape, const),
                  pl.BlockSpec((1, d), const),
                  pl.BlockSpec(wg_b.shape, const),
                  pl.BlockSpec(wp_b.shape, const),
                  pl.BlockSpec((1, d), const)],
        out_specs=pl.BlockSpec((tm, d), row),
        out_shape=jax.ShapeDtypeStruct((m, d), F32),
        compiler_params=_params("arbitrary"),
        name="post",
    )(h, u, u, oa, ob, p_i, wo_b, ple_gain.reshape(1, d), wg_b, wp_b, final_gain.reshape(1, d))


def kernel(x, p, attn_norm_gain, w_in, w_out, lambda_q1, lambda_k1, lambda_q2, lambda_k2, subln_gain,
           ple_norm_gain, w_ple_gate, w_ple, final_norm_gain):
    b, s, d = x.shape
    depth = w_in.shape[0]
    h = x.reshape(b * s, d)
    cos, sa, sb = _rope_tables(s)
    col = jnp.arange(IN_COLS) // WIDTH
    q_scale = jnp.where((col == COL_QA) | (col == COL_QB), HEAD_DIM ** -0.5, 1.0).astype(F32)
    for i in range(depth):
        lambda_init = 0.8 - 0.6 * math.exp(-0.3 * i)
        w_in_b = (w_in[i] * q_scale[None, :]).astype(BF16)
        u = _proj(h, attn_norm_gain[i], w_in_b, cos, sa, sb, s)
        u3 = u.reshape(b, s, IN_COLS)
        oa = _dswa(u3)
        lam_p = jnp.stack([lambda_q1[i], lambda_k1[i], lambda_q2[i], lambda_k2[i]]).astype(F32)
        ob = _diff(u3, lam_p, subln_gain[i], lambda_init)
        h = _post(h, u, oa.reshape(b * s, WIDTH), ob.reshape(b * s, WIDTH), p[i].reshape(b * s, -1),
                  w_out[i].astype(BF16), ple_norm_gain[i], w_ple_gate[i].astype(BF16),
                  w_ple[i].astype(BF16), final_norm_gain, final=i == depth - 1)
    return h.reshape(b, s, d)
```

```python
import functools
import math

import jax
import jax.numpy as jnp
from jax import lax
from jax.experimental import pallas as pl
from jax.experimental.pallas import tpu as pltpu

HEAD_DIM = 64
LANES = 128
DSWA_W = 128
DSWA_DILATIONS = (16, 4, 1)
N_HEADS_DIFF = 4
WIDTH = 512
IN_COLS = 8 * WIDTH
ROPE_THETA = 500000.0
ROPE_DIM = HEAD_DIM // 4
RMS_EPS = 1e-6
SUBLN_EPS = 1e-5
NEG = -1e30
LOG2_E = math.log2(math.e)
VMEM_LIMIT = 48 * 1024 * 1024

COL_QA, COL_KA, COL_VA, COL_GA, COL_QB, COL_KB, COL_VB, COL_GB = range(8)
ROPE_SPLITS = (COL_QA, COL_KA, COL_QB, COL_KB)

BF16 = jnp.bfloat16
F32 = jnp.float32


def _params(*sem):
    return pltpu.CompilerParams(dimension_semantics=sem, vmem_limit_bytes=VMEM_LIMIT)


def _proj_kernel(x_ref, g_ref, w_ref, cos_ref, sa_ref, sb_ref, o_ref):
    x = x_ref[...]
    ms = jnp.mean(x * x, axis=-1, keepdims=True)
    xn = (x * lax.rsqrt(ms + RMS_EPS) * g_ref[...]).astype(BF16)
    cos, sa, sb = cos_ref[...], sa_ref[...], sb_ref[...]
    for c in range(IN_COLS // WIDTH):
        acc = jnp.dot(xn, w_ref[:, c * WIDTH:(c + 1) * WIDTH], preferred_element_type=F32)
        if c in ROPE_SPLITS:
            for j in range(WIDTH // LANES):
                blk = acc[:, j * LANES:(j + 1) * LANES]
                rot = (blk * cos + pltpu.roll(blk, LANES - ROPE_DIM // 2, 1) * sa
                       + pltpu.roll(blk, ROPE_DIM // 2, 1) * sb)
                o_ref[:, c * WIDTH + j * LANES:c * WIDTH + (j + 1) * LANES] = rot.astype(BF16)
        else:
            o_ref[:, c * WIDTH:(c + 1) * WIDTH] = acc.astype(BF16)


def _proj(h, gain, w_b, cos, sa, sb, seq, tm=512):
    m, d = h.shape
    nt = seq // tm
    row = lambda i: (i, 0)
    tab = lambda i: (i % nt, 0)
    return pl.pallas_call(
        _proj_kernel,
        grid=(m // tm,),
        in_specs=[pl.BlockSpec((tm, d), row),
                  pl.BlockSpec((1, d), lambda i: (0, 0)),
                  pl.BlockSpec((d, IN_COLS), lambda i: (0, 0)),
                  pl.BlockSpec((tm, LANES), tab),
                  pl.BlockSpec((tm, LANES), tab),
                  pl.BlockSpec((tm, LANES), tab)],
        out_specs=pl.BlockSpec((tm, IN_COLS), row),
        out_shape=jax.ShapeDtypeStruct((m, IN_COLS), BF16),
        compiler_params=_params("arbitrary"),
        name="proj",
    )(h, gain.reshape(1, d), w_b, cos, sa, sb)


def _rope_tables(seq):
    half = ROPE_DIM // 2
    inv = jnp.power(ROPE_THETA, -jnp.arange(half, dtype=F32) * (2.0 / ROPE_DIM))
    ang = jnp.arange(seq).astype(F32)[:, None] * inv[None, :]
    cos, sin = jnp.cos(ang), jnp.sin(ang)
    ch = jnp.arange(LANES) % HEAD_DIM
    cos_l = jnp.take(cos, ch % half, axis=1)
    sin_l = jnp.take(sin, ch % half, axis=1)
    cos_t = jnp.where(ch < ROPE_DIM, cos_l, 1.0)
    sa = jnp.where(ch < half, -sin_l, 0.0)
    sb = jnp.where((ch >= half) & (ch < ROPE_DIM), sin_l, 0.0)
    return cos_t, sa, sb


def _dswa_kernel(*refs, lc, has_prev, is_last):
    q_ref, kc_ref, kp_ref, vc_ref, vp_ref = refs[:5]
    rest = refs[5:]
    if has_prev:
        po_ref, pl_ref = rest[:2]
        rest = rest[2:]
    o_ref = rest[0]
    l_ref = None if is_last else rest[1]

    w = DSWA_W
    first_chunk = pl.program_id(2) == 0
    head_a = lax.broadcasted_iota(jnp.int32, (1, LANES), 1) < HEAD_DIM
    ri = lax.broadcasted_iota(jnp.int32, (2 * w, 2 * w), 0) & (w - 1)
    cj = lax.broadcasted_iota(jnp.int32, (2 * w, 2 * w), 1)
    dist = w + ri - cj
    band = (dist >= 0) & (dist <= w)
    band0 = band & ((cj >= w) | jnp.logical_not(first_chunk))

    def unit(row0, hp, kk, vv, mask):
        lanes = slice(hp * LANES, (hp + 1) * LANES)
        qb = q_ref[0, pl.ds(row0, w), lanes]
        zero = jnp.zeros_like(qb)
        qs = jnp.concatenate([jnp.where(head_a, qb, zero), jnp.where(head_a, zero, qb)], axis=0)
        s = lax.dot_general(qs, kk, (((1,), (1,)), ((), ())), preferred_element_type=F32)
        s = jnp.where(mask, s, NEG)
        m = jnp.max(s, axis=1, keepdims=True)
        p = jnp.exp2(s - m)
        den = jnp.sum(p, axis=1, keepdims=True)
        o = jnp.dot(p.astype(BF16), vv, preferred_element_type=F32) / den
        lse = m + jnp.log2(den)
        o_t = jnp.where(head_a, o[:w], o[w:])
        lse_t = jnp.where(head_a, lse[:w], lse[w:])
        if has_prev:
            o_p = po_ref[0, pl.ds(row0, w), lanes]
            lse_p = pl_ref[0, pl.ds(row0, w), lanes]
            top = jnp.maximum(lse_t, lse_p)
            wa, wb = jnp.exp2(lse_t - top), jnp.exp2(lse_p - top)
            tot = wa + wb
            o_t = (wa * o_t + wb * o_p) / tot
            lse_t = top + jnp.log2(tot)
        o_ref[0, pl.ds(row0, w), lanes] = o_t.astype(o_ref.dtype)
        if not is_last:
            l_ref[0, pl.ds(row0, w), lanes] = lse_t

    for hp in range(WIDTH // LANES):
        lanes = slice(hp * LANES, (hp + 1) * LANES)
        kk0 = jnp.concatenate([kp_ref[0, :, lanes], kc_ref[0, 0:w, lanes]], axis=0)
        vv0 = jnp.concatenate([vp_ref[0, :, lanes], vc_ref[0, 0:w, lanes]], axis=0)
        unit(0, hp, kk0, vv0, band0)

    def body(n, carry):
        row0 = pl.multiple_of(n * w, w)
        prev0 = pl.multiple_of((n - 1) * w, w)
        for hp in range(WIDTH // LANES):
            lanes = slice(hp * LANES, (hp + 1) * LANES)
            unit(row0, hp, kc_ref[0, pl.ds(prev0, 2 * w), lanes],
                 vc_ref[0, pl.ds(prev0, 2 * w), lanes], band)
        return carry

    lax.fori_loop(1, lc // w, body, 0)


def _dswa_group(u3, prev, dilation, is_last):
    b, s, _ = u3.shape
    d = dilation
    sub = s // d
    lc = min(sub, 1024)
    nc = sub // lc
    per = lc // DSWA_W
    uv = u3.reshape(b, sub, d * IN_COLS)
    ncol = IN_COLS // WIDTH

    def cur(split):
        return pl.BlockSpec((1, lc, WIDTH), lambda bi, r, c: (bi, c, r * ncol + split))

    def prv(split):
        return pl.BlockSpec((1, DSWA_W, WIDTH),
                            lambda bi, r, c: (bi, jnp.maximum(c * per - 1, 0), r * ncol + split))

    nat = pl.BlockSpec((1, lc, WIDTH), lambda bi, r, c: (bi, c, r))
    in_specs = [cur(COL_QA), cur(COL_KA), prv(COL_KA), cur(COL_VA), prv(COL_VA)]
    args = [uv, uv, uv, uv, uv]
    if prev is not None:
        in_specs += [nat, nat]
        args += [prev[0].reshape(b, sub, d * WIDTH), prev[1].reshape(b, sub, d * WIDTH)]
    if is_last:
        out_specs = nat
        out_shape = jax.ShapeDtypeStruct((b, sub, d * WIDTH), BF16)
    else:
        out_specs = [nat, nat]
        out_shape = [jax.ShapeDtypeStruct((b, sub, d * WIDTH), F32)] * 2
    out = pl.pallas_call(
        functools.partial(_dswa_kernel, lc=lc, has_prev=prev is not None, is_last=is_last),
        grid=(b, d, nc),
        in_specs=in_specs,
        out_specs=out_specs,
        out_shape=out_shape,
        compiler_params=_params("arbitrary", "arbitrary", "arbitrary"),
        name=f"dswa_d{d}",
    )(*args)
    if is_last:
        return out.reshape(b, s, WIDTH)
    return out[0].reshape(b, s, WIDTH), out[1].reshape(b, s, WIDTH)


def _dswa(u3):
    prev = None
    for i, d in enumerate(DSWA_DILATIONS):
        prev = _dswa_group(u3, prev, d, is_last=i == len(DSWA_DILATIONS) - 1)
    return prev


def _diff_kernel(lam_ref, gain_ref, q_ref, k_ref, vt_ref, o_ref, m_sc, l_sc, acc_sc, sa_sc, sb_sc, *, qb,
                 lambda_init):
    qi = pl.program_id(2)
    head_a = lax.broadcasted_iota(jnp.int32, (1, LANES), 1) < HEAD_DIM
    q = q_ref[0]
    zero = jnp.zeros_like(q)
    qs = jnp.concatenate([jnp.where(head_a, q, zero), jnp.where(head_a, zero, q)], axis=0)

    m_sc[...] = jnp.full(m_sc.shape, NEG, F32)
    l_sc[...] = jnp.zeros(l_sc.shape, F32)
    acc_sc[...] = jnp.zeros(acc_sc.shape, F32)

    def scores(blk):
        kb = k_ref[0, pl.ds(pl.multiple_of(blk * qb, qb), qb), :]
        return lax.dot_general(kb, qs, (((1,), (1,)), ((), ())), preferred_element_type=F32)

    def step(blk, s_ref, mask):
        s = s_ref[...]
        if mask is not None:
            s = jnp.where(mask, s, NEG)
        m_old = m_sc[...]
        m_new = jnp.maximum(m_old, jnp.max(s, axis=0, keepdims=True))
        alpha = jnp.exp2(m_old - m_new)
        p = jnp.exp2(s - m_new)
        l_sc[...] = alpha * l_sc[...] + jnp.sum(p, axis=0, keepdims=True)
        acc_sc[...] = alpha * acc_sc[...] + jnp.dot(vt_ref[0, 0, blk], p.astype(BF16),
                                                    preferred_element_type=F32)
        m_sc[...] = m_new

    last = jnp.maximum(qi - 1, 0)
    key = lax.broadcasted_iota(jnp.int32, (qb, 2 * qb), 0)
    qry = lax.broadcasted_iota(jnp.int32, (qb, 2 * qb), 1) & (qb - 1)
    sa_sc[...] = scores(qi)
    sb_sc[...] = scores(0)
    step(qi, sa_sc, key <= qry)

    def pair(j, carry):
        b0 = 2 * j
        sa_sc[...] = scores(b0 + 1)
        step(b0, sb_sc, None)
        sb_sc[...] = scores(jnp.minimum(b0 + 2, last))
        step(b0 + 1, sa_sc, None)
        return carry

    lax.fori_loop(0, qi // 2, pair, 0)

    @pl.when(qi % 2 == 1)
    def _():
        step(qi - 1, sb_sc, None)

    o_all = acc_sc[...] / l_sc[...]
    lam_p = lam_ref[...]
    lam = (jnp.exp(jnp.sum(lam_p[0:1] * lam_p[1:2], axis=1, keepdims=True))
           - jnp.exp(jnp.sum(lam_p[2:3] * lam_p[3:4], axis=1, keepdims=True)) + lambda_init)
    o = o_all[:, :qb] - lam * o_all[:, qb:]
    o = o * lax.rsqrt(jnp.mean(o * o, axis=0, keepdims=True) + SUBLN_EPS) * gain_ref[...]
    o_ref[0] = (o * (1.0 - lambda_init)).T.astype(o_ref.dtype)


def _diff(u3, lam_p, subln_gain, lambda_init, qb=256):
    b, s, _ = u3.shape
    per = WIDTH // LANES
    nb = s // qb
    vt = u3[:, :, COL_VB * WIDTH:(COL_VB + 1) * WIDTH].reshape(b, nb, qb, N_HEADS_DIFF, LANES)
    vt = jnp.transpose(vt, (0, 3, 1, 4, 2))
    return pl.pallas_call(
        functools.partial(_diff_kernel, qb=qb, lambda_init=lambda_init),
        grid=(b, N_HEADS_DIFF, nb),
        in_specs=[pl.BlockSpec(lam_p.shape, lambda bi, h, qi: (0, 0)),
                  pl.BlockSpec((LANES, 1), lambda bi, h, qi: (0, 0)),
                  pl.BlockSpec((1, qb, LANES), lambda bi, h, qi: (bi, qi, COL_QB * per + h)),
                  pl.BlockSpec((1, s, LANES), lambda bi, h, qi: (bi, 0, COL_KB * per + h)),
                  pl.BlockSpec((1, 1, nb, LANES, qb), lambda bi, h, qi: (bi, h, 0, 0, 0))],
        out_specs=pl.BlockSpec((1, qb, LANES), lambda bi, h, qi: (bi, qi, h)),
        out_shape=jax.ShapeDtypeStruct((b, s, WIDTH), BF16),
        scratch_shapes=[pltpu.VMEM((1, 2 * qb), F32), pltpu.VMEM((1, 2 * qb), F32),
                        pltpu.VMEM((LANES, 2 * qb), F32),
                        pltpu.VMEM((qb, 2 * qb), F32), pltpu.VMEM((qb, 2 * qb), F32)],
        compiler_params=_params("arbitrary", "arbitrary", "arbitrary"),
        name="diff",
    )(lam_p, subln_gain.reshape(LANES, 1), u3, u3, vt)


def _sigmoid(x):
    return 1.0 / (1.0 + jnp.exp(-x))


def _post_kernel(h_ref, ga_ref, gb_ref, oa_ref, ob_ref, p_ref, wo_ref, ng_ref, wg_ref, wp_ref, fg_ref,
                 o_ref, *, final):
    ga = ga_ref[...].astype(F32)
    gb = gb_ref[...].astype(F32)
    ya = (oa_ref[...].astype(F32) * (ga * _sigmoid(ga))).astype(BF16)
    yb = (ob_ref[...].astype(F32) * (gb * _sigmoid(gb))).astype(BF16)
    y = (jnp.dot(ya, wo_ref[0:WIDTH, :], preferred_element_type=F32)
         + jnp.dot(yb, wo_ref[WIDTH:2 * WIDTH, :], preferred_element_type=F32))
    h1 = h_ref[...] + y
    n = h1 * lax.rsqrt(jnp.mean(h1 * h1, axis=-1, keepdims=True) + RMS_EPS) * ng_ref[...]
    gate = _sigmoid(jnp.dot(n.astype(BF16), wg_ref[...], preferred_element_type=F32))
    ple = jnp.dot(p_ref[...].astype(BF16), wp_ref[...], preferred_element_type=F32)
    h2 = h1 + ple * gate
    if final:
        h2 = h2 * lax.rsqrt(jnp.mean(h2 * h2, axis=-1, keepdims=True) + RMS_EPS) * fg_ref[...]
    o_ref[...] = h2


def _post(h, u, oa, ob, p_i, wo_b, ple_gain, wg_b, wp_b, final_gain, final, tm=512):
    m, d = h.shape
    row = lambda i: (i, 0)
    const = lambda i: (0, 0)
    return pl.pallas_call(
        functools.partial(_post_kernel, final=final),
        grid=(m // tm,),
        in_specs=[pl.BlockSpec((tm, d), row),
                  pl.BlockSpec((tm, WIDTH), lambda i: (i, COL_GA)),
                  pl.BlockSpec((tm, WIDTH), lambda i: (i, COL_GB)),
                  pl.BlockSpec((tm, WIDTH), row),
                  pl.BlockSpec((tm, WIDTH), row),
                  pl.BlockSpec((tm, p_i.shape[1]), row),
                  pl.BlockSpec(wo_b.shape, const),
                  pl.BlockSpec((1, d), const),
                  pl.BlockSpec(wg_b.shape, const),
                  pl.BlockSpec(wp_b.shape, const),
                  pl.BlockSpec((1, d), const)],
        out_specs=pl.BlockSpec((tm, d), row),
        out_shape=jax.ShapeDtypeStruct((m, d), F32),
        compiler_params=_params("arbitrary"),
        name="post",
    )(h, u, u, oa, ob, p_i, wo_b, ple_gain.reshape(1, d), wg_b, wp_b, final_gain.reshape(1, d))


def kernel(x, p, attn_norm_gain, w_in, w_out, lambda_q1, lambda_k1, lambda_q2, lambda_k2, subln_gain,
           ple_norm_gain, w_ple_gate, w_ple, final_norm_gain):
    b, s, d = x.shape
    depth = w_in.shape[0]
    h = x.reshape(b * s, d)
    cos, sa, sb = _rope_tables(s)
    col = jnp.arange(IN_COLS) // WIDTH
    q_scale = jnp.where((col == COL_QA) | (col == COL_QB), HEAD_DIM ** -0.5 * LOG2_E, 1.0).astype(F32)
    for i in range(depth):
        lambda_init = 0.8 - 0.6 * math.exp(-0.3 * i)
        w_in_b = (w_in[i] * q_scale[None, :]).astype(BF16)
        u = _proj(h, attn_norm_gain[i], w_in_b, cos, sa, sb, s)
        u3 = u.reshape(b, s, IN_COLS)
        oa = _dswa(u3)
        lam_p = jnp.stack([lambda_q1[i], lambda_k1[i], lambda_q2[i], lambda_k2[i]]).astype(F32)
        ob = _diff(u3, lam_p, subln_gain[i], lambda_init)
        h = _post(h, u, oa.reshape(b * s, WIDTH), ob.reshape(b * s, WIDTH), p[i].reshape(b * s, -1),
                  w_out[i].astype(BF16), ple_norm_gain[i], w_ple_gate[i].astype(BF16),
                  w_ple[i].astype(BF16), final_norm_gain, final=i == depth - 1)
    return h.reshape(b, s, d)
```

```python
import functools
import math

import jax
import jax.numpy as jnp
from jax import lax
from jax.experimental import pallas as pl
from jax.experimental.pallas import tpu as pltpu

HEAD_DIM = 64
LANES = 128
DSWA_W = 128
N_HEADS_DIFF = 4
DIFF_QB = 256
WIDTH = 512
IN_COLS = 8 * WIDTH
ROPE_THETA = 500000.0
ROPE_DIM = HEAD_DIM // 4
RMS_EPS = 1e-6
SUBLN_EPS = 1e-5
NEG = -1e30
LOG2_E = math.log2(math.e)
VMEM_LIMIT = 48 * 1024 * 1024

COL_QA, COL_KA, COL_VA, COL_GA, COL_QB, COL_KB, COL_VB, COL_GB = range(8)
ROPE_SPLITS = (COL_QA, COL_KA, COL_QB, COL_KB)

BF16 = jnp.bfloat16
F32 = jnp.float32


def _params(*sem):
    return pltpu.CompilerParams(dimension_semantics=sem, vmem_limit_bytes=VMEM_LIMIT)


def _proj_kernel(x_ref, g_ref, w_ref, cos_ref, sa_ref, sb_ref, o_ref, d4_ref, d16_ref, vt_ref, buf_sc, *, qb):
    tm = x_ref.shape[0]
    x = x_ref[...]
    ms = jnp.mean(x * x, axis=-1, keepdims=True)
    xn = (x * lax.rsqrt(ms + RMS_EPS) * g_ref[...]).astype(BF16)
    cos, sa, sb = cos_ref[...], sa_ref[...], sb_ref[...]
    for c in range(IN_COLS // WIDTH):
        cols = slice(c * WIDTH, (c + 1) * WIDTH)
        acc = jnp.dot(xn, w_ref[:, cols], preferred_element_type=F32)
        if c in ROPE_SPLITS:
            blocks = []
            for j in range(WIDTH // LANES):
                blk = acc[:, j * LANES:(j + 1) * LANES]
                blocks.append(blk * cos + pltpu.roll(blk, LANES - ROPE_DIM // 2, 1) * sa
                              + pltpu.roll(blk, ROPE_DIM // 2, 1) * sb)
            acc = jnp.concatenate(blocks, axis=1)
        o_ref[:, cols] = acc.astype(BF16)
        if c in (COL_QA, COL_KA, COL_VA):
            for j in range(WIDTH // LANES):
                buf_sc[j] = acc[:, j * LANES:(j + 1) * LANES]
            for dref in (d4_ref, d16_ref):
                d = dref.shape[1]
                for r in range(d):
                    for j in range(WIDTH // LANES):
                        lanes = slice(c * WIDTH + j * LANES, c * WIDTH + (j + 1) * LANES)
                        dref[0, r, :, lanes] = buf_sc[j, pl.ds(r, tm // d, stride=d), :].astype(BF16)
        if c == COL_VB:
            for kb in range(tm // qb):
                for hd in range(N_HEADS_DIFF):
                    blk = acc[kb * qb:(kb + 1) * qb, hd * LANES:(hd + 1) * LANES]
                    vt_ref[0, hd, kb] = blk.T.astype(BF16)


def _proj(h, gain, w_b, cos, sa, sb, batch, qb, tm=512):
    m, d = h.shape
    seq = m // batch
    nt = seq // tm
    row = lambda i: (i, 0)
    tab = lambda i: (i % nt, 0)
    res = lambda i: (i // nt, 0, i % nt, 0)
    return pl.pallas_call(
        functools.partial(_proj_kernel, qb=qb),
        grid=(m // tm,),
        in_specs=[pl.BlockSpec((tm, d), row),
                  pl.BlockSpec((1, d), lambda i: (0, 0)),
                  pl.BlockSpec((d, IN_COLS), lambda i: (0, 0)),
                  pl.BlockSpec((tm, LANES), tab),
                  pl.BlockSpec((tm, LANES), tab),
                  pl.BlockSpec((tm, LANES), tab)],
        out_specs=[pl.BlockSpec((tm, IN_COLS), row),
                   pl.BlockSpec((1, 4, tm // 4, 3 * WIDTH), res),
                   pl.BlockSpec((1, 16, tm // 16, 3 * WIDTH), res),
                   pl.BlockSpec((1, N_HEADS_DIFF, tm // qb, LANES, qb), lambda i: (i // nt, 0, i % nt, 0, 0))],
        out_shape=[jax.ShapeDtypeStruct((m, IN_COLS), BF16),
                   jax.ShapeDtypeStruct((batch, 4, seq // 4, 3 * WIDTH), BF16),
                   jax.ShapeDtypeStruct((batch, 16, seq // 16, 3 * WIDTH), BF16),
                   jax.ShapeDtypeStruct((batch, N_HEADS_DIFF, seq // qb, LANES, qb), BF16)],
        scratch_shapes=[pltpu.VMEM((WIDTH // LANES, tm, LANES), F32)],
        compiler_params=_params("arbitrary"),
        name="proj",
    )(h, gain.reshape(1, d), w_b, cos, sa, sb)


def _rope_tables(seq):
    half = ROPE_DIM // 2
    inv = jnp.power(ROPE_THETA, -jnp.arange(half, dtype=F32) * (2.0 / ROPE_DIM))
    ang = jnp.arange(seq).astype(F32)[:, None] * inv[None, :]
    cos, sin = jnp.cos(ang), jnp.sin(ang)
    ch = jnp.arange(LANES) % HEAD_DIM
    cos_l = jnp.take(cos, ch % half, axis=1)
    sin_l = jnp.take(sin, ch % half, axis=1)
    cos_t = jnp.where(ch < ROPE_DIM, cos_l, 1.0)
    sa = jnp.where(ch < half, -sin_l, 0.0)
    sb = jnp.where((ch >= half) & (ch < ROPE_DIM), sin_l, 0.0)
    return cos_t, sa, sb


def _dswa_kernel(q_ref, kc_ref, kp_ref, vc_ref, vp_ref, o_ref, l_ref, *, lc):
    w = DSWA_W
    first_chunk = pl.program_id(2) == 0
    head_a = lax.broadcasted_iota(jnp.int32, (1, LANES), 1) < HEAD_DIM
    ri = lax.broadcasted_iota(jnp.int32, (2 * w, 2 * w), 0) & (w - 1)
    cj = lax.broadcasted_iota(jnp.int32, (2 * w, 2 * w), 1)
    dist = w + ri - cj
    band = (dist >= 0) & (dist <= w)
    band0 = band & ((cj >= w) | jnp.logical_not(first_chunk))

    def unit(row0, hp, kk, vv, mask):
        lanes = slice(hp * LANES, (hp + 1) * LANES)
        qb = q_ref[0, 0, pl.ds(row0, w), lanes]
        zero = jnp.zeros_like(qb)
        qs = jnp.concatenate([jnp.where(head_a, qb, zero), jnp.where(head_a, zero, qb)], axis=0)
        s = lax.dot_general(qs, kk, (((1,), (1,)), ((), ())), preferred_element_type=F32)
        s = jnp.where(mask, s, NEG)
        m = jnp.max(s, axis=1, keepdims=True)
        p = jnp.exp2(s - m)
        den = jnp.sum(p, axis=1, keepdims=True)
        o = jnp.dot(p.astype(BF16), vv, preferred_element_type=F32) / den
        lse = m + jnp.log2(den)
        o_ref[0, 0, pl.ds(row0, w), lanes] = jnp.where(head_a, o[:w], o[w:]).astype(o_ref.dtype)
        l_ref[0, 0, pl.ds(row0, w), lanes] = jnp.where(head_a, lse[:w], lse[w:])

    for hp in range(WIDTH // LANES):
        lanes = slice(hp * LANES, (hp + 1) * LANES)
        kk0 = jnp.concatenate([kp_ref[0, 0, :, lanes], kc_ref[0, 0, 0:w, lanes]], axis=0)
        vv0 = jnp.concatenate([vp_ref[0, 0, :, lanes], vc_ref[0, 0, 0:w, lanes]], axis=0)
        unit(0, hp, kk0, vv0, band0)

    def body(n, carry):
        row0 = pl.multiple_of(n * w, w)
        prev0 = pl.multiple_of((n - 1) * w, w)
        for hp in range(WIDTH // LANES):
            lanes = slice(hp * LANES, (hp + 1) * LANES)
            unit(row0, hp, kc_ref[0, 0, pl.ds(prev0, 2 * w), lanes],
                 vc_ref[0, 0, pl.ds(prev0, 2 * w), lanes], band)
        return carry

    lax.fori_loop(1, lc // w, body, 0)


def _dswa_group(qkv, splits):
    b, d, sub, _ = qkv.shape
    lc = min(sub, 1024)
    per = lc // DSWA_W
    sq, sk, sv = splits

    def cur(split):
        return pl.BlockSpec((1, 1, lc, WIDTH), lambda bi, r, c: (bi, r, c, split))

    def prv(split):
        return pl.BlockSpec((1, 1, DSWA_W, WIDTH), lambda bi, r, c: (bi, r, jnp.maximum(c * per - 1, 0), split))

    out = pl.BlockSpec((1, 1, lc, WIDTH), lambda bi, r, c: (bi, r, c, 0))
    return pl.pallas_call(
        functools.partial(_dswa_kernel, lc=lc),
        grid=(b, d, sub // lc),
        in_specs=[cur(sq), cur(sk), prv(sk), cur(sv), prv(sv)],
        out_specs=[out, out],
        out_shape=[jax.ShapeDtypeStruct((b, d, sub, WIDTH), BF16),
                   jax.ShapeDtypeStruct((b, d, sub, WIDTH), F32)],
        compiler_params=_params("arbitrary", "arbitrary", "arbitrary"),
        name=f"dswa_d{d}",
    )(qkv, qkv, qkv, qkv, qkv)


def _diff_kernel(lam_ref, gain_ref, q_ref, k_ref, vt_ref, o_ref, m_sc, l_sc, acc_sc, sa_sc, sb_sc, *, qb,
                 lambda_init):
    qi = pl.program_id(2)
    head_a = lax.broadcasted_iota(jnp.int32, (1, LANES), 1) < HEAD_DIM
    q = q_ref[0]
    zero = jnp.zeros_like(q)
    qs = jnp.concatenate([jnp.where(head_a, q, zero), jnp.where(head_a, zero, q)], axis=0)

    m_sc[...] = jnp.full(m_sc.shape, NEG, F32)
    l_sc[...] = jnp.zeros(l_sc.shape, F32)
    acc_sc[...] = jnp.zeros(acc_sc.shape, F32)

    def scores(blk):
        kb = k_ref[0, pl.ds(pl.multiple_of(blk * qb, qb), qb), :]
        return lax.dot_general(kb, qs, (((1,), (1,)), ((), ())), preferred_element_type=F32)

    def step(blk, s_ref, mask):
        s = s_ref[...]
        if mask is not None:
            s = jnp.where(mask, s, NEG)
        m_old = m_sc[...]
        m_new = jnp.maximum(m_old, jnp.max(s, axis=0, keepdims=True))
        alpha = jnp.exp2(m_old - m_new)
        p = jnp.exp2(s - m_new)
        l_sc[...] = alpha * l_sc[...] + jnp.sum(p, axis=0, keepdims=True)
        acc_sc[...] = alpha * acc_sc[...] + jnp.dot(vt_ref[0, 0, blk], p.astype(BF16),
                                                    preferred_element_type=F32)
        m_sc[...] = m_new

    last = jnp.maximum(qi - 1, 0)
    key = lax.broadcasted_iota(jnp.int32, (qb, 2 * qb), 0)
    qry = lax.broadcasted_iota(jnp.int32, (qb, 2 * qb), 1) & (qb - 1)
    sa_sc[...] = scores(qi)
    sb_sc[...] = scores(0)
    step(qi, sa_sc, key <= qry)

    def pair(j, carry):
        b0 = 2 * j
        sa_sc[...] = scores(b0 + 1)
        step(b0, sb_sc, None)
        sb_sc[...] = scores(jnp.minimum(b0 + 2, last))
        step(b0 + 1, sa_sc, None)
        return carry

    lax.fori_loop(0, qi // 2, pair, 0)

    @pl.when(qi % 2 == 1)
    def _():
        step(qi - 1, sb_sc, None)

    o_all = acc_sc[...] / l_sc[...]
    lam_p = lam_ref[...]
    lam = (jnp.exp(jnp.sum(lam_p[0:1] * lam_p[1:2], axis=1, keepdims=True))
           - jnp.exp(jnp.sum(lam_p[2:3] * lam_p[3:4], axis=1, keepdims=True)) + lambda_init)
    o = o_all[:, :qb] - lam * o_all[:, qb:]
    o = o * lax.rsqrt(jnp.mean(o * o, axis=0, keepdims=True) + SUBLN_EPS) * gain_ref[...]
    o_ref[0] = (o * (1.0 - lambda_init)).T.astype(o_ref.dtype)


def _diff(u3, vt, lam_p, subln_gain, lambda_init, qb):
    b, s, _ = u3.shape
    per = WIDTH // LANES
    nb = s // qb
    return pl.pallas_call(
        functools.partial(_diff_kernel, qb=qb, lambda_init=lambda_init),
        grid=(b, N_HEADS_DIFF, nb),
        in_specs=[pl.BlockSpec(lam_p.shape, lambda bi, h, qi: (0, 0)),
                  pl.BlockSpec((LANES, 1), lambda bi, h, qi: (0, 0)),
                  pl.BlockSpec((1, qb, LANES), lambda bi, h, qi: (bi, qi, COL_QB * per + h)),
                  pl.BlockSpec((1, s, LANES), lambda bi, h, qi: (bi, 0, COL_KB * per + h)),
                  pl.BlockSpec((1, 1, nb, LANES, qb), lambda bi, h, qi: (bi, h, 0, 0, 0))],
        out_specs=pl.BlockSpec((1, qb, LANES), lambda bi, h, qi: (bi, qi, h)),
        out_shape=jax.ShapeDtypeStruct((b, s, WIDTH), BF16),
        scratch_shapes=[pltpu.VMEM((1, 2 * qb), F32), pltpu.VMEM((1, 2 * qb), F32),
                        pltpu.VMEM((LANES, 2 * qb), F32),
                        pltpu.VMEM((qb, 2 * qb), F32), pltpu.VMEM((qb, 2 * qb), F32)],
        compiler_params=_params("arbitrary", "arbitrary", "arbitrary"),
        name="diff",
    )(lam_p, subln_gain.reshape(LANES, 1), u3, u3, vt)


def _sigmoid(x):
    return 1.0 / (1.0 + jnp.exp(-x))


def _post_kernel(h_ref, ga_ref, gb_ref, o1_ref, l1_ref, o4_ref, l4_ref, o16_ref, l16_ref, ob_ref, p_ref,
                 wo_ref, ng_ref, wg_ref, wp_ref, fg_ref, out_ref, on_sc, ln_sc, *, final):
    tm = h_ref.shape[0]

    def natural(o_ref, l_ref, lanes):
        d = o_ref.shape[1]
        for r in range(d):
            on_sc[pl.ds(r, tm // d, stride=d), :] = o_ref[0, r, :, lanes].astype(F32)
            ln_sc[pl.ds(r, tm // d, stride=d), :] = l_ref[0, r, :, lanes]
        return on_sc[...], ln_sc[...]

    merged = []
    for j in range(WIDTH // LANES):
        lanes = slice(j * LANES, (j + 1) * LANES)
        o1, l1 = o1_ref[:, lanes].astype(F32), l1_ref[:, lanes]
        o4, l4 = natural(o4_ref, l4_ref, lanes)
        top = jnp.maximum(l1, l4)
        w1, w4 = jnp.exp2(l1 - top), jnp.exp2(l4 - top)
        num, den = w1 * o1 + w4 * o4, w1 + w4
        o16, l16 = natural(o16_ref, l16_ref, lanes)
        top2 = jnp.maximum(top, l16)
        sc, w16 = jnp.exp2(top - top2), jnp.exp2(l16 - top2)
        merged.append((sc * num + w16 * o16) / (sc * den + w16))
    oa = jnp.concatenate(merged, axis=1)

    ga = ga_ref[...].astype(F32)
    gb = gb_ref[...].astype(F32)
    ya = (oa * (ga * _sigmoid(ga))).astype(BF16)
    yb = (ob_ref[...].astype(F32) * (gb * _sigmoid(gb))).astype(BF16)
    y = (jnp.dot(ya, wo_ref[0:WIDTH, :], preferred_element_type=F32)
         + jnp.dot(yb, wo_ref[WIDTH:2 * WIDTH, :], preferred_element_type=F32))
    h1 = h_ref[...] + y
    n = h1 * lax.rsqrt(jnp.mean(h1 * h1, axis=-1, keepdims=True) + RMS_EPS) * ng_ref[...]
    gate = _sigmoid(jnp.dot(n.astype(BF16), wg_ref[...], preferred_element_type=F32))
    ple = jnp.dot(p_ref[...].astype(BF16), wp_ref[...], preferred_element_type=F32)
    h2 = h1 + ple * gate
    if final:
        h2 = h2 * lax.rsqrt(jnp.mean(h2 * h2, axis=-1, keepdims=True) + RMS_EPS) * fg_ref[...]
    out_ref[...] = h2


def _post(h, u, g1, g4, g16, ob, p_i, wo_b, ple_gain, wg_b, wp_b, final_gain, final, batch, tm=512):
    m, d = h.shape
    nt = m // batch // tm
    row = lambda i: (i, 0)
    const = lambda i: (0, 0)
    res = lambda i: (i // nt, 0, i % nt, 0)
    grp = lambda dil: pl.BlockSpec((1, dil, tm // dil, WIDTH), res)
    return pl.pallas_call(
        functools.partial(_post_kernel, final=final),
        grid=(m // tm,),
        in_specs=[pl.BlockSpec((tm, d), row),
                  pl.BlockSpec((tm, WIDTH), lambda i: (i, COL_GA)),
                  pl.BlockSpec((tm, WIDTH), lambda i: (i, COL_GB)),
                  pl.BlockSpec((tm, WIDTH), row), pl.BlockSpec((tm, WIDTH), row),
                  grp(4), grp(4), grp(16), grp(16),
                  pl.BlockSpec((tm, WIDTH), row),
                  pl.BlockSpec((tm, p_i.shape[1]), row),
                  pl.BlockSpec(wo_b.shape, const),
                  pl.BlockSpec((1, d), const),
                  pl.BlockSpec(wg_b.shape, const),
                  pl.BlockSpec(wp_b.shape, const),
                  pl.BlockSpec((1, d), const)],
        out_specs=pl.BlockSpec((tm, d), row),
        out_shape=jax.ShapeDtypeStruct((m, d), F32),
        scratch_shapes=[pltpu.VMEM((tm, LANES), F32), pltpu.VMEM((tm, LANES), F32)],
        compiler_params=_params("arbitrary"),
        name="post",
    )(h, u, u, g1[0].reshape(m, WIDTH), g1[1].reshape(m, WIDTH), g4[0], g4[1], g16[0], g16[1], ob, p_i,
      wo_b, ple_gain.reshape(1, d), wg_b, wp_b, final_gain.reshape(1, d))


def kernel(x, p, attn_norm_gain, w_in, w_out, lambda_q1, lambda_k1, lambda_q2, lambda_k2, subln_gain,
           ple_norm_gain, w_ple_gate, w_ple, final_norm_gain):
    b, s, d = x.shape
    depth = w_in.shape[0]
    h = x.reshape(b * s, d)
    cos, sa, sb = _rope_tables(s)
    col = jnp.arange(IN_COLS) // WIDTH
    q_scale = jnp.where((col == COL_QA) | (col == COL_QB), HEAD_DIM ** -0.5 * LOG2_E, 1.0).astype(F32)
    a_splits = (COL_QA, COL_KA, COL_VA)
    for i in range(depth):
        lambda_init = 0.8 - 0.6 * math.exp(-0.3 * i)
        w_in_b = (w_in[i] * q_scale[None, :]).astype(BF16)
        u, qkv4, qkv16, vt = _proj(h, attn_norm_gain[i], w_in_b, cos, sa, sb, b, DIFF_QB)
        u3 = u.reshape(b, s, IN_COLS)
        g1 = _dswa_group(u3.reshape(b, 1, s, IN_COLS), a_splits)
        g4 = _dswa_group(qkv4, a_splits)
        g16 = _dswa_group(qkv16, a_splits)
        lam_p = jnp.stack([lambda_q1[i], lambda_k1[i], lambda_q2[i], lambda_k2[i]]).astype(F32)
        ob = _diff(u3, vt, lam_p, subln_gain[i], lambda_init, DIFF_QB)
        h = _post(h, u, g1, g4, g16, ob.reshape(b * s, WIDTH), p[i].reshape(b * s, -1),
                  w_out[i].astype(BF16), ple_norm_gain[i], w_ple_gate[i].astype(BF16),
                  w_ple[i].astype(BF16), final_norm_gain, final=i == depth - 1, batch=b)
    return h.reshape(b, s, d)
```

```python
import functools
import math

import jax
import jax.numpy as jnp
from jax import lax
from jax.experimental import pallas as pl
from jax.experimental.pallas import tpu as pltpu

HEAD_DIM = 64
LANES = 128
DSWA_W = 128
N_HEADS_DIFF = 4
DIFF_QB = 512
WIDTH = 512
IN_COLS = 8 * WIDTH
ROPE_THETA = 500000.0
ROPE_DIM = HEAD_DIM // 4
RMS_EPS = 1e-6
SUBLN_EPS = 1e-5
NEG = -1e30
LOG2_E = math.log2(math.e)
VMEM_LIMIT = 48 * 1024 * 1024

COL_QA, COL_KA, COL_VA, COL_GA, COL_QB, COL_KB, COL_VB, COL_GB = range(8)
ROPE_SPLITS = (COL_QA, COL_KA, COL_QB, COL_KB)

BF16 = jnp.bfloat16
F32 = jnp.float32


def _params(*sem):
    return pltpu.CompilerParams(dimension_semantics=sem, vmem_limit_bytes=VMEM_LIMIT)


def _proj_kernel(x_ref, g_ref, w_ref, cos_ref, sa_ref, sb_ref, o_ref, d4_ref, d16_ref, vt_ref, buf_sc, *, qb):
    tm = x_ref.shape[0]
    x = x_ref[...]
    ms = jnp.mean(x * x, axis=-1, keepdims=True)
    xn = (x * lax.rsqrt(ms + RMS_EPS) * g_ref[...]).astype(BF16)
    cos, sa, sb = cos_ref[...], sa_ref[...], sb_ref[...]
    for c in range(IN_COLS // WIDTH):
        cols = slice(c * WIDTH, (c + 1) * WIDTH)
        acc = jnp.dot(xn, w_ref[:, cols], preferred_element_type=F32)
        if c in ROPE_SPLITS:
            blocks = []
            for j in range(WIDTH // LANES):
                blk = acc[:, j * LANES:(j + 1) * LANES]
                blocks.append(blk * cos + pltpu.roll(blk, LANES - ROPE_DIM // 2, 1) * sa
                              + pltpu.roll(blk, ROPE_DIM // 2, 1) * sb)
            acc = jnp.concatenate(blocks, axis=1)
        o_ref[:, cols] = acc.astype(BF16)
        if c in (COL_QA, COL_KA, COL_VA):
            for j in range(WIDTH // LANES):
                buf_sc[j] = acc[:, j * LANES:(j + 1) * LANES]
            for dref in (d4_ref, d16_ref):
                d = dref.shape[1]
                for r in range(d):
                    for j in range(WIDTH // LANES):
                        lanes = slice(c * WIDTH + j * LANES, c * WIDTH + (j + 1) * LANES)
                        dref[0, r, :, lanes] = buf_sc[j, pl.ds(r, tm // d, stride=d), :].astype(BF16)
        if c == COL_VB:
            for kb in range(tm // qb):
                for hd in range(N_HEADS_DIFF):
                    blk = acc[kb * qb:(kb + 1) * qb, hd * LANES:(hd + 1) * LANES]
                    vt_ref[0, hd, kb] = blk.T.astype(BF16)


def _proj(h, gain, w_b, cos, sa, sb, batch, qb, tm=512):
    m, d = h.shape
    seq = m // batch
    nt = seq // tm
    row = lambda i: (i, 0)
    tab = lambda i: (i % nt, 0)
    res = lambda i: (i // nt, 0, i % nt, 0)
    return pl.pallas_call(
        functools.partial(_proj_kernel, qb=qb),
        grid=(m // tm,),
        in_specs=[pl.BlockSpec((tm, d), row),
                  pl.BlockSpec((1, d), lambda i: (0, 0)),
                  pl.BlockSpec((d, IN_COLS), lambda i: (0, 0)),
                  pl.BlockSpec((tm, LANES), tab),
                  pl.BlockSpec((tm, LANES), tab),
                  pl.BlockSpec((tm, LANES), tab)],
        out_specs=[pl.BlockSpec((tm, IN_COLS), row),
                   pl.BlockSpec((1, 4, tm // 4, 3 * WIDTH), res),
                   pl.BlockSpec((1, 16, tm // 16, 3 * WIDTH), res),
                   pl.BlockSpec((1, N_HEADS_DIFF, tm // qb, LANES, qb), lambda i: (i // nt, 0, i % nt, 0, 0))],
        out_shape=[jax.ShapeDtypeStruct((m, IN_COLS), BF16),
                   jax.ShapeDtypeStruct((batch, 4, seq // 4, 3 * WIDTH), BF16),
                   jax.ShapeDtypeStruct((batch, 16, seq // 16, 3 * WIDTH), BF16),
                   jax.ShapeDtypeStruct((batch, N_HEADS_DIFF, seq // qb, LANES, qb), BF16)],
        scratch_shapes=[pltpu.VMEM((WIDTH // LANES, tm, LANES), F32)],
        compiler_params=_params("arbitrary"),
        name="proj",
    )(h, gain.reshape(1, d), w_b, cos, sa, sb)


def _rope_tables(seq):
    half = ROPE_DIM // 2
    inv = jnp.power(ROPE_THETA, -jnp.arange(half, dtype=F32) * (2.0 / ROPE_DIM))
    ang = jnp.arange(seq).astype(F32)[:, None] * inv[None, :]
    cos, sin = jnp.cos(ang), jnp.sin(ang)
    ch = jnp.arange(LANES) % HEAD_DIM
    cos_l = jnp.take(cos, ch % half, axis=1)
    sin_l = jnp.take(sin, ch % half, axis=1)
    cos_t = jnp.where(ch < ROPE_DIM, cos_l, 1.0)
    sa = jnp.where(ch < half, -sin_l, 0.0)
    sb = jnp.where((ch >= half) & (ch < ROPE_DIM), sin_l, 0.0)
    return cos_t, sa, sb


def _dswa_kernel(q_ref, kc_ref, kp_ref, vc_ref, vp_ref, o_ref, l_ref, *, lc):
    w = DSWA_W
    first_chunk = pl.program_id(2) == 0
    head_a = lax.broadcasted_iota(jnp.int32, (1, LANES), 1) < HEAD_DIM
    ri = lax.broadcasted_iota(jnp.int32, (2 * w, 2 * w), 0) & (w - 1)
    cj = lax.broadcasted_iota(jnp.int32, (2 * w, 2 * w), 1)
    dist = w + ri - cj
    band = (dist >= 0) & (dist <= w)
    band0 = band & ((cj >= w) | jnp.logical_not(first_chunk))

    def unit(row0, hp, kk, vv, mask):
        lanes = slice(hp * LANES, (hp + 1) * LANES)
        qb = q_ref[0, 0, pl.ds(row0, w), lanes]
        zero = jnp.zeros_like(qb)
        qs = jnp.concatenate([jnp.where(head_a, qb, zero), jnp.where(head_a, zero, qb)], axis=0)
        s = lax.dot_general(qs, kk, (((1,), (1,)), ((), ())), preferred_element_type=F32)
        s = jnp.where(mask, s, NEG)
        m = jnp.max(s, axis=1, keepdims=True)
        p = jnp.exp2(s - m)
        den = jnp.sum(p, axis=1, keepdims=True)
        o = jnp.dot(p.astype(BF16), vv, preferred_element_type=F32) / den
        lse = m + jnp.log2(den)
        o_ref[0, 0, pl.ds(row0, w), lanes] = jnp.where(head_a, o[:w], o[w:]).astype(o_ref.dtype)
        l_ref[0, 0, pl.ds(row0, w), lanes] = jnp.where(head_a, lse[:w], lse[w:])

    for hp in range(WIDTH // LANES):
        lanes = slice(hp * LANES, (hp + 1) * LANES)
        kk0 = jnp.concatenate([kp_ref[0, 0, :, lanes], kc_ref[0, 0, 0:w, lanes]], axis=0)
        vv0 = jnp.concatenate([vp_ref[0, 0, :, lanes], vc_ref[0, 0, 0:w, lanes]], axis=0)
        unit(0, hp, kk0, vv0, band0)

    def body(n, carry):
        row0 = pl.multiple_of(n * w, w)
        prev0 = pl.multiple_of((n - 1) * w, w)
        for hp in range(WIDTH // LANES):
            lanes = slice(hp * LANES, (hp + 1) * LANES)
            unit(row0, hp, kc_ref[0, 0, pl.ds(prev0, 2 * w), lanes],
                 vc_ref[0, 0, pl.ds(prev0, 2 * w), lanes], band)
        return carry

    lax.fori_loop(1, lc // w, body, 0)


def _dswa_group(qkv, splits):
    b, d, sub, _ = qkv.shape
    lc = min(sub, 1024)
    per = lc // DSWA_W
    sq, sk, sv = splits

    def cur(split):
        return pl.BlockSpec((1, 1, lc, WIDTH), lambda bi, r, c: (bi, r, c, split))

    def prv(split):
        return pl.BlockSpec((1, 1, DSWA_W, WIDTH), lambda bi, r, c: (bi, r, jnp.maximum(c * per - 1, 0), split))

    out = pl.BlockSpec((1, 1, lc, WIDTH), lambda bi, r, c: (bi, r, c, 0))
    return pl.pallas_call(
        functools.partial(_dswa_kernel, lc=lc),
        grid=(b, d, sub // lc),
        in_specs=[cur(sq), cur(sk), prv(sk), cur(sv), prv(sv)],
        out_specs=[out, out],
        out_shape=[jax.ShapeDtypeStruct((b, d, sub, WIDTH), BF16),
                   jax.ShapeDtypeStruct((b, d, sub, WIDTH), F32)],
        compiler_params=_params("arbitrary", "arbitrary", "arbitrary"),
        name=f"dswa_d{d}",
    )(qkv, qkv, qkv, qkv, qkv)


def _diff_kernel(lam_ref, gain_ref, q_ref, k_ref, vt_ref, o_ref, m_sc, l_sc, acc_sc, sa_sc, sb_sc, *, qb,
                 lambda_init):
    qi = pl.program_id(2)
    head_a = lax.broadcasted_iota(jnp.int32, (1, LANES), 1) < HEAD_DIM
    q = q_ref[0]
    zero = jnp.zeros_like(q)
    qs = jnp.concatenate([jnp.where(head_a, q, zero), jnp.where(head_a, zero, q)], axis=0)

    m_sc[...] = jnp.full(m_sc.shape, NEG, F32)
    l_sc[...] = jnp.zeros(l_sc.shape, F32)
    acc_sc[...] = jnp.zeros(acc_sc.shape, F32)

    def scores(blk):
        kb = k_ref[0, pl.ds(pl.multiple_of(blk * qb, qb), qb), :]
        return lax.dot_general(kb, qs, (((1,), (1,)), ((), ())), preferred_element_type=F32)

    def step(blk, s_ref, mask):
        s = s_ref[...]
        if mask is not None:
            s = jnp.where(mask, s, NEG)
        m_old = m_sc[...]
        m_new = jnp.maximum(m_old, jnp.max(s, axis=0, keepdims=True))
        alpha = jnp.exp2(m_old - m_new)
        p = jnp.exp2(s - m_new)
        l_sc[...] = alpha * l_sc[...] + jnp.sum(p, axis=0, keepdims=True)
        acc_sc[...] = alpha * acc_sc[...] + jnp.dot(vt_ref[0, 0, blk], p.astype(BF16),
                                                    preferred_element_type=F32)
        m_sc[...] = m_new

    last = jnp.maximum(qi - 1, 0)
    key = lax.broadcasted_iota(jnp.int32, (qb, 2 * qb), 0)
    qry = lax.broadcasted_iota(jnp.int32, (qb, 2 * qb), 1) & (qb - 1)
    sa_sc[...] = scores(qi)
    sb_sc[...] = scores(0)
    step(qi, sa_sc, key <= qry)

    def pair(j, carry):
        b0 = 2 * j
        sa_sc[...] = scores(b0 + 1)
        step(b0, sb_sc, None)
        sb_sc[...] = scores(jnp.minimum(b0 + 2, last))
        step(b0 + 1, sa_sc, None)
        return carry

    lax.fori_loop(0, qi // 2, pair, 0)

    @pl.when(qi % 2 == 1)
    def _():
        step(qi - 1, sb_sc, None)

    o_all = acc_sc[...] / l_sc[...]
    lam_p = lam_ref[...]
    lam = (jnp.exp(jnp.sum(lam_p[0:1] * lam_p[1:2], axis=1, keepdims=True))
           - jnp.exp(jnp.sum(lam_p[2:3] * lam_p[3:4], axis=1, keepdims=True)) + lambda_init)
    o = o_all[:, :qb] - lam * o_all[:, qb:]
    o = o * lax.rsqrt(jnp.mean(o * o, axis=0, keepdims=True) + SUBLN_EPS) * gain_ref[...]
    o_ref[0] = (o * (1.0 - lambda_init)).T.astype(o_ref.dtype)


def _diff(u3, vt, lam_p, subln_gain, lambda_init, qb):
    b, s, _ = u3.shape
    per = WIDTH // LANES
    nb = s // qb
    return pl.pallas_call(
        functools.partial(_diff_kernel, qb=qb, lambda_init=lambda_init),
        grid=(b, N_HEADS_DIFF, nb),
        in_specs=[pl.BlockSpec(lam_p.shape, lambda bi, h, qi: (0, 0)),
                  pl.BlockSpec((LANES, 1), lambda bi, h, qi: (0, 0)),
                  pl.BlockSpec((1, qb, LANES), lambda bi, h, qi: (bi, qi, COL_QB * per + h)),
                  pl.BlockSpec((1, s, LANES), lambda bi, h, qi: (bi, 0, COL_KB * per + h)),
                  pl.BlockSpec((1, 1, nb, LANES, qb), lambda bi, h, qi: (bi, h, 0, 0, 0))],
        out_specs=pl.BlockSpec((1, qb, LANES), lambda bi, h, qi: (bi, qi, h)),
        out_shape=jax.ShapeDtypeStruct((b, s, WIDTH), BF16),
        scratch_shapes=[pltpu.VMEM((1, 2 * qb), F32), pltpu.VMEM((1, 2 * qb), F32),
                        pltpu.VMEM((LANES, 2 * qb), F32),
                        pltpu.VMEM((qb, 2 * qb), F32), pltpu.VMEM((qb, 2 * qb), F32)],
        compiler_params=_params("arbitrary", "arbitrary", "arbitrary"),
        name="diff",
    )(lam_p, subln_gain.reshape(LANES, 1), u3, u3, vt)


def _sigmoid(x):
    return 1.0 / (1.0 + jnp.exp(-x))


def _post_kernel(h_ref, ga_ref, gb_ref, o1_ref, l1_ref, o4_ref, l4_ref, o16_ref, l16_ref, ob_ref, p_ref,
                 wo_ref, ng_ref, wg_ref, wp_ref, fg_ref, out_ref, on_sc, ln_sc, *, final):
    tm = h_ref.shape[0]

    def natural(o_ref, l_ref, lanes):
        d = o_ref.shape[1]
        for r in range(d):
            on_sc[pl.ds(r, tm // d, stride=d), :] = o_ref[0, r, :, lanes].astype(F32)
            ln_sc[pl.ds(r, tm // d, stride=d), :] = l_ref[0, r, :, lanes]
        return on_sc[...], ln_sc[...]

    merged = []
    for j in range(WIDTH // LANES):
        lanes = slice(j * LANES, (j + 1) * LANES)
        o1, l1 = o1_ref[:, lanes].astype(F32), l1_ref[:, lanes]
        o4, l4 = natural(o4_ref, l4_ref, lanes)
        top = jnp.maximum(l1, l4)
        w1, w4 = jnp.exp2(l1 - top), jnp.exp2(l4 - top)
        num, den = w1 * o1 + w4 * o4, w1 + w4
        o16, l16 = natural(o16_ref, l16_ref, lanes)
        top2 = jnp.maximum(top, l16)
        sc, w16 = jnp.exp2(top - top2), jnp.exp2(l16 - top2)
        merged.append((sc * num + w16 * o16) / (sc * den + w16))
    oa = jnp.concatenate(merged, axis=1)

    ga = ga_ref[...].astype(F32)
    gb = gb_ref[...].astype(F32)
    ya = (oa * (ga * _sigmoid(ga))).astype(BF16)
    yb = (ob_ref[...].astype(F32) * (gb * _sigmoid(gb))).astype(BF16)
    y = (jnp.dot(ya, wo_ref[0:WIDTH, :], preferred_element_type=F32)
         + jnp.dot(yb, wo_ref[WIDTH:2 * WIDTH, :], preferred_element_type=F32))
    h1 = h_ref[...] + y
    n = h1 * lax.rsqrt(jnp.mean(h1 * h1, axis=-1, keepdims=True) + RMS_EPS) * ng_ref[...]
    gate = _sigmoid(jnp.dot(n.astype(BF16), wg_ref[...], preferred_element_type=F32))
    ple = jnp.dot(p_ref[...].astype(BF16), wp_ref[...], preferred_element_type=F32)
    h2 = h1 + ple * gate
    if final:
        h2 = h2 * lax.rsqrt(jnp.mean(h2 * h2, axis=-1, keepdims=True) + RMS_EPS) * fg_ref[...]
    out_ref[...] = h2


def _post(h, u, g1, g4, g16, ob, p_i, wo_b, ple_gain, wg_b, wp_b, final_gain, final, batch, tm=512):
    m, d = h.shape
    nt = m // batch // tm
    row = lambda i: (i, 0)
    const = lambda i: (0, 0)
    res = lambda i: (i // nt, 0, i % nt, 0)
    grp = lambda dil: pl.BlockSpec((1, dil, tm // dil, WIDTH), res)
    return pl.pallas_call(
        functools.partial(_post_kernel, final=final),
        grid=(m // tm,),
        in_specs=[pl.BlockSpec((tm, d), row),
                  pl.BlockSpec((tm, WIDTH), lambda i: (i, COL_GA)),
                  pl.BlockSpec((tm, WIDTH), lambda i: (i, COL_GB)),
                  pl.BlockSpec((tm, WIDTH), row), pl.BlockSpec((tm, WIDTH), row),
                  grp(4), grp(4), grp(16), grp(16),
                  pl.BlockSpec((tm, WIDTH), row),
                  pl.BlockSpec((tm, p_i.shape[1]), row),
                  pl.BlockSpec(wo_b.shape, const),
                  pl.BlockSpec((1, d), const),
                  pl.BlockSpec(wg_b.shape, const),
                  pl.BlockSpec(wp_b.shape, const),
                  pl.BlockSpec((1, d), const)],
        out_specs=pl.BlockSpec((tm, d), row),
        out_shape=jax.ShapeDtypeStruct((m, d), F32),
        scratch_shapes=[pltpu.VMEM((tm, LANES), F32), pltpu.VMEM((tm, LANES), F32)],
        compiler_params=_params("arbitrary"),
        name="post",
    )(h, u, u, g1[0].reshape(m, WIDTH), g1[1].reshape(m, WIDTH), g4[0], g4[1], g16[0], g16[1], ob, p_i,
      wo_b, ple_gain.reshape(1, d), wg_b, wp_b, final_gain.reshape(1, d))


def kernel(x, p, attn_norm_gain, w_in, w_out, lambda_q1, lambda_k1, lambda_q2, lambda_k2, subln_gain,
           ple_norm_gain, w_ple_gate, w_ple, final_norm_gain):
    b, s, d = x.shape
    depth = w_in.shape[0]
    h = x.reshape(b * s, d)
    cos, sa, sb = _rope_tables(s)
    col = jnp.arange(IN_COLS) // WIDTH
    q_scale = jnp.where((col == COL_QA) | (col == COL_QB), HEAD_DIM ** -0.5 * LOG2_E, 1.0).astype(F32)
    a_splits = (COL_QA, COL_KA, COL_VA)
    for i in range(depth):
        lambda_init = 0.8 - 0.6 * math.exp(-0.3 * i)
        w_in_b = (w_in[i] * q_scale[None, :]).astype(BF16)
        u, qkv4, qkv16, vt = _proj(h, attn_norm_gain[i], w_in_b, cos, sa, sb, b, DIFF_QB)
        u3 = u.reshape(b, s, IN_COLS)
        g1 = _dswa_group(u3.reshape(b, 1, s, IN_COLS), a_splits)
        g4 = _dswa_group(qkv4, a_splits)
        g16 = _dswa_group(qkv16, a_splits)
        lam_p = jnp.stack([lambda_q1[i], lambda_k1[i], lambda_q2[i], lambda_k2[i]]).astype(F32)
        ob = _diff(u3, vt, lam_p, subln_gain[i], lambda_init, DIFF_QB)
        h = _post(h, u, g1, g4, g16, ob.reshape(b * s, WIDTH), p[i].reshape(b * s, -1),
                  w_out[i].astype(BF16), ple_norm_gain[i], w_ple_gate[i].astype(BF16),
                  w_ple[i].astype(BF16), final_norm_gain, final=i == depth - 1, batch=b)
    return h.reshape(b, s, d)
```

```python
import functools
import math

import jax
import jax.numpy as jnp
from jax import lax
from jax.experimental import pallas as pl
from jax.experimental.pallas import tpu as pltpu

HEAD_DIM = 64
LANES = 128
DSWA_W = 128
N_HEADS_DIFF = 4
DIFF_QB = 512
WIDTH = 512
IN_COLS = 8 * WIDTH
ROPE_THETA = 500000.0
ROPE_DIM = HEAD_DIM // 4
RMS_EPS = 1e-6
SUBLN_EPS = 1e-5
NEG = -1e30
LOG2_E = math.log2(math.e)
VMEM_LIMIT = 48 * 1024 * 1024

COL_QA, COL_KA, COL_VA, COL_GA, COL_QB, COL_KB, COL_VB, COL_GB = range(8)
ROPE_SPLITS = (COL_QA, COL_KA, COL_QB, COL_KB)

BF16 = jnp.bfloat16
F32 = jnp.float32


def _params(*sem):
    return pltpu.CompilerParams(dimension_semantics=sem, vmem_limit_bytes=VMEM_LIMIT)


def _proj_kernel(x_ref, g_ref, w_ref, cos_ref, sa_ref, sb_ref, o_ref, d4_ref, d16_ref, vt_ref, buf_sc, *, qb):
    tm = x_ref.shape[0]
    x = x_ref[...]
    ms = jnp.mean(x * x, axis=-1, keepdims=True)
    xn = (x * lax.rsqrt(ms + RMS_EPS) * g_ref[...]).astype(BF16)
    cos, sa, sb = cos_ref[...], sa_ref[...], sb_ref[...]
    for c in range(IN_COLS // WIDTH):
        cols = slice(c * WIDTH, (c + 1) * WIDTH)
        acc = jnp.dot(xn, w_ref[:, cols], preferred_element_type=F32)
        if c in ROPE_SPLITS:
            blocks = []
            for j in range(WIDTH // LANES):
                blk = acc[:, j * LANES:(j + 1) * LANES]
                blocks.append(blk * cos + pltpu.roll(blk, LANES - ROPE_DIM // 2, 1) * sa
                              + pltpu.roll(blk, ROPE_DIM // 2, 1) * sb)
            acc = jnp.concatenate(blocks, axis=1)
        o_ref[:, cols] = acc.astype(BF16)
        if c in (COL_QA, COL_KA, COL_VA):
            for j in range(WIDTH // LANES):
                buf_sc[j] = acc[:, j * LANES:(j + 1) * LANES]
            for dref in (d4_ref, d16_ref):
                d = dref.shape[1]
                for r in range(d):
                    for j in range(WIDTH // LANES):
                        lanes = slice(c * WIDTH + j * LANES, c * WIDTH + (j + 1) * LANES)
                        dref[0, r, :, lanes] = buf_sc[j, pl.ds(r, tm // d, stride=d), :].astype(BF16)
        if c == COL_VB:
            for kb in range(tm // qb):
                for hd in range(N_HEADS_DIFF):
                    blk = acc[kb * qb:(kb + 1) * qb, hd * LANES:(hd + 1) * LANES]
                    vt_ref[0, hd, kb] = blk.T.astype(BF16)


def _proj(h, gain, w_b, cos, sa, sb, batch, qb, tm=512):
    m, d = h.shape
    seq = m // batch
    nt = seq // tm
    row = lambda i: (i, 0)
    tab = lambda i: (i % nt, 0)
    res = lambda i: (i // nt, 0, i % nt, 0)
    return pl.pallas_call(
        functools.partial(_proj_kernel, qb=qb),
        grid=(m // tm,),
        in_specs=[pl.BlockSpec((tm, d), row),
                  pl.BlockSpec((1, d), lambda i: (0, 0)),
                  pl.BlockSpec((d, IN_COLS), lambda i: (0, 0)),
                  pl.BlockSpec((tm, LANES), tab),
                  pl.BlockSpec((tm, LANES), tab),
                  pl.BlockSpec((tm, LANES), tab)],
        out_specs=[pl.BlockSpec((tm, IN_COLS), row),
                   pl.BlockSpec((1, 4, tm // 4, 3 * WIDTH), res),
                   pl.BlockSpec((1, 16, tm // 16, 3 * WIDTH), res),
                   pl.BlockSpec((1, N_HEADS_DIFF, tm // qb, LANES, qb), lambda i: (i // nt, 0, i % nt, 0, 0))],
        out_shape=[jax.ShapeDtypeStruct((m, IN_COLS), BF16),
                   jax.ShapeDtypeStruct((batch, 4, seq // 4, 3 * WIDTH), BF16),
                   jax.ShapeDtypeStruct((batch, 16, seq // 16, 3 * WIDTH), BF16),
                   jax.ShapeDtypeStruct((batch, N_HEADS_DIFF, seq // qb, LANES, qb), BF16)],
        scratch_shapes=[pltpu.VMEM((WIDTH // LANES, tm, LANES), F32)],
        compiler_params=_params("arbitrary"),
        name="proj",
    )(h, gain.reshape(1, d), w_b, cos, sa, sb)


def _rope_tables(seq):
    half = ROPE_DIM // 2
    inv = jnp.power(ROPE_THETA, -jnp.arange(half, dtype=F32) * (2.0 / ROPE_DIM))
    ang = jnp.arange(seq).astype(F32)[:, None] * inv[None, :]
    cos, sin = jnp.cos(ang), jnp.sin(ang)
    ch = jnp.arange(LANES) % HEAD_DIM
    cos_l = jnp.take(cos, ch % half, axis=1)
    sin_l = jnp.take(sin, ch % half, axis=1)
    cos_t = jnp.where(ch < ROPE_DIM, cos_l, 1.0)
    sa = jnp.where(ch < half, -sin_l, 0.0)
    sb = jnp.where((ch >= half) & (ch < ROPE_DIM), sin_l, 0.0)
    return cos_t, sa, sb


def _dswa_kernel(q_ref, kc_ref, kp_ref, vc_ref, vp_ref, o_ref, l_ref, vt_sc, bias_sc, k_sc, sa_sc, sb_sc, *,
                 lc):
    w = DSWA_W
    nblk = lc // w
    first_chunk = pl.program_id(2) == 0
    head_a = lax.broadcasted_iota(jnp.int32, (1, LANES), 1) < HEAD_DIM
    kj = lax.broadcasted_iota(jnp.int32, (2 * w, 2 * w), 0)
    qi = lax.broadcasted_iota(jnp.int32, (2 * w, 2 * w), 1) & (w - 1)
    dist = w + qi - kj
    band = (dist >= 0) & (dist <= w)
    bias_sc[0] = jnp.where(band, 0.0, NEG)
    bias_sc[1] = jnp.where(band & ((kj >= w) | jnp.logical_not(first_chunk)), 0.0, NEG)

    def transposed(v):
        return v.astype(F32).T.astype(BF16)

    for hp in range(WIDTH // LANES):
        lanes = slice(hp * LANES, (hp + 1) * LANES)
        vt_sc[0, lanes, :] = transposed(vp_ref[0, 0, :, lanes])
        for n in range(nblk):
            vt_sc[n + 1, lanes, :] = transposed(vc_ref[0, 0, n * w:(n + 1) * w, lanes])

    k_sc[0:w, :] = kp_ref[0, 0]
    k_sc[w:w + lc, :] = kc_ref[0, 0]

    def scores(n, hp):
        lanes = slice(hp * LANES, (hp + 1) * LANES)
        row0 = pl.multiple_of(n * w, w)
        qb = q_ref[0, 0, pl.ds(row0, w), lanes]
        zero = jnp.zeros_like(qb)
        qs = jnp.concatenate([jnp.where(head_a, qb, zero), jnp.where(head_a, zero, qb)], axis=0)
        kk = k_sc[pl.ds(row0, 2 * w), lanes]
        bias = bias_sc[jnp.where(n == 0, 1, 0)]
        return lax.dot_general(kk, qs, (((1,), (1,)), ((), ())), preferred_element_type=F32) + bias

    def finish(n, hp, s):
        lanes = slice(hp * LANES, (hp + 1) * LANES)
        row0 = pl.multiple_of(n * w, w)
        m = jnp.max(s, axis=0, keepdims=True)
        p = jnp.exp2(s - m)
        den = jnp.sum(p, axis=0, keepdims=True)
        vv = jnp.concatenate([vt_sc[n, lanes, :], vt_sc[n + 1, lanes, :]], axis=1)
        o = jnp.dot(vv, p.astype(BF16), preferred_element_type=F32) * (1.0 / den)
        lse = m + jnp.log2(den)
        o_sel = jnp.concatenate([o[:HEAD_DIM, :w], o[HEAD_DIM:, w:]], axis=0)
        l_sel = jnp.concatenate([jnp.broadcast_to(lse[:, :w], (HEAD_DIM, w)),
                                 jnp.broadcast_to(lse[:, w:], (HEAD_DIM, w))], axis=0)
        o_ref[0, 0, pl.ds(row0, w), lanes] = o_sel.T.astype(o_ref.dtype)
        l_ref[0, 0, pl.ds(row0, w), lanes] = l_sel.T

    nhp = WIDTH // LANES

    def issue(n, s_ref):
        for hp in range(nhp):
            s_ref[hp] = scores(n, hp)

    def consume(n, s_ref):
        for hp in range(nhp):
            finish(n, hp, s_ref[hp])

    issue(0, sa_sc)

    def pair(j, carry):
        n0 = 2 * j
        issue(n0 + 1, sb_sc)
        consume(n0, sa_sc)
        issue(jnp.minimum(n0 + 2, nblk - 1), sa_sc)
        consume(n0 + 1, sb_sc)
        return carry

    lax.fori_loop(0, nblk // 2, pair, 0)


def _dswa_group(qkv, splits):
    b, d, sub, _ = qkv.shape
    lc = min(sub, 1024)
    per = lc // DSWA_W
    assert per % 2 == 0, "the kernel walks query blocks in pairs"
    sq, sk, sv = splits

    def cur(split):
        return pl.BlockSpec((1, 1, lc, WIDTH), lambda bi, r, c: (bi, r, c, split))

    def prv(split):
        return pl.BlockSpec((1, 1, DSWA_W, WIDTH), lambda bi, r, c: (bi, r, jnp.maximum(c * per - 1, 0), split))

    out = pl.BlockSpec((1, 1, lc, WIDTH), lambda bi, r, c: (bi, r, c, 0))
    return pl.pallas_call(
        functools.partial(_dswa_kernel, lc=lc),
        grid=(b, d, sub // lc),
        in_specs=[cur(sq), cur(sk), prv(sk), cur(sv), prv(sv)],
        out_specs=[out, out],
        out_shape=[jax.ShapeDtypeStruct((b, d, sub, WIDTH), BF16),
                   jax.ShapeDtypeStruct((b, d, sub, WIDTH), F32)],
        scratch_shapes=[pltpu.VMEM((per + 1, WIDTH, DSWA_W), BF16),
                        pltpu.VMEM((2, 2 * DSWA_W, 2 * DSWA_W), F32),
                        pltpu.VMEM((lc + DSWA_W, WIDTH), BF16),
                        pltpu.VMEM((WIDTH // LANES, 2 * DSWA_W, 2 * DSWA_W), F32),
                        pltpu.VMEM((WIDTH // LANES, 2 * DSWA_W, 2 * DSWA_W), F32)],
        compiler_params=_params("arbitrary", "arbitrary", "arbitrary"),
        name=f"dswa_d{d}",
    )(qkv, qkv, qkv, qkv, qkv)


def _diff_kernel(lam_ref, gain_ref, q_ref, k_ref, vt_ref, o_ref, m_sc, l_sc, acc_sc, sa_sc, sb_sc, *, qb,
                 lambda_init):
    qi = pl.program_id(2)
    head_a = lax.broadcasted_iota(jnp.int32, (1, LANES), 1) < HEAD_DIM
    q = q_ref[0]
    zero = jnp.zeros_like(q)
    qs = jnp.concatenate([jnp.where(head_a, q, zero), jnp.where(head_a, zero, q)], axis=0)

    m_sc[...] = jnp.full(m_sc.shape, NEG, F32)
    l_sc[...] = jnp.zeros(l_sc.shape, F32)
    acc_sc[...] = jnp.zeros(acc_sc.shape, F32)

    def scores(blk):
        kb = k_ref[0, pl.ds(pl.multiple_of(blk * qb, qb), qb), :]
        return lax.dot_general(kb, qs, (((1,), (1,)), ((), ())), preferred_element_type=F32)

    def step(blk, s_ref, mask):
        s = s_ref[...]
        if mask is not None:
            s = jnp.where(mask, s, NEG)
        m_old = m_sc[...]
        m_new = jnp.maximum(m_old, jnp.max(s, axis=0, keepdims=True))
        alpha = jnp.exp2(m_old - m_new)
        p = jnp.exp2(s - m_new)
        l_sc[...] = alpha * l_sc[...] + jnp.sum(p, axis=0, keepdims=True)
        acc_sc[...] = alpha * acc_sc[...] + jnp.dot(vt_ref[0, 0, blk], p.astype(BF16),
                                                    preferred_element_type=F32)
        m_sc[...] = m_new

    last = jnp.maximum(qi - 1, 0)
    key = lax.broadcasted_iota(jnp.int32, (qb, 2 * qb), 0)
    qry = lax.broadcasted_iota(jnp.int32, (qb, 2 * qb), 1) & (qb - 1)
    sa_sc[...] = scores(qi)
    sb_sc[...] = scores(0)
    step(qi, sa_sc, key <= qry)

    def pair(j, carry):
        b0 = 2 * j
        sa_sc[...] = scores(b0 + 1)
        step(b0, sb_sc, None)
        sb_sc[...] = scores(jnp.minimum(b0 + 2, last))
        step(b0 + 1, sa_sc, None)
        return carry

    lax.fori_loop(0, qi // 2, pair, 0)

    @pl.when(qi % 2 == 1)
    def _():
        step(qi - 1, sb_sc, None)

    o_all = acc_sc[...] / l_sc[...]
    lam_p = lam_ref[...]
    lam = (jnp.exp(jnp.sum(lam_p[0:1] * lam_p[1:2], axis=1, keepdims=True))
           - jnp.exp(jnp.sum(lam_p[2:3] * lam_p[3:4], axis=1, keepdims=True)) + lambda_init)
    o = o_all[:, :qb] - lam * o_all[:, qb:]
    o = o * lax.rsqrt(jnp.mean(o * o, axis=0, keepdims=True) + SUBLN_EPS) * gain_ref[...]
    o_ref[0] = (o * (1.0 - lambda_init)).T.astype(o_ref.dtype)


def _diff(u3, vt, lam_p, subln_gain, lambda_init, qb):
    b, s, _ = u3.shape
    per = WIDTH // LANES
    nb = s // qb
    return pl.pallas_call(
        functools.partial(_diff_kernel, qb=qb, lambda_init=lambda_init),
        grid=(b, N_HEADS_DIFF, nb),
        in_specs=[pl.BlockSpec(lam_p.shape, lambda bi, h, qi: (0, 0)),
                  pl.BlockSpec((LANES, 1), lambda bi, h, qi: (0, 0)),
                  pl.BlockSpec((1, qb, LANES), lambda bi, h, qi: (bi, qi, COL_QB * per + h)),
                  pl.BlockSpec((1, s, LANES), lambda bi, h, qi: (bi, 0, COL_KB * per + h)),
                  pl.BlockSpec((1, 1, nb, LANES, qb), lambda bi, h, qi: (bi, h, 0, 0, 0))],
        out_specs=pl.BlockSpec((1, qb, LANES), lambda bi, h, qi: (bi, qi, h)),
        out_shape=jax.ShapeDtypeStruct((b, s, WIDTH), BF16),
        scratch_shapes=[pltpu.VMEM((1, 2 * qb), F32), pltpu.VMEM((1, 2 * qb), F32),
                        pltpu.VMEM((LANES, 2 * qb), F32),
                        pltpu.VMEM((qb, 2 * qb), F32), pltpu.VMEM((qb, 2 * qb), F32)],
        compiler_params=_params("arbitrary", "arbitrary", "arbitrary"),
        name="diff",
    )(lam_p, subln_gain.reshape(LANES, 1), u3, u3, vt)


def _sigmoid(x):
    return 1.0 / (1.0 + jnp.exp(-x))


def _post_kernel(h_ref, ga_ref, gb_ref, o1_ref, l1_ref, o4_ref, l4_ref, o16_ref, l16_ref, ob_ref, p_ref,
                 wo_ref, ng_ref, wg_ref, wp_ref, fg_ref, out_ref, on_sc, ln_sc, *, final):
    tm = h_ref.shape[0]

    def natural(o_ref, l_ref, lanes):
        d = o_ref.shape[1]
        for r in range(d):
            on_sc[pl.ds(r, tm // d, stride=d), :] = o_ref[0, r, :, lanes].astype(F32)
            ln_sc[pl.ds(r, tm // d, stride=d), :] = l_ref[0, r, :, lanes]
        return on_sc[...], ln_sc[...]

    merged = []
    for j in range(WIDTH // LANES):
        lanes = slice(j * LANES, (j + 1) * LANES)
        o1, l1 = o1_ref[:, lanes].astype(F32), l1_ref[:, lanes]
        o4, l4 = natural(o4_ref, l4_ref, lanes)
        top = jnp.maximum(l1, l4)
        w1, w4 = jnp.exp2(l1 - top), jnp.exp2(l4 - top)
        num, den = w1 * o1 + w4 * o4, w1 + w4
        o16, l16 = natural(o16_ref, l16_ref, lanes)
        top2 = jnp.maximum(top, l16)
        sc, w16 = jnp.exp2(top - top2), jnp.exp2(l16 - top2)
        merged.append((sc * num + w16 * o16) / (sc * den + w16))
    oa = jnp.concatenate(merged, axis=1)

    ga = ga_ref[...].astype(F32)
    gb = gb_ref[...].astype(F32)
    ya = (oa * (ga * _sigmoid(ga))).astype(BF16)
    yb = (ob_ref[...].astype(F32) * (gb * _sigmoid(gb))).astype(BF16)
    y = (jnp.dot(ya, wo_ref[0:WIDTH, :], preferred_element_type=F32)
         + jnp.dot(yb, wo_ref[WIDTH:2 * WIDTH, :], preferred_element_type=F32))
    h1 = h_ref[...] + y
    n = h1 * lax.rsqrt(jnp.mean(h1 * h1, axis=-1, keepdims=True) + RMS_EPS) * ng_ref[...]
    gate = _sigmoid(jnp.dot(n.astype(BF16), wg_ref[...], preferred_element_type=F32))
    ple = jnp.dot(p_ref[...].astype(BF16), wp_ref[...], preferred_element_type=F32)
    h2 = h1 + ple * gate
    if final:
        h2 = h2 * lax.rsqrt(jnp.mean(h2 * h2, axis=-1, keepdims=True) + RMS_EPS) * fg_ref[...]
    out_ref[...] = h2


def _post(h, u, g1, g4, g16, ob, p_i, wo_b, ple_gain, wg_b, wp_b, final_gain, final, batch, tm=512):
    m, d = h.shape
    nt = m // batch // tm
    row = lambda i: (i, 0)
    const = lambda i: (0, 0)
    res = lambda i: (i // nt, 0, i % nt, 0)
    grp = lambda dil: pl.BlockSpec((1, dil, tm // dil, WIDTH), res)
    return pl.pallas_call(
        functools.partial(_post_kernel, final=final),
        grid=(m // tm,),
        in_specs=[pl.BlockSpec((tm, d), row),
                  pl.BlockSpec((tm, WIDTH), lambda i: (i, COL_GA)),
                  pl.BlockSpec((tm, WIDTH), lambda i: (i, COL_GB)),
                  pl.BlockSpec((tm, WIDTH), row), pl.BlockSpec((tm, WIDTH), row),
                  grp(4), grp(4), grp(16), grp(16),
                  pl.BlockSpec((tm, WIDTH), row),
                  pl.BlockSpec((tm, p_i.shape[1]), row),
                  pl.BlockSpec(wo_b.shape, const),
                  pl.BlockSpec((1, d), const),
                  pl.BlockSpec(wg_b.shape, const),
                  pl.BlockSpec(wp_b.shape, const),
                  pl.BlockSpec((1, d), const)],
        out_specs=pl.BlockSpec((tm, d), row),
        out_shape=jax.ShapeDtypeStruct((m, d), F32),
        scratch_shapes=[pltpu.VMEM((tm, LANES), F32), pltpu.VMEM((tm, LANES), F32)],
        compiler_params=_params("arbitrary"),
        name="post",
    )(h, u, u, g1[0].reshape(m, WIDTH), g1[1].reshape(m, WIDTH), g4[0], g4[1], g16[0], g16[1], ob, p_i,
      wo_b, ple_gain.reshape(1, d), wg_b, wp_b, final_gain.reshape(1, d))


def kernel(x, p, attn_norm_gain, w_in, w_out, lambda_q1, lambda_k1, lambda_q2, lambda_k2, subln_gain,
           ple_norm_gain, w_ple_gate, w_ple, final_norm_gain):
    b, s, d = x.shape
    depth = w_in.shape[0]
    h = x.reshape(b * s, d)
    cos, sa, sb = _rope_tables(s)
    col = jnp.arange(IN_COLS) // WIDTH
    q_scale = jnp.where((col == COL_QA) | (col == COL_QB), HEAD_DIM ** -0.5 * LOG2_E, 1.0).astype(F32)
    a_splits = (COL_QA, COL_KA, COL_VA)
    for i in range(depth):
        lambda_init = 0.8 - 0.6 * math.exp(-0.3 * i)
        w_in_b = (w_in[i] * q_scale[None, :]).astype(BF16)
        u, qkv4, qkv16, vt = _proj(h, attn_norm_gain[i], w_in_b, cos, sa, sb, b, DIFF_QB)
        u3 = u.reshape(b, s, IN_COLS)
        g1 = _dswa_group(u3.reshape(b, 1, s, IN_COLS), a_splits)
        g4 = _dswa_group(qkv4, a_splits)
        g16 = _dswa_group(qkv16, a_splits)
        lam_p = jnp.stack([lambda_q1[i], lambda_k1[i], lambda_q2[i], lambda_k2[i]]).astype(F32)
        ob = _diff(u3, vt, lam_p, subln_gain[i], lambda_init, DIFF_QB)
        h = _post(h, u, g1, g4, g16, ob.reshape(b * s, WIDTH), p[i].reshape(b * s, -1),
                  w_out[i].astype(BF16), ple_norm_gain[i], w_ple_gate[i].astype(BF16),
                  w_ple[i].astype(BF16), final_norm_gain, final=i == depth - 1, batch=b)
    return h.reshape(b, s, d)
```

```python
import functools
import math

import jax
import jax.numpy as jnp
from jax import lax
from jax.experimental import pallas as pl
from jax.experimental.pallas import tpu as pltpu

HEAD_DIM = 64
LANES = 128
DSWA_W = 128
N_HEADS_DIFF = 4
DIFF_QB = 512
WIDTH = 512
IN_COLS = 8 * WIDTH
ROPE_THETA = 500000.0
ROPE_DIM = HEAD_DIM // 4
RMS_EPS = 1e-6
SUBLN_EPS = 1e-5
NEG = -1e30
LOG2_E = math.log2(math.e)
VMEM_LIMIT = 48 * 1024 * 1024

COL_QA, COL_KA, COL_VA, COL_GA, COL_QB, COL_KB, COL_VB, COL_GB = range(8)
ROPE_SPLITS = (COL_QA, COL_KA, COL_QB, COL_KB)

BF16 = jnp.bfloat16
F32 = jnp.float32


def _params(*sem):
    return pltpu.CompilerParams(dimension_semantics=sem, vmem_limit_bytes=VMEM_LIMIT)


def _proj_kernel(x_ref, g_ref, w_ref, cos_ref, sa_ref, sb_ref, o_ref, d4_ref, d16_ref, vt_ref, buf_sc, buf4_sc,
                 *, qb):
    tm = x_ref.shape[0]
    x = x_ref[...]
    ms = jnp.mean(x * x, axis=-1, keepdims=True)
    xn = (x * lax.rsqrt(ms + RMS_EPS) * g_ref[...]).astype(BF16)
    cos, sa, sb = cos_ref[...], sa_ref[...], sb_ref[...]
    for c in range(IN_COLS // WIDTH):
        cols = slice(c * WIDTH, (c + 1) * WIDTH)
        acc = jnp.dot(xn, w_ref[:, cols], preferred_element_type=F32)
        if c in ROPE_SPLITS:
            blocks = []
            for j in range(WIDTH // LANES):
                blk = acc[:, j * LANES:(j + 1) * LANES]
                blocks.append(blk * cos + pltpu.roll(blk, LANES - ROPE_DIM // 2, 1) * sa
                              + pltpu.roll(blk, ROPE_DIM // 2, 1) * sb)
            acc = jnp.concatenate(blocks, axis=1)
        o_ref[:, cols] = acc.astype(BF16)
        if c in (COL_QA, COL_KA, COL_VA):
            n4, n16 = tm // 4, tm // 16
            for j in range(WIDTH // LANES):
                lanes = slice(c * WIDTH + j * LANES, c * WIDTH + (j + 1) * LANES)
                buf_sc[j] = acc[:, j * LANES:(j + 1) * LANES]
                for r4 in range(4):
                    rows4 = buf_sc[j, pl.ds(r4, n4, stride=4), :]
                    d4_ref[0, r4, :, lanes] = rows4.astype(BF16)
                    buf4_sc[j, r4 * n4:(r4 + 1) * n4, :] = rows4
                for r4 in range(4):
                    for k in range(4):
                        rows16 = buf4_sc[j, pl.ds(r4 * n4 + k, n16, stride=4), :]
                        d16_ref[0, 4 * k + r4, :, lanes] = rows16.astype(BF16)
        if c == COL_VB:
            for kb in range(tm // qb):
                for hd in range(N_HEADS_DIFF):
                    blk = acc[kb * qb:(kb + 1) * qb, hd * LANES:(hd + 1) * LANES]
                    vt_ref[0, hd, kb] = blk.T.astype(BF16)


def _proj(h, gain, w_b, cos, sa, sb, batch, qb, tm=512):
    m, d = h.shape
    seq = m // batch
    nt = seq // tm
    row = lambda i: (i, 0)
    tab = lambda i: (i % nt, 0)
    res = lambda i: (i // nt, 0, i % nt, 0)
    return pl.pallas_call(
        functools.partial(_proj_kernel, qb=qb),
        grid=(m // tm,),
        in_specs=[pl.BlockSpec((tm, d), row),
                  pl.BlockSpec((1, d), lambda i: (0, 0)),
                  pl.BlockSpec((d, IN_COLS), lambda i: (0, 0)),
                  pl.BlockSpec((tm, LANES), tab),
                  pl.BlockSpec((tm, LANES), tab),
                  pl.BlockSpec((tm, LANES), tab)],
        out_specs=[pl.BlockSpec((tm, IN_COLS), row),
                   pl.BlockSpec((1, 4, tm // 4, 3 * WIDTH), res),
                   pl.BlockSpec((1, 16, tm // 16, 3 * WIDTH), res),
                   pl.BlockSpec((1, N_HEADS_DIFF, tm // qb, LANES, qb), lambda i: (i // nt, 0, i % nt, 0, 0))],
        out_shape=[jax.ShapeDtypeStruct((m, IN_COLS), BF16),
                   jax.ShapeDtypeStruct((batch, 4, seq // 4, 3 * WIDTH), BF16),
                   jax.ShapeDtypeStruct((batch, 16, seq // 16, 3 * WIDTH), BF16),
                   jax.ShapeDtypeStruct((batch, N_HEADS_DIFF, seq // qb, LANES, qb), BF16)],
        scratch_shapes=[pltpu.VMEM((WIDTH // LANES, tm, LANES), F32),
                        pltpu.VMEM((WIDTH // LANES, tm, LANES), F32)],
        compiler_params=_params("arbitrary"),
        name="proj",
    )(h, gain.reshape(1, d), w_b, cos, sa, sb)


def _rope_tables(seq):
    half = ROPE_DIM // 2
    inv = jnp.power(ROPE_THETA, -jnp.arange(half, dtype=F32) * (2.0 / ROPE_DIM))
    ang = jnp.arange(seq).astype(F32)[:, None] * inv[None, :]
    cos, sin = jnp.cos(ang), jnp.sin(ang)
    ch = jnp.arange(LANES) % HEAD_DIM
    cos_l = jnp.take(cos, ch % half, axis=1)
    sin_l = jnp.take(sin, ch % half, axis=1)
    cos_t = jnp.where(ch < ROPE_DIM, cos_l, 1.0)
    sa = jnp.where(ch < half, -sin_l, 0.0)
    sb = jnp.where((ch >= half) & (ch < ROPE_DIM), sin_l, 0.0)
    return cos_t, sa, sb


def _dswa_kernel(q_ref, kc_ref, kp_ref, vc_ref, vp_ref, o_ref, l_ref, vt_sc, bias_sc, k_sc, sa_sc, sb_sc, *,
                 lc):
    w = DSWA_W
    nblk = lc // w
    first_chunk = pl.program_id(2) == 0
    head_a = lax.broadcasted_iota(jnp.int32, (1, LANES), 1) < HEAD_DIM
    kj = lax.broadcasted_iota(jnp.int32, (2 * w, 2 * w), 0)
    qi = lax.broadcasted_iota(jnp.int32, (2 * w, 2 * w), 1) & (w - 1)
    dist = w + qi - kj
    band = (dist >= 0) & (dist <= w)
    bias_sc[0] = jnp.where(band, 0.0, NEG)
    bias_sc[1] = jnp.where(band & ((kj >= w) | jnp.logical_not(first_chunk)), 0.0, NEG)

    def transposed(v):
        return v.astype(F32).T.astype(BF16)

    for hp in range(WIDTH // LANES):
        lanes = slice(hp * LANES, (hp + 1) * LANES)
        vt_sc[0, lanes, :] = transposed(vp_ref[0, 0, :, lanes])
        for n in range(nblk):
            vt_sc[n + 1, lanes, :] = transposed(vc_ref[0, 0, n * w:(n + 1) * w, lanes])

    k_sc[0:w, :] = kp_ref[0, 0]
    k_sc[w:w + lc, :] = kc_ref[0, 0]

    def scores(n, hp):
        lanes = slice(hp * LANES, (hp + 1) * LANES)
        row0 = pl.multiple_of(n * w, w)
        qb = q_ref[0, 0, pl.ds(row0, w), lanes]
        zero = jnp.zeros_like(qb)
        qs = jnp.concatenate([jnp.where(head_a, qb, zero), jnp.where(head_a, zero, qb)], axis=0)
        kk = k_sc[pl.ds(row0, 2 * w), lanes]
        bias = bias_sc[jnp.where(n == 0, 1, 0)]
        return lax.dot_general(kk, qs, (((1,), (1,)), ((), ())), preferred_element_type=F32) + bias

    def finish(n, hp, s):
        lanes = slice(hp * LANES, (hp + 1) * LANES)
        row0 = pl.multiple_of(n * w, w)
        m = jnp.max(s, axis=0, keepdims=True)
        p = jnp.exp2(s - m)
        den = jnp.sum(p, axis=0, keepdims=True)
        vv = jnp.concatenate([vt_sc[n, lanes, :], vt_sc[n + 1, lanes, :]], axis=1)
        o = jnp.dot(vv, p.astype(BF16), preferred_element_type=F32) * (1.0 / den)
        lse = m + jnp.log2(den)
        o_sel = jnp.concatenate([o[:HEAD_DIM, :w], o[HEAD_DIM:, w:]], axis=0)
        l_sel = jnp.concatenate([jnp.broadcast_to(lse[:, :w], (HEAD_DIM, w)),
                                 jnp.broadcast_to(lse[:, w:], (HEAD_DIM, w))], axis=0)
        o_ref[0, 0, pl.ds(row0, w), lanes] = o_sel.T.astype(o_ref.dtype)
        l_ref[0, 0, pl.ds(row0, w), lanes] = l_sel.T

    nhp = WIDTH // LANES

    def issue(n, s_ref):
        for hp in range(nhp):
            s_ref[hp] = scores(n, hp)

    def consume(n, s_ref):
        for hp in range(nhp):
            finish(n, hp, s_ref[hp])

    issue(0, sa_sc)

    def pair(j, carry):
        n0 = 2 * j
        issue(n0 + 1, sb_sc)
        consume(n0, sa_sc)
        issue(jnp.minimum(n0 + 2, nblk - 1), sa_sc)
        consume(n0 + 1, sb_sc)
        return carry

    lax.fori_loop(0, nblk // 2, pair, 0)


def _dswa_group(qkv, splits):
    b, d, sub, _ = qkv.shape
    lc = min(sub, 1024)
    per = lc // DSWA_W
    assert per % 2 == 0, "the kernel walks query blocks in pairs"
    sq, sk, sv = splits

    def cur(split):
        return pl.BlockSpec((1, 1, lc, WIDTH), lambda bi, r, c: (bi, r, c, split))

    def prv(split):
        return pl.BlockSpec((1, 1, DSWA_W, WIDTH), lambda bi, r, c: (bi, r, jnp.maximum(c * per - 1, 0), split))

    out = pl.BlockSpec((1, 1, lc, WIDTH), lambda bi, r, c: (bi, r, c, 0))
    return pl.pallas_call(
        functools.partial(_dswa_kernel, lc=lc),
        grid=(b, d, sub // lc),
        in_specs=[cur(sq), cur(sk), prv(sk), cur(sv), prv(sv)],
        out_specs=[out, out],
        out_shape=[jax.ShapeDtypeStruct((b, d, sub, WIDTH), BF16),
                   jax.ShapeDtypeStruct((b, d, sub, WIDTH), F32)],
        scratch_shapes=[pltpu.VMEM((per + 1, WIDTH, DSWA_W), BF16),
                        pltpu.VMEM((2, 2 * DSWA_W, 2 * DSWA_W), F32),
                        pltpu.VMEM((lc + DSWA_W, WIDTH), BF16),
                        pltpu.VMEM((WIDTH // LANES, 2 * DSWA_W, 2 * DSWA_W), F32),
                        pltpu.VMEM((WIDTH // LANES, 2 * DSWA_W, 2 * DSWA_W), F32)],
        compiler_params=_params("arbitrary", "arbitrary", "arbitrary"),
        name=f"dswa_d{d}",
    )(qkv, qkv, qkv, qkv, qkv)


def _diff_kernel(lam_ref, gain_ref, q_ref, k_ref, vt_ref, o_ref, m_sc, l_sc, acc_sc, sa_sc, sb_sc, *, qb,
                 lambda_init):
    qi = pl.program_id(2)
    nh = q_ref.shape[2] // LANES
    heads = range(nh)
    head_a = lax.broadcasted_iota(jnp.int32, (1, LANES), 1) < HEAD_DIM

    def stacked_queries(h):
        q = q_ref[0, :, h * LANES:(h + 1) * LANES]
        zero = jnp.zeros_like(q)
        return jnp.concatenate([jnp.where(head_a, q, zero), jnp.where(head_a, zero, q)], axis=0)

    qs = [stacked_queries(h) for h in heads]
    m_sc[...] = jnp.full(m_sc.shape, NEG, F32)
    l_sc[...] = jnp.zeros(l_sc.shape, F32)
    acc_sc[...] = jnp.zeros(acc_sc.shape, F32)

    def issue(blk, s_ref):
        rows = pl.ds(pl.multiple_of(blk * qb, qb), qb)
        for h in heads:
            kb = k_ref[0, rows, h * LANES:(h + 1) * LANES]
            s_ref[h] = lax.dot_general(kb, qs[h], (((1,), (1,)), ((), ())), preferred_element_type=F32)

    def consume(blk, s_ref, mask):
        for h in heads:
            s = s_ref[h]
            if mask is not None:
                s = jnp.where(mask, s, NEG)
            m_old = m_sc[h]
            m_new = jnp.maximum(m_old, jnp.max(s, axis=0, keepdims=True))
            alpha = jnp.exp2(m_old - m_new)
            p = jnp.exp2(s - m_new)
            l_sc[h] = alpha * l_sc[h] + jnp.sum(p, axis=0, keepdims=True)
            acc_sc[h] = alpha * acc_sc[h] + jnp.dot(vt_ref[0, h, blk], p.astype(BF16),
                                                    preferred_element_type=F32)
            m_sc[h] = m_new

    last = jnp.maximum(qi - 1, 0)
    key = lax.broadcasted_iota(jnp.int32, (qb, 2 * qb), 0)
    qry = lax.broadcasted_iota(jnp.int32, (qb, 2 * qb), 1) & (qb - 1)
    issue(qi, sa_sc)
    issue(0, sb_sc)
    consume(qi, sa_sc, key <= qry)

    def pair(j, carry):
        b0 = 2 * j
        issue(b0 + 1, sa_sc)
        consume(b0, sb_sc, None)
        issue(jnp.minimum(b0 + 2, last), sb_sc)
        consume(b0 + 1, sa_sc, None)
        return carry

    lax.fori_loop(0, qi // 2, pair, 0)

    @pl.when(qi % 2 == 1)
    def _():
        consume(qi - 1, sb_sc, None)

    lam_p = lam_ref[...]
    lam = (jnp.exp(jnp.sum(lam_p[0:1] * lam_p[1:2], axis=1, keepdims=True))
           - jnp.exp(jnp.sum(lam_p[2:3] * lam_p[3:4], axis=1, keepdims=True)) + lambda_init)
    for h in heads:
        o_all = acc_sc[h] / l_sc[h]
        o = o_all[:, :qb] - lam * o_all[:, qb:]
        o = o * lax.rsqrt(jnp.mean(o * o, axis=0, keepdims=True) + SUBLN_EPS) * gain_ref[...]
        o_ref[0, :, h * LANES:(h + 1) * LANES] = (o * (1.0 - lambda_init)).T.astype(o_ref.dtype)


def _diff(u3, vt, lam_p, subln_gain, lambda_init, qb, nh=2):
    b, s, _ = u3.shape
    per = WIDTH // (nh * LANES)
    nb = s // qb
    return pl.pallas_call(
        functools.partial(_diff_kernel, qb=qb, lambda_init=lambda_init),
        grid=(b, N_HEADS_DIFF // nh, nb),
        in_specs=[pl.BlockSpec(lam_p.shape, lambda bi, g, qi: (0, 0)),
                  pl.BlockSpec((LANES, 1), lambda bi, g, qi: (0, 0)),
                  pl.BlockSpec((1, qb, nh * LANES), lambda bi, g, qi: (bi, qi, COL_QB * per + g)),
                  pl.BlockSpec((1, s, nh * LANES), lambda bi, g, qi: (bi, 0, COL_KB * per + g)),
                  pl.BlockSpec((1, nh, nb, LANES, qb), lambda bi, g, qi: (bi, g, 0, 0, 0))],
        out_specs=pl.BlockSpec((1, qb, nh * LANES), lambda bi, g, qi: (bi, qi, g)),
        out_shape=jax.ShapeDtypeStruct((b, s, WIDTH), BF16),
        scratch_shapes=[pltpu.VMEM((nh, 1, 2 * qb), F32), pltpu.VMEM((nh, 1, 2 * qb), F32),
                        pltpu.VMEM((nh, LANES, 2 * qb), F32),
                        pltpu.VMEM((nh, qb, 2 * qb), F32), pltpu.VMEM((nh, qb, 2 * qb), F32)],
        compiler_params=_params("arbitrary", "arbitrary", "arbitrary"),
        name="diff",
    )(lam_p, subln_gain.reshape(LANES, 1), u3, u3, vt)


def _silu_of_half(half):
    return half + half * jnp.tanh(half)


def _post_kernel(h_ref, ga_ref, gb_ref, o1_ref, l1_ref, o4_ref, l4_ref, o16_ref, l16_ref, ob_ref, p_ref,
                 wo_ref, ng_ref, wg_ref, wp_ref, fg_ref, out_ref, on_sc, ln_sc, o4_sc, l4_sc, *, final):
    tm = h_ref.shape[0]

    n4, n16 = tm // 4, tm // 16
    merged = []
    for j in range(WIDTH // LANES):
        lanes = slice(j * LANES, (j + 1) * LANES)
        for r4 in range(4):
            for k in range(4):
                o4_sc[pl.ds(k, n16, stride=4), :] = o16_ref[0, 4 * k + r4, :, lanes].astype(F32)
                l4_sc[pl.ds(k, n16, stride=4), :] = l16_ref[0, 4 * k + r4, :, lanes]
            o16, l16 = o4_sc[...], l4_sc[...]
            o4, l4 = o4_ref[0, r4, :, lanes].astype(F32), l4_ref[0, r4, :, lanes]
            top = jnp.maximum(l4, l16)
            w4, w16 = jnp.exp2(l4 - top), jnp.exp2(l16 - top)
            den = w4 + w16
            on_sc[pl.ds(r4, n4, stride=4), :] = (w4 * o4 + w16 * o16) / den
            ln_sc[pl.ds(r4, n4, stride=4), :] = top + jnp.log2(den)
        o1, l1 = o1_ref[:, lanes].astype(F32), l1_ref[:, lanes]
        ox, lx = on_sc[...], ln_sc[...]
        top = jnp.maximum(l1, lx)
        w1, wx = jnp.exp2(l1 - top), jnp.exp2(lx - top)
        merged.append((w1 * o1 + wx * ox) / (w1 + wx))
    oa = jnp.concatenate(merged, axis=1)

    ya = (oa * _silu_of_half(ga_ref[...].astype(F32))).astype(BF16)
    yb = (ob_ref[...].astype(F32) * _silu_of_half(gb_ref[...].astype(F32))).astype(BF16)
    y = (jnp.dot(ya, wo_ref[0:WIDTH, :], preferred_element_type=F32)
         + jnp.dot(yb, wo_ref[WIDTH:2 * WIDTH, :], preferred_element_type=F32))
    h1 = h_ref[...] + y
    n = h1 * lax.rsqrt(jnp.mean(h1 * h1, axis=-1, keepdims=True) + RMS_EPS) * ng_ref[...]
    t = jnp.tanh(jnp.dot(n.astype(BF16), wg_ref[...], preferred_element_type=F32))
    ple_half = jnp.dot(p_ref[...].astype(BF16), wp_ref[...], preferred_element_type=F32)
    h2 = h1 + ple_half + ple_half * t
    if final:
        h2 = h2 * lax.rsqrt(jnp.mean(h2 * h2, axis=-1, keepdims=True) + RMS_EPS) * fg_ref[...]
    out_ref[...] = h2


def _post(h, u, g1, g4, g16, ob, p_i, wo_b, ple_gain, wg_b, wp_b, final_gain, final, batch, tm=512):
    m, d = h.shape
    nt = m // batch // tm
    row = lambda i: (i, 0)
    const = lambda i: (0, 0)
    res = lambda i: (i // nt, 0, i % nt, 0)
    grp = lambda dil: pl.BlockSpec((1, dil, tm // dil, WIDTH), res)
    return pl.pallas_call(
        functools.partial(_post_kernel, final=final),
        grid=(m // tm,),
        in_specs=[pl.BlockSpec((tm, d), row),
                  pl.BlockSpec((tm, WIDTH), lambda i: (i, COL_GA)),
                  pl.BlockSpec((tm, WIDTH), lambda i: (i, COL_GB)),
                  pl.BlockSpec((tm, WIDTH), row), pl.BlockSpec((tm, WIDTH), row),
                  grp(4), grp(4), grp(16), grp(16),
                  pl.BlockSpec((tm, WIDTH), row),
                  pl.BlockSpec((tm, p_i.shape[1]), row),
                  pl.BlockSpec(wo_b.shape, const),
                  pl.BlockSpec((1, d), const),
                  pl.BlockSpec(wg_b.shape, const),
                  pl.BlockSpec(wp_b.shape, const),
                  pl.BlockSpec((1, d), const)],
        out_specs=pl.BlockSpec((tm, d), row),
        out_shape=jax.ShapeDtypeStruct((m, d), F32),
        scratch_shapes=[pltpu.VMEM((tm, LANES), F32), pltpu.VMEM((tm, LANES), F32),
                        pltpu.VMEM((tm // 4, LANES), F32), pltpu.VMEM((tm // 4, LANES), F32)],
        compiler_params=_params("arbitrary"),
        name="post",
    )(h, u, u, g1[0].reshape(m, WIDTH), g1[1].reshape(m, WIDTH), g4[0], g4[1], g16[0], g16[1], ob, p_i,
      wo_b, ple_gain.reshape(1, d), wg_b, wp_b, final_gain.reshape(1, d))


def kernel(x, p, attn_norm_gain, w_in, w_out, lambda_q1, lambda_k1, lambda_q2, lambda_k2, subln_gain,
           ple_norm_gain, w_ple_gate, w_ple, final_norm_gain):
    b, s, d = x.shape
    depth = w_in.shape[0]
    h = x.reshape(b * s, d)
    cos, sa, sb = _rope_tables(s)
    col = jnp.arange(IN_COLS) // WIDTH
    col_scale = jnp.where((col == COL_QA) | (col == COL_QB), HEAD_DIM ** -0.5 * LOG2_E,
                          jnp.where((col == COL_GA) | (col == COL_GB), 0.5, 1.0)).astype(F32)
    a_splits = (COL_QA, COL_KA, COL_VA)
    for i in range(depth):
        lambda_init = 0.8 - 0.6 * math.exp(-0.3 * i)
        w_in_b = (w_in[i] * col_scale[None, :]).astype(BF16)
        u, qkv4, qkv16, vt = _proj(h, attn_norm_gain[i], w_in_b, cos, sa, sb, b, DIFF_QB)
        u3 = u.reshape(b, s, IN_COLS)
        g1 = _dswa_group(u3.reshape(b, 1, s, IN_COLS), a_splits)
        g4 = _dswa_group(qkv4, a_splits)
        g16 = _dswa_group(qkv16, a_splits)
        lam_p = jnp.stack([lambda_q1[i], lambda_k1[i], lambda_q2[i], lambda_k2[i]]).astype(F32)
        ob = _diff(u3, vt, lam_p, subln_gain[i], lambda_init, DIFF_QB)
        h = _post(h, u, g1, g4, g16, ob.reshape(b * s, WIDTH), p[i].reshape(b * s, -1),
                  w_out[i].astype(BF16), ple_norm_gain[i], (0.5 * w_ple_gate[i]).astype(BF16),
                  (0.5 * w_ple[i]).astype(BF16), final_norm_gain, final=i == depth - 1, batch=b)
    return h.reshape(b, s, d)
```

```python
import functools
import math

import jax
import jax.numpy as jnp
from jax import lax
from jax.experimental import pallas as pl
from jax.experimental.pallas import tpu as pltpu

HEAD_DIM = 64
LANES = 128
DSWA_W = 128
DSWA_UNROLL = 4
N_HEADS_DIFF = 4
DIFF_QB = 512
VT_ROWS = LANES + 16
WIDTH = 512
IN_COLS = 8 * WIDTH
ROPE_THETA = 500000.0
ROPE_DIM = HEAD_DIM // 4
RMS_EPS = 1e-6
SUBLN_EPS = 1e-5
NEG = -1e30
LOG2_E = math.log2(math.e)
VMEM_LIMIT = 48 * 1024 * 1024
PROJ_VMEM_LIMIT = 58 * 1024 * 1024

COL_QA, COL_KA, COL_VA, COL_GA, COL_QB, COL_KB, COL_VB, COL_GB = range(8)
ROPE_SPLITS = (COL_QA, COL_KA, COL_QB, COL_KB)

BF16 = jnp.bfloat16
F32 = jnp.float32


def _params(*sem):
    return pltpu.CompilerParams(dimension_semantics=sem, vmem_limit_bytes=VMEM_LIMIT)


def _proj_kernel(x_ref, g_ref, w32_ref, ws_ref, cos_ref, sa_ref, sb_ref, o_ref, d4_ref, d16_ref, vt_ref, buf_sc,
                 buf4_sc, w_ref, *, qb):
    tm = x_ref.shape[0]

    @pl.when(pl.program_id(0) == 0)
    def _():
        for c in range(IN_COLS // WIDTH):
            cols = slice(c * WIDTH, (c + 1) * WIDTH)
            w_ref[:, cols] = (w32_ref[0, :, cols] * ws_ref[:, cols]).astype(BF16)

    x = x_ref[...]
    ms = jnp.mean(x * x, axis=-1, keepdims=True)
    xn = (x * lax.rsqrt(ms + RMS_EPS) * g_ref[...]).astype(BF16)
    cos, sa, sb = cos_ref[...], sa_ref[...], sb_ref[...]
    for c in range(IN_COLS // WIDTH):
        cols = slice(c * WIDTH, (c + 1) * WIDTH)
        acc = jnp.dot(xn, w_ref[:, cols], preferred_element_type=F32)
        if c in ROPE_SPLITS:
            blocks = []
            for j in range(WIDTH // LANES):
                blk = acc[:, j * LANES:(j + 1) * LANES]
                blocks.append(blk * cos + pltpu.roll(blk, LANES - ROPE_DIM // 2, 1) * sa
                              + pltpu.roll(blk, ROPE_DIM // 2, 1) * sb)
            acc = jnp.concatenate(blocks, axis=1)
        o_ref[:, cols] = acc.astype(BF16)
        if c in (COL_QA, COL_KA, COL_VA):
            n4, n16 = tm // 4, tm // 16
            for j in range(WIDTH // LANES):
                lanes = slice(c * WIDTH + j * LANES, c * WIDTH + (j + 1) * LANES)
                buf_sc[j] = acc[:, j * LANES:(j + 1) * LANES]
                for r4 in range(4):
                    rows4 = buf_sc[j, pl.ds(r4, n4, stride=4), :]
                    d4_ref[0, r4, :, lanes] = rows4.astype(BF16)
                    buf4_sc[j, r4 * n4:(r4 + 1) * n4, :] = rows4
                for r4 in range(4):
                    for k in range(4):
                        rows16 = buf4_sc[j, pl.ds(r4 * n4 + k, n16, stride=4), :]
                        d16_ref[0, 4 * k + r4, :, lanes] = rows16.astype(BF16)
        if c == COL_VB:
            ones_row = (lax.broadcasted_iota(jnp.int32, (VT_ROWS - LANES, qb), 0) == 0).astype(BF16)
            for kb in range(tm // qb):
                for hd in range(N_HEADS_DIFF):
                    blk = acc[kb * qb:(kb + 1) * qb, hd * LANES:(hd + 1) * LANES]
                    vt_ref[0, hd, kb, 0:LANES, :] = blk.T.astype(BF16)
                    vt_ref[0, hd, kb, LANES:VT_ROWS, :] = ones_row


def _proj(h, gain, w_in, layer, col_scale, cos, sa, sb, batch, qb, tm=512):
    m, d = h.shape
    seq = m // batch
    nt = seq // tm
    row = lambda i: (i, 0)
    tab = lambda i: (i % nt, 0)
    res = lambda i: (i // nt, 0, i % nt, 0)
    return pl.pallas_call(
        functools.partial(_proj_kernel, qb=qb),
        grid=(m // tm,),
        in_specs=[pl.BlockSpec((tm, d), row),
                  pl.BlockSpec((1, d), lambda i: (0, 0)),
                  pl.BlockSpec((1, d, IN_COLS), lambda i: (layer, 0, 0), pipeline_mode=pl.Buffered(1)),
                  pl.BlockSpec((1, IN_COLS), lambda i: (0, 0)),
                  pl.BlockSpec((tm, LANES), tab),
                  pl.BlockSpec((tm, LANES), tab),
                  pl.BlockSpec((tm, LANES), tab)],
        out_specs=[pl.BlockSpec((tm, IN_COLS), row),
                   pl.BlockSpec((1, 4, tm // 4, 3 * WIDTH), res),
                   pl.BlockSpec((1, 16, tm // 16, 3 * WIDTH), res),
                   pl.BlockSpec((1, N_HEADS_DIFF, tm // qb, VT_ROWS, qb), lambda i: (i // nt, 0, i % nt, 0, 0))],
        out_shape=[jax.ShapeDtypeStruct((m, IN_COLS), BF16),
                   jax.ShapeDtypeStruct((batch, 4, seq // 4, 3 * WIDTH), BF16),
                   jax.ShapeDtypeStruct((batch, 16, seq // 16, 3 * WIDTH), BF16),
                   jax.ShapeDtypeStruct((batch, N_HEADS_DIFF, seq // qb, VT_ROWS, qb), BF16)],
        scratch_shapes=[pltpu.VMEM((WIDTH // LANES, tm, LANES), F32),
                        pltpu.VMEM((WIDTH // LANES, tm, LANES), F32),
                        pltpu.VMEM((d, IN_COLS), BF16)],
        compiler_params=pltpu.CompilerParams(dimension_semantics=("arbitrary",),
                                             vmem_limit_bytes=PROJ_VMEM_LIMIT),
        name="proj",
    )(h, gain.reshape(1, d), w_in, col_scale.reshape(1, IN_COLS), cos, sa, sb)


def _rope_tables(seq):
    half = ROPE_DIM // 2
    inv = jnp.power(ROPE_THETA, -jnp.arange(half, dtype=F32) * (2.0 / ROPE_DIM))
    ang = jnp.arange(seq).astype(F32)[:, None] * inv[None, :]
    cos, sin = jnp.cos(ang), jnp.sin(ang)
    ch = jnp.arange(LANES) % HEAD_DIM
    cos_l = jnp.take(cos, ch % half, axis=1)
    sin_l = jnp.take(sin, ch % half, axis=1)
    cos_t = jnp.where(ch < ROPE_DIM, cos_l, 1.0)
    sa = jnp.where(ch < half, -sin_l, 0.0)
    sb = jnp.where((ch >= half) & (ch < ROPE_DIM), sin_l, 0.0)
    return cos_t, sa, sb


def _dswa_kernel(q_ref, kc_ref, kp_ref, vc_ref, vp_ref, o_ref, l_ref, vt_sc, bias_sc, k_sc, sa_sc, sb_sc, *,
                 lc):
    w = DSWA_W
    nblk = lc // w
    first_chunk = pl.program_id(2) == 0
    head_a = lax.broadcasted_iota(jnp.int32, (1, LANES), 1) < HEAD_DIM
    kj = lax.broadcasted_iota(jnp.int32, (2 * w, 2 * w), 0)
    qi = lax.broadcasted_iota(jnp.int32, (2 * w, 2 * w), 1) & (w - 1)
    dist = w + qi - kj
    band = (dist >= 0) & (dist <= w)
    bias_sc[0] = jnp.where(band, 0.0, NEG)
    bias_sc[1] = jnp.where(band & ((kj >= w) | jnp.logical_not(first_chunk)), 0.0, NEG)

    def transposed(v):
        return v.astype(F32).T.astype(BF16)

    for hp in range(WIDTH // LANES):
        lanes = slice(hp * LANES, (hp + 1) * LANES)
        vt_sc[0, lanes, :] = transposed(vp_ref[0, 0, :, lanes])
        for n in range(nblk):
            vt_sc[n + 1, lanes, :] = transposed(vc_ref[0, 0, n * w:(n + 1) * w, lanes])

    k_sc[0:w, :] = kp_ref[0, 0]
    k_sc[w:w + lc, :] = kc_ref[0, 0]

    def scores(n, hp):
        lanes = slice(hp * LANES, (hp + 1) * LANES)
        row0 = pl.multiple_of(n * w, w)
        qb = q_ref[0, 0, pl.ds(row0, w), lanes]
        zero = jnp.zeros_like(qb)
        qs = jnp.concatenate([jnp.where(head_a, qb, zero), jnp.where(head_a, zero, qb)], axis=0)
        kk = k_sc[pl.ds(row0, 2 * w), lanes]
        bias = bias_sc[jnp.where(n == 0, 1, 0)]
        return lax.dot_general(kk, qs, (((1,), (1,)), ((), ())), preferred_element_type=F32) + bias

    def finish(n, hp, s):
        lanes = slice(hp * LANES, (hp + 1) * LANES)
        row0 = pl.multiple_of(n * w, w)
        m = jnp.max(s, axis=0, keepdims=True)
        p = jnp.exp2(s - m)
        den = jnp.sum(p, axis=0, keepdims=True)
        vv = jnp.concatenate([vt_sc[n, lanes, :], vt_sc[n + 1, lanes, :]], axis=1)
        o = jnp.dot(vv, p.astype(BF16), preferred_element_type=F32) * (1.0 / den)
        lse = m + jnp.log2(den)
        o_sel = jnp.concatenate([o[:HEAD_DIM, :w], o[HEAD_DIM:, w:]], axis=0)
        l_sel = jnp.concatenate([jnp.broadcast_to(lse[:, :w], (HEAD_DIM, w)),
                                 jnp.broadcast_to(lse[:, w:], (HEAD_DIM, w))], axis=0)
        o_ref[0, 0, pl.ds(row0, w), lanes] = o_sel.T.astype(o_ref.dtype)
        l_ref[0, 0, pl.ds(row0, w), lanes] = l_sel.T

    nhp = WIDTH // LANES

    def issue(n, s_ref):
        for hp in range(nhp):
            s_ref[hp] = scores(n, hp)

    def consume(n, s_ref):
        for hp in range(nhp):
            finish(n, hp, s_ref[hp])

    issue(0, sa_sc)

    def group(j, carry):
        n0 = DSWA_UNROLL * j
        for i in range(DSWA_UNROLL):
            cur, nxt = (sa_sc, sb_sc) if i % 2 == 0 else (sb_sc, sa_sc)
            issue(jnp.minimum(n0 + i + 1, nblk - 1), nxt)
            consume(n0 + i, cur)
        return carry

    lax.fori_loop(0, nblk // DSWA_UNROLL, group, 0)


def _dswa_group(qkv, splits):
    b, d, sub, _ = qkv.shape
    lc = min(sub, 1024)
    per = lc // DSWA_W
    assert per % DSWA_UNROLL == 0, "the kernel walks query blocks in groups of DSWA_UNROLL"
    sq, sk, sv = splits

    def cur(split):
        return pl.BlockSpec((1, 1, lc, WIDTH), lambda bi, r, c: (bi, r, c, split))

    def prv(split):
        return pl.BlockSpec((1, 1, DSWA_W, WIDTH), lambda bi, r, c: (bi, r, jnp.maximum(c * per - 1, 0), split))

    out = pl.BlockSpec((1, 1, lc, WIDTH), lambda bi, r, c: (bi, r, c, 0))
    return pl.pallas_call(
        functools.partial(_dswa_kernel, lc=lc),
        grid=(b, d, sub // lc),
        in_specs=[cur(sq), cur(sk), prv(sk), cur(sv), prv(sv)],
        out_specs=[out, out],
        out_shape=[jax.ShapeDtypeStruct((b, d, sub, WIDTH), BF16),
                   jax.ShapeDtypeStruct((b, d, sub, WIDTH), F32)],
        scratch_shapes=[pltpu.VMEM((per + 1, WIDTH, DSWA_W), BF16),
                        pltpu.VMEM((2, 2 * DSWA_W, 2 * DSWA_W), F32),
                        pltpu.VMEM((lc + DSWA_W, WIDTH), BF16),
                        pltpu.VMEM((WIDTH // LANES, 2 * DSWA_W, 2 * DSWA_W), F32),
                        pltpu.VMEM((WIDTH // LANES, 2 * DSWA_W, 2 * DSWA_W), F32)],
        compiler_params=_params("arbitrary", "arbitrary", "arbitrary"),
        name=f"dswa_d{d}",
    )(qkv, qkv, qkv, qkv, qkv)


def _diff_kernel(lam_ref, gain_ref, q_ref, k_ref, vt_ref, o_ref, m_sc, acc_sc, sa_sc, sb_sc, *, qb,
                 lambda_init):
    qi = pl.program_id(2)
    nh = q_ref.shape[2] // LANES
    heads = range(nh)
    head_a = lax.broadcasted_iota(jnp.int32, (1, LANES), 1) < HEAD_DIM

    def stacked_queries(h):
        q = q_ref[0, :, h * LANES:(h + 1) * LANES]
        zero = jnp.zeros_like(q)
        return jnp.concatenate([jnp.where(head_a, q, zero), jnp.where(head_a, zero, q)], axis=0)

    qs = [stacked_queries(h) for h in heads]
    m_sc[...] = jnp.full(m_sc.shape, NEG, F32)
    acc_sc[...] = jnp.zeros(acc_sc.shape, F32)

    def issue(blk, s_ref):
        rows = pl.ds(pl.multiple_of(blk * qb, qb), qb)
        for h in heads:
            kb = k_ref[0, rows, h * LANES:(h + 1) * LANES]
            s_ref[h] = lax.dot_general(kb, qs[h], (((1,), (1,)), ((), ())), preferred_element_type=F32)

    def consume(blk, s_ref, mask):
        for h in heads:
            s = s_ref[h]
            if mask is not None:
                s = jnp.where(mask, s, NEG)
            m_old = m_sc[h]
            m_new = jnp.maximum(m_old, jnp.max(s, axis=0, keepdims=True))
            alpha = jnp.exp2(m_old - m_new)
            p = jnp.exp2(s - m_new).astype(BF16)
            acc_sc[h] = alpha * acc_sc[h] + jnp.dot(vt_ref[0, h, blk], p, preferred_element_type=F32)
            m_sc[h] = m_new

    last = jnp.maximum(qi - 1, 0)
    key = lax.broadcasted_iota(jnp.int32, (qb, 2 * qb), 0)
    qry = lax.broadcasted_iota(jnp.int32, (qb, 2 * qb), 1) & (qb - 1)
    issue(qi, sa_sc)
    issue(0, sb_sc)
    consume(qi, sa_sc, key <= qry)

    def pair(j, carry):
        b0 = 2 * j
        issue(b0 + 1, sa_sc)
        consume(b0, sb_sc, None)
        issue(jnp.minimum(b0 + 2, last), sb_sc)
        consume(b0 + 1, sa_sc, None)
        return carry

    lax.fori_loop(0, qi // 2, pair, 0)

    @pl.when(qi % 2 == 1)
    def _():
        consume(qi - 1, sb_sc, None)

    lam_p = lam_ref[...]
    lam = (jnp.exp(jnp.sum(lam_p[0:1] * lam_p[1:2], axis=1, keepdims=True))
           - jnp.exp(jnp.sum(lam_p[2:3] * lam_p[3:4], axis=1, keepdims=True)) + lambda_init)
    for h in heads:
        o_all = acc_sc[h, 0:LANES, :] / acc_sc[h, LANES:LANES + 1, :]
        o = o_all[:, :qb] - lam * o_all[:, qb:]
        o = o * lax.rsqrt(jnp.mean(o * o, axis=0, keepdims=True) + SUBLN_EPS) * gain_ref[...]
        o_ref[0, :, h * LANES:(h + 1) * LANES] = (o * (1.0 - lambda_init)).T.astype(o_ref.dtype)


def _diff(u3, vt, lam_p, subln_gain, lambda_init, qb, nh=2):
    b, s, _ = u3.shape
    per = WIDTH // (nh * LANES)
    nb = s // qb
    return pl.pallas_call(
        functools.partial(_diff_kernel, qb=qb, lambda_init=lambda_init),
        grid=(b, N_HEADS_DIFF // nh, nb),
        in_specs=[pl.BlockSpec(lam_p.shape, lambda bi, g, qi: (0, 0)),
                  pl.BlockSpec((LANES, 1), lambda bi, g, qi: (0, 0)),
                  pl.BlockSpec((1, qb, nh * LANES), lambda bi, g, qi: (bi, qi, COL_QB * per + g)),
                  pl.BlockSpec((1, s, nh * LANES), lambda bi, g, qi: (bi, 0, COL_KB * per + g)),
                  pl.BlockSpec((1, nh, nb, VT_ROWS, qb), lambda bi, g, qi: (bi, g, 0, 0, 0))],
        out_specs=pl.BlockSpec((1, qb, nh * LANES), lambda bi, g, qi: (bi, qi, g)),
        out_shape=jax.ShapeDtypeStruct((b, s, WIDTH), BF16),
        scratch_shapes=[pltpu.VMEM((nh, 1, 2 * qb), F32),
                        pltpu.VMEM((nh, VT_ROWS, 2 * qb), F32),
                        pltpu.VMEM((nh, qb, 2 * qb), F32), pltpu.VMEM((nh, qb, 2 * qb), F32)],
        compiler_params=_params("arbitrary", "arbitrary", "arbitrary"),
        name="diff",
    )(lam_p, subln_gain.reshape(LANES, 1), u3, u3, vt)


def _silu_of_half(half):
    return half + half * jnp.tanh(half)


def _post_kernel(h_ref, ga_ref, gb_ref, o1_ref, l1_ref, o4_ref, l4_ref, o16_ref, l16_ref, ob_ref, p_ref,
                 wo_ref, ng_ref, wg_ref, wp_ref, fg_ref, out_ref, on_sc, ln_sc, o4_sc, l4_sc, *, final):
    tm = h_ref.shape[0]

    n4, n16 = tm // 4, tm // 16
    merged = []
    for j in range(WIDTH // LANES):
        lanes = slice(j * LANES, (j + 1) * LANES)
        for r4 in range(4):
            for k in range(4):
                o4_sc[pl.ds(k, n16, stride=4), :] = o16_ref[0, 4 * k + r4, :, lanes].astype(F32)
                l4_sc[pl.ds(k, n16, stride=4), :] = l16_ref[0, 4 * k + r4, :, lanes]
            o16, l16 = o4_sc[...], l4_sc[...]
            o4, l4 = o4_ref[0, r4, :, lanes].astype(F32), l4_ref[0, r4, :, lanes]
            top = jnp.maximum(l4, l16)
            w4, w16 = jnp.exp2(l4 - top), jnp.exp2(l16 - top)
            den = w4 + w16
            on_sc[pl.ds(r4, n4, stride=4), :] = (w4 * o4 + w16 * o16) / den
            ln_sc[pl.ds(r4, n4, stride=4), :] = top + jnp.log2(den)
        o1, l1 = o1_ref[:, lanes].astype(F32), l1_ref[:, lanes]
        ox, lx = on_sc[...], ln_sc[...]
        top = jnp.maximum(l1, lx)
        w1, wx = jnp.exp2(l1 - top), jnp.exp2(lx - top)
        merged.append((w1 * o1 + wx * ox) / (w1 + wx))
    oa = jnp.concatenate(merged, axis=1)

    ya = (oa * _silu_of_half(ga_ref[...].astype(F32))).astype(BF16)
    yb = (ob_ref[...].astype(F32) * _silu_of_half(gb_ref[...].astype(F32))).astype(BF16)
    y = (jnp.dot(ya, wo_ref[0:WIDTH, :], preferred_element_type=F32)
         + jnp.dot(yb, wo_ref[WIDTH:2 * WIDTH, :], preferred_element_type=F32))
    h1 = h_ref[...] + y
    n = h1 * lax.rsqrt(jnp.mean(h1 * h1, axis=-1, keepdims=True) + RMS_EPS) * ng_ref[...]
    t = jnp.tanh(jnp.dot(n.astype(BF16), wg_ref[...], preferred_element_type=F32))
    ple_half = jnp.dot(p_ref[...].astype(BF16), wp_ref[...], preferred_element_type=F32)
    h2 = h1 + ple_half + ple_half * t
    if final:
        h2 = h2 * lax.rsqrt(jnp.mean(h2 * h2, axis=-1, keepdims=True) + RMS_EPS) * fg_ref[...]
    out_ref[...] = h2


def _post(h, u, g1, g4, g16, ob, p_i, wo_b, ple_gain, wg_b, wp_b, final_gain, final, batch, tm=512):
    m, d = h.shape
    nt = m // batch // tm
    row = lambda i: (i, 0)
    const = lambda i: (0, 0)
    res = lambda i: (i // nt, 0, i % nt, 0)
    grp = lambda dil: pl.BlockSpec((1, dil, tm // dil, WIDTH), res)
    return pl.pallas_call(
        functools.partial(_post_kernel, final=final),
        grid=(m // tm,),
        in_specs=[pl.BlockSpec((tm, d), row),
                  pl.BlockSpec((tm, WIDTH), lambda i: (i, COL_GA)),
                  pl.BlockSpec((tm, WIDTH), lambda i: (i, COL_GB)),
                  pl.BlockSpec((tm, WIDTH), row), pl.BlockSpec((tm, WIDTH), row),
                  grp(4), grp(4), grp(16), grp(16),
                  pl.BlockSpec((tm, WIDTH), row),
                  pl.BlockSpec((tm, p_i.shape[1]), row),
                  pl.BlockSpec(wo_b.shape, const),
                  pl.BlockSpec((1, d), const),
                  pl.BlockSpec(wg_b.shape, const),
                  pl.BlockSpec(wp_b.shape, const),
                  pl.BlockSpec((1, d), const)],
        out_specs=pl.BlockSpec((tm, d), row),
        out_shape=jax.ShapeDtypeStruct((m, d), F32),
        scratch_shapes=[pltpu.VMEM((tm, LANES), F32), pltpu.VMEM((tm, LANES), F32),
                        pltpu.VMEM((tm // 4, LANES), F32), pltpu.VMEM((tm // 4, LANES), F32)],
        compiler_params=_params("arbitrary"),
        name="post",
    )(h, u, u, g1[0].reshape(m, WIDTH), g1[1].reshape(m, WIDTH), g4[0], g4[1], g16[0], g16[1], ob, p_i,
      wo_b, ple_gain.reshape(1, d), wg_b, wp_b, final_gain.reshape(1, d))


def kernel(x, p, attn_norm_gain, w_in, w_out, lambda_q1, lambda_k1, lambda_q2, lambda_k2, subln_gain,
           ple_norm_gain, w_ple_gate, w_ple, final_norm_gain):
    b, s, d = x.shape
    depth = w_in.shape[0]
    h = x.reshape(b * s, d)
    cos, sa, sb = _rope_tables(s)
    col = jnp.arange(IN_COLS) // WIDTH
    col_scale = jnp.where((col == COL_QA) | (col == COL_QB), HEAD_DIM ** -0.5 * LOG2_E,
                          jnp.where((col == COL_GA) | (col == COL_GB), 0.5, 1.0)).astype(F32)
    a_splits = (COL_QA, COL_KA, COL_VA)
    for i in range(depth):
        lambda_init = 0.8 - 0.6 * math.exp(-0.3 * i)
        u, qkv4, qkv16, vt = _proj(h, attn_norm_gain[i], w_in, i, col_scale, cos, sa, sb, b, DIFF_QB)
        u3 = u.reshape(b, s, IN_COLS)
        g1 = _dswa_group(u3.reshape(b, 1, s, IN_COLS), a_splits)
        g4 = _dswa_group(qkv4, a_splits)
        g16 = _dswa_group(qkv16, a_splits)
        lam_p = jnp.stack([lambda_q1[i], lambda_k1[i], lambda_q2[i], lambda_k2[i]]).astype(F32)
        ob = _diff(u3, vt, lam_p, subln_gain[i], lambda_init, DIFF_QB)
        h = _post(h, u, g1, g4, g16, ob.reshape(b * s, WIDTH), p[i].reshape(b * s, -1),
                  w_out[i].astype(BF16), ple_norm_gain[i], (0.5 * w_ple_gate[i]).astype(BF16),
                  (0.5 * w_ple[i]).astype(BF16), final_norm_gain, final=i == depth - 1, batch=b)
    return h.reshape(b, s, d)
```

```python
import functools
import math

import jax
import jax.numpy as jnp
from jax import lax
from jax.experimental import pallas as pl
from jax.experimental.pallas import tpu as pltpu

HEAD_DIM = 64
LANES = 128
DSWA_W = 128
DSWA_UNROLL = 4
N_HEADS_DIFF = 4
DIFF_QB = 512
VT_ROWS = LANES + 16
WIDTH = 512
IN_COLS = 8 * WIDTH
ROPE_THETA = 500000.0
ROPE_DIM = HEAD_DIM // 4
RMS_EPS = 1e-6
SUBLN_EPS = 1e-5
NEG = -1e30
LOG2_E = math.log2(math.e)
VMEM_LIMIT = 48 * 1024 * 1024
PROJ_VMEM_LIMIT = 58 * 1024 * 1024

COL_QA, COL_KA, COL_VA, COL_GA, COL_QB, COL_KB, COL_VB, COL_GB = range(8)
ROPE_SPLITS = (COL_QA, COL_KA, COL_QB, COL_KB)

BF16 = jnp.bfloat16
F32 = jnp.float32


def _params(*sem):
    return pltpu.CompilerParams(dimension_semantics=sem, vmem_limit_bytes=VMEM_LIMIT)


def _proj_kernel(x_ref, g_ref, w32_ref, ws_ref, cos_ref, sa_ref, sb_ref, o_ref, d4_ref, d16_ref, vt_ref, buf_sc,
                 buf4_sc, w_ref, *, qb):
    tm = x_ref.shape[0]

    @pl.when(pl.program_id(0) == 0)
    def _():
        for c in range(IN_COLS // WIDTH):
            cols = slice(c * WIDTH, (c + 1) * WIDTH)
            w_ref[:, cols] = (w32_ref[0, :, cols] * ws_ref[:, cols]).astype(BF16)

    x = x_ref[...]
    ms = jnp.mean(x * x, axis=-1, keepdims=True)
    xn = (x * lax.rsqrt(ms + RMS_EPS) * g_ref[...]).astype(BF16)
    cos, sa, sb = cos_ref[...], sa_ref[...], sb_ref[...]
    for c in range(IN_COLS // WIDTH):
        cols = slice(c * WIDTH, (c + 1) * WIDTH)
        acc = jnp.dot(xn, w_ref[:, cols], preferred_element_type=F32)
        if c in ROPE_SPLITS:
            blocks = []
            for j in range(WIDTH // LANES):
                blk = acc[:, j * LANES:(j + 1) * LANES]
                blocks.append(blk * cos + pltpu.roll(blk, LANES - ROPE_DIM // 2, 1) * sa
                              + pltpu.roll(blk, ROPE_DIM // 2, 1) * sb)
            acc = jnp.concatenate(blocks, axis=1)
        o_ref[:, cols] = acc.astype(BF16)
        if c in (COL_QA, COL_KA, COL_VA):
            n4, n16 = tm // 4, tm // 16
            for j in range(WIDTH // LANES):
                lanes = slice(c * WIDTH + j * LANES, c * WIDTH + (j + 1) * LANES)
                buf_sc[j] = acc[:, j * LANES:(j + 1) * LANES]
                for r4 in range(4):
                    rows4 = buf_sc[j, pl.ds(r4, n4, stride=4), :]
                    d4_ref[0, r4, :, lanes] = rows4.astype(BF16)
                    buf4_sc[j, r4 * n4:(r4 + 1) * n4, :] = rows4
                for r4 in range(4):
                    for k in range(4):
                        rows16 = buf4_sc[j, pl.ds(r4 * n4 + k, n16, stride=4), :]
                        d16_ref[0, 4 * k + r4, :, lanes] = rows16.astype(BF16)
        if c == COL_VB:
            ones_row = (lax.broadcasted_iota(jnp.int32, (VT_ROWS - LANES, qb), 0) == 0).astype(BF16)
            for kb in range(tm // qb):
                for hd in range(N_HEADS_DIFF):
                    blk = acc[kb * qb:(kb + 1) * qb, hd * LANES:(hd + 1) * LANES]
                    vt_ref[0, hd, kb, 0:LANES, :] = blk.T.astype(BF16)
                    vt_ref[0, hd, kb, LANES:VT_ROWS, :] = ones_row


def _proj(h, gain, w_in, layer, col_scale, cos, sa, sb, batch, qb, tm=512):
    m, d = h.shape
    seq = m // batch
    nt = seq // tm
    row = lambda i: (i, 0)
    tab = lambda i: (i % nt, 0)
    res = lambda i: (i // nt, 0, i % nt, 0)
    return pl.pallas_call(
        functools.partial(_proj_kernel, qb=qb),
        grid=(m // tm,),
        in_specs=[pl.BlockSpec((tm, d), row),
                  pl.BlockSpec((1, d), lambda i: (0, 0)),
                  pl.BlockSpec((1, d, IN_COLS), lambda i: (layer, 0, 0), pipeline_mode=pl.Buffered(1)),
                  pl.BlockSpec((1, IN_COLS), lambda i: (0, 0)),
                  pl.BlockSpec((tm, LANES), tab),
                  pl.BlockSpec((tm, LANES), tab),
                  pl.BlockSpec((tm, LANES), tab)],
        out_specs=[pl.BlockSpec((tm, IN_COLS), row),
                   pl.BlockSpec((1, 4, tm // 4, 3 * WIDTH), res),
                   pl.BlockSpec((1, 16, tm // 16, 3 * WIDTH), res),
                   pl.BlockSpec((1, N_HEADS_DIFF, tm // qb, VT_ROWS, qb), lambda i: (i // nt, 0, i % nt, 0, 0))],
        out_shape=[jax.ShapeDtypeStruct((m, IN_COLS), BF16),
                   jax.ShapeDtypeStruct((batch, 4, seq // 4, 3 * WIDTH), BF16),
                   jax.ShapeDtypeStruct((batch, 16, seq // 16, 3 * WIDTH), BF16),
                   jax.ShapeDtypeStruct((batch, N_HEADS_DIFF, seq // qb, VT_ROWS, qb), BF16)],
        scratch_shapes=[pltpu.VMEM((WIDTH // LANES, tm, LANES), F32),
                        pltpu.VMEM((WIDTH // LANES, tm, LANES), F32),
                        pltpu.VMEM((d, IN_COLS), BF16)],
        compiler_params=pltpu.CompilerParams(dimension_semantics=("arbitrary",),
                                             vmem_limit_bytes=PROJ_VMEM_LIMIT),
        name="proj",
    )(h, gain.reshape(1, d), w_in, col_scale.reshape(1, IN_COLS), cos, sa, sb)


def _rope_tables(seq):
    half = ROPE_DIM // 2
    inv = jnp.power(ROPE_THETA, -jnp.arange(half, dtype=F32) * (2.0 / ROPE_DIM))
    ang = jnp.arange(seq).astype(F32)[:, None] * inv[None, :]
    cos, sin = jnp.cos(ang), jnp.sin(ang)
    rest = HEAD_DIM - ROPE_DIM
    zeros_h, zeros_r, ones_r = jnp.zeros((seq, half), F32), jnp.zeros((seq, rest), F32), jnp.ones((seq, rest), F32)
    heads = LANES // HEAD_DIM
    cos_t = jnp.concatenate([cos, cos, ones_r] * heads, axis=1)
    sa = jnp.concatenate([-sin, zeros_h, zeros_r] * heads, axis=1)
    sb = jnp.concatenate([zeros_h, sin, zeros_r] * heads, axis=1)
    return cos_t, sa, sb


def _dswa_kernel(q_ref, kc_ref, kp_ref, vc_ref, vp_ref, o_ref, l_ref, vt_sc, bias_sc, k_sc, sa_sc, sb_sc, *,
                 lc):
    w = DSWA_W
    nblk = lc // w
    first_chunk = pl.program_id(2) == 0
    head_a = lax.broadcasted_iota(jnp.int32, (1, LANES), 1) < HEAD_DIM
    kj = lax.broadcasted_iota(jnp.int32, (2 * w, 2 * w), 0)
    qi = lax.broadcasted_iota(jnp.int32, (2 * w, 2 * w), 1) & (w - 1)
    dist = w + qi - kj
    band = (dist >= 0) & (dist <= w)
    bias_sc[0] = jnp.where(band, 0.0, NEG)
    bias_sc[1] = jnp.where(band & ((kj >= w) | jnp.logical_not(first_chunk)), 0.0, NEG)

    def transposed(v):
        return v.astype(F32).T.astype(BF16)

    for hp in range(WIDTH // LANES):
        lanes = slice(hp * LANES, (hp + 1) * LANES)
        vt_sc[0, lanes, :] = transposed(vp_ref[0, 0, :, lanes])
        for n in range(nblk):
            vt_sc[n + 1, lanes, :] = transposed(vc_ref[0, 0, n * w:(n + 1) * w, lanes])

    k_sc[0:w, :] = kp_ref[0, 0]
    k_sc[w:w + lc, :] = kc_ref[0, 0]

    def scores(n, hp):
        lanes = slice(hp * LANES, (hp + 1) * LANES)
        row0 = pl.multiple_of(n * w, w)
        qb = q_ref[0, 0, pl.ds(row0, w), lanes]
        zero = jnp.zeros_like(qb)
        qs = jnp.concatenate([jnp.where(head_a, qb, zero), jnp.where(head_a, zero, qb)], axis=0)
        kk = k_sc[pl.ds(row0, 2 * w), lanes]
        bias = bias_sc[jnp.where(n == 0, 1, 0)]
        return lax.dot_general(kk, qs, (((1,), (1,)), ((), ())), preferred_element_type=F32) + bias

    def finish(n, hp, s):
        lanes = slice(hp * LANES, (hp + 1) * LANES)
        row0 = pl.multiple_of(n * w, w)
        m = jnp.max(s, axis=0, keepdims=True)
        p = jnp.exp2(s - m)
        den = jnp.sum(p, axis=0, keepdims=True)
        vv = jnp.concatenate([vt_sc[n, lanes, :], vt_sc[n + 1, lanes, :]], axis=1)
        o = jnp.dot(vv, p.astype(BF16), preferred_element_type=F32) * (1.0 / den)
        lse = m + jnp.log2(den)
        o_sel = jnp.concatenate([o[:HEAD_DIM, :w], o[HEAD_DIM:, w:]], axis=0)
        l_sel = jnp.concatenate([jnp.broadcast_to(lse[:, :w], (HEAD_DIM, w)),
                                 jnp.broadcast_to(lse[:, w:], (HEAD_DIM, w))], axis=0)
        o_ref[0, 0, pl.ds(row0, w), lanes] = o_sel.T.astype(o_ref.dtype)
        l_ref[0, 0, pl.ds(row0, w), lanes] = l_sel.T

    nhp = WIDTH // LANES

    def issue(n, s_ref):
        for hp in range(nhp):
            s_ref[hp] = scores(n, hp)

    def consume(n, s_ref):
        for hp in range(nhp):
            finish(n, hp, s_ref[hp])

    issue(0, sa_sc)

    def group(j, carry):
        n0 = DSWA_UNROLL * j
        for i in range(DSWA_UNROLL):
            cur, nxt = (sa_sc, sb_sc) if i % 2 == 0 else (sb_sc, sa_sc)
            issue(jnp.minimum(n0 + i + 1, nblk - 1), nxt)
            consume(n0 + i, cur)
        return carry

    lax.fori_loop(0, nblk // DSWA_UNROLL, group, 0)


def _dswa_group(qkv, splits):
    b, d, sub, _ = qkv.shape
    lc = min(sub, 1024)
    per = lc // DSWA_W
    assert per % DSWA_UNROLL == 0, "the kernel walks query blocks in groups of DSWA_UNROLL"
    sq, sk, sv = splits

    def cur(split):
        return pl.BlockSpec((1, 1, lc, WIDTH), lambda bi, r, c: (bi, r, c, split))

    def prv(split):
        return pl.BlockSpec((1, 1, DSWA_W, WIDTH), lambda bi, r, c: (bi, r, jnp.maximum(c * per - 1, 0), split))

    out = pl.BlockSpec((1, 1, lc, WIDTH), lambda bi, r, c: (bi, r, c, 0))
    return pl.pallas_call(
        functools.partial(_dswa_kernel, lc=lc),
        grid=(b, d, sub // lc),
        in_specs=[cur(sq), cur(sk), prv(sk), cur(sv), prv(sv)],
        out_specs=[out, out],
        out_shape=[jax.ShapeDtypeStruct((b, d, sub, WIDTH), BF16),
                   jax.ShapeDtypeStruct((b, d, sub, WIDTH), F32)],
        scratch_shapes=[pltpu.VMEM((per + 1, WIDTH, DSWA_W), BF16),
                        pltpu.VMEM((2, 2 * DSWA_W, 2 * DSWA_W), F32),
                        pltpu.VMEM((lc + DSWA_W, WIDTH), BF16),
                        pltpu.VMEM((WIDTH // LANES, 2 * DSWA_W, 2 * DSWA_W), F32),
                        pltpu.VMEM((WIDTH // LANES, 2 * DSWA_W, 2 * DSWA_W), F32)],
        compiler_params=_params("arbitrary", "arbitrary", "arbitrary"),
        name=f"dswa_d{d}",
    )(qkv, qkv, qkv, qkv, qkv)


def _diff_kernel(lam_ref, gain_ref, q_ref, k_ref, vt_ref, o_ref, m_sc, acc_sc, sa_sc, sb_sc, *, qb,
                 lambda_init):
    nh = q_ref.shape[2] // LANES
    heads = range(nh)
    head_a = lax.broadcasted_iota(jnp.int32, (1, LANES), 1) < HEAD_DIM
    key = lax.broadcasted_iota(jnp.int32, (qb, 2 * qb), 0)
    qry = lax.broadcasted_iota(jnp.int32, (qb, 2 * qb), 1) & (qb - 1)
    lam_p = lam_ref[...]
    lam = (jnp.exp(jnp.sum(lam_p[0:1] * lam_p[1:2], axis=1, keepdims=True))
           - jnp.exp(jnp.sum(lam_p[2:3] * lam_p[3:4], axis=1, keepdims=True)) + lambda_init)

    def query_block(qi, carry):
        q_rows = pl.ds(pl.multiple_of(qi * qb, qb), qb)

        def stacked_queries(h):
            q = q_ref[0, q_rows, h * LANES:(h + 1) * LANES]
            zero = jnp.zeros_like(q)
            return jnp.concatenate([jnp.where(head_a, q, zero), jnp.where(head_a, zero, q)], axis=0)

        qs = [stacked_queries(h) for h in heads]
        m_sc[...] = jnp.full(m_sc.shape, NEG, F32)
        acc_sc[...] = jnp.zeros(acc_sc.shape, F32)

        def issue(blk, s_ref):
            rows = pl.ds(pl.multiple_of(blk * qb, qb), qb)
            for h in heads:
                kb = k_ref[0, rows, h * LANES:(h + 1) * LANES]
                s_ref[h] = lax.dot_general(kb, qs[h], (((1,), (1,)), ((), ())), preferred_element_type=F32)

        def consume(blk, s_ref, mask):
            for h in heads:
                s = s_ref[h]
                if mask is not None:
                    s = jnp.where(mask, s, NEG)
                m_old = m_sc[h]
                m_new = jnp.maximum(m_old, jnp.max(s, axis=0, keepdims=True))
                alpha = jnp.exp2(m_old - m_new)
                p = jnp.exp2(s - m_new).astype(BF16)
                acc_sc[h] = alpha * acc_sc[h] + jnp.dot(vt_ref[0, h, blk], p, preferred_element_type=F32)
                m_sc[h] = m_new

        last = jnp.maximum(qi - 1, 0)
        issue(qi, sa_sc)
        issue(0, sb_sc)
        consume(qi, sa_sc, key <= qry)

        def pair(j, c):
            b0 = 2 * j
            issue(b0 + 1, sa_sc)
            consume(b0, sb_sc, None)
            issue(jnp.minimum(b0 + 2, last), sb_sc)
            consume(b0 + 1, sa_sc, None)
            return c

        lax.fori_loop(0, qi // 2, pair, 0)

        @pl.when(qi % 2 == 1)
        def _():
            consume(qi - 1, sb_sc, None)

        for h in heads:
            o_all = acc_sc[h, 0:LANES, :] / acc_sc[h, LANES:LANES + 1, :]
            o = o_all[:, :qb] - lam * o_all[:, qb:]
            o = o * lax.rsqrt(jnp.mean(o * o, axis=0, keepdims=True) + SUBLN_EPS) * gain_ref[...]
            o_ref[0, q_rows, h * LANES:(h + 1) * LANES] = (o * (1.0 - lambda_init)).T.astype(o_ref.dtype)
        return carry

    lax.fori_loop(0, q_ref.shape[1] // qb, query_block, 0)


def _diff(u3, vt, lam_p, subln_gain, lambda_init, qb, nh=2):
    b, s, _ = u3.shape
    per = WIDTH // (nh * LANES)
    nb = s // qb
    seq = lambda split: pl.BlockSpec((1, s, nh * LANES), lambda bi, g: (bi, 0, split * per + g))
    return pl.pallas_call(
        functools.partial(_diff_kernel, qb=qb, lambda_init=lambda_init),
        grid=(b, N_HEADS_DIFF // nh),
        in_specs=[pl.BlockSpec(lam_p.shape, lambda bi, g: (0, 0)),
                  pl.BlockSpec((LANES, 1), lambda bi, g: (0, 0)),
                  seq(COL_QB), seq(COL_KB),
                  pl.BlockSpec((1, nh, nb, VT_ROWS, qb), lambda bi, g: (bi, g, 0, 0, 0))],
        out_specs=pl.BlockSpec((1, s, nh * LANES), lambda bi, g: (bi, 0, g)),
        out_shape=jax.ShapeDtypeStruct((b, s, WIDTH), BF16),
        scratch_shapes=[pltpu.VMEM((nh, 1, 2 * qb), F32),
                        pltpu.VMEM((nh, VT_ROWS, 2 * qb), F32),
                        pltpu.VMEM((nh, qb, 2 * qb), F32), pltpu.VMEM((nh, qb, 2 * qb), F32)],
        compiler_params=_params("arbitrary", "arbitrary"),
        name="diff",
    )(lam_p, subln_gain.reshape(LANES, 1), u3, u3, vt)


def _silu_of_half(half):
    return half + half * jnp.tanh(half)


def _post_kernel(h_ref, ga_ref, gb_ref, o1_ref, l1_ref, o4_ref, l4_ref, o16_ref, l16_ref, ob_ref, p_ref,
                 wo32_ref, ng_ref, wg32_ref, wp32_ref, fg_ref, out_ref, on_sc, ln_sc, o4_sc, l4_sc,
                 wo_ref, wg_ref, wp_ref, *, final):
    tm = h_ref.shape[0]

    @pl.when(pl.program_id(0) == 0)
    def _():
        wo_ref[...] = wo32_ref[0].astype(BF16)
        wg_ref[...] = (0.5 * wg32_ref[0]).astype(BF16)
        wp_ref[...] = (0.5 * wp32_ref[0]).astype(BF16)

    n4, n16 = tm // 4, tm // 16
    merged = []
    for j in range(WIDTH // LANES):
        lanes = slice(j * LANES, (j + 1) * LANES)
        for r4 in range(4):
            for k in range(4):
                o4_sc[pl.ds(k, n16, stride=4), :] = o16_ref[0, 4 * k + r4, :, lanes].astype(F32)
                l4_sc[pl.ds(k, n16, stride=4), :] = l16_ref[0, 4 * k + r4, :, lanes]
            o16, l16 = o4_sc[...], l4_sc[...]
            o4, l4 = o4_ref[0, r4, :, lanes].astype(F32), l4_ref[0, r4, :, lanes]
            top = jnp.maximum(l4, l16)
            w4, w16 = jnp.exp2(l4 - top), jnp.exp2(l16 - top)
            den = w4 + w16
            on_sc[pl.ds(r4, n4, stride=4), :] = (w4 * o4 + w16 * o16) / den
            ln_sc[pl.ds(r4, n4, stride=4), :] = top + jnp.log2(den)
        o1, l1 = o1_ref[:, lanes].astype(F32), l1_ref[:, lanes]
        ox, lx = on_sc[...], ln_sc[...]
        top = jnp.maximum(l1, lx)
        w1, wx = jnp.exp2(l1 - top), jnp.exp2(lx - top)
        merged.append((w1 * o1 + wx * ox) / (w1 + wx))
    oa = jnp.concatenate(merged, axis=1)

    ya = (oa * _silu_of_half(ga_ref[...].astype(F32))).astype(BF16)
    yb = (ob_ref[...].astype(F32) * _silu_of_half(gb_ref[...].astype(F32))).astype(BF16)
    y = (jnp.dot(ya, wo_ref[0:WIDTH, :], preferred_element_type=F32)
         + jnp.dot(yb, wo_ref[WIDTH:2 * WIDTH, :], preferred_element_type=F32))
    h1 = h_ref[...] + y
    n = h1 * lax.rsqrt(jnp.mean(h1 * h1, axis=-1, keepdims=True) + RMS_EPS) * ng_ref[...]
    t = jnp.tanh(jnp.dot(n.astype(BF16), wg_ref[...], preferred_element_type=F32))
    ple_half = jnp.dot(p_ref[...].astype(BF16), wp_ref[...], preferred_element_type=F32)
    h2 = h1 + ple_half + ple_half * t
    if final:
        h2 = h2 * lax.rsqrt(jnp.mean(h2 * h2, axis=-1, keepdims=True) + RMS_EPS) * fg_ref[...]
    out_ref[...] = h2


def _post(h, u, g1, g4, g16, ob, p_i, w_out, ple_gain, w_gate, w_ple, layer, final_gain, final, batch, tm=512):
    m, d = h.shape
    whole = lambda w: pl.BlockSpec((1,) + w.shape[1:], lambda i: (layer, 0, 0), pipeline_mode=pl.Buffered(1))
    nt = m // batch // tm
    row = lambda i: (i, 0)
    const = lambda i: (0, 0)
    res = lambda i: (i // nt, 0, i % nt, 0)
    grp = lambda dil: pl.BlockSpec((1, dil, tm // dil, WIDTH), res)
    return pl.pallas_call(
        functools.partial(_post_kernel, final=final),
        grid=(m // tm,),
        in_specs=[pl.BlockSpec((tm, d), row),
                  pl.BlockSpec((tm, WIDTH), lambda i: (i, COL_GA)),
                  pl.BlockSpec((tm, WIDTH), lambda i: (i, COL_GB)),
                  pl.BlockSpec((tm, WIDTH), row), pl.BlockSpec((tm, WIDTH), row),
                  grp(4), grp(4), grp(16), grp(16),
                  pl.BlockSpec((tm, WIDTH), row),
                  pl.BlockSpec((tm, p_i.shape[1]), row),
                  whole(w_out),
                  pl.BlockSpec((1, d), const),
                  whole(w_gate),
                  whole(w_ple),
                  pl.BlockSpec((1, d), const)],
        out_specs=pl.BlockSpec((tm, d), row),
        out_shape=jax.ShapeDtypeStruct((m, d), F32),
        scratch_shapes=[pltpu.VMEM((tm, LANES), F32), pltpu.VMEM((tm, LANES), F32),
                        pltpu.VMEM((tm // 4, LANES), F32), pltpu.VMEM((tm // 4, LANES), F32),
                        pltpu.VMEM(w_out.shape[1:], BF16), pltpu.VMEM(w_gate.shape[1:], BF16),
                        pltpu.VMEM(w_ple.shape[1:], BF16)],
        compiler_params=_params("arbitrary"),
        name="post",
    )(h, u, u, g1[0].reshape(m, WIDTH), g1[1].reshape(m, WIDTH), g4[0], g4[1], g16[0], g16[1], ob, p_i,
      w_out, ple_gain.reshape(1, d), w_gate, w_ple, final_gain.reshape(1, d))


def kernel(x, p, attn_norm_gain, w_in, w_out, lambda_q1, lambda_k1, lambda_q2, lambda_k2, subln_gain,
           ple_norm_gain, w_ple_gate, w_ple, final_norm_gain):
    b, s, d = x.shape
    depth = w_in.shape[0]
    h = x.reshape(b * s, d)
    cos, sa, sb = _rope_tables(s)
    col = jnp.arange(IN_COLS) // WIDTH
    col_scale = jnp.where((col == COL_QA) | (col == COL_QB), HEAD_DIM ** -0.5 * LOG2_E,
                          jnp.where((col == COL_GA) | (col == COL_GB), 0.5, 1.0)).astype(F32)
    a_splits = (COL_QA, COL_KA, COL_VA)
    for i in range(depth):
        lambda_init = 0.8 - 0.6 * math.exp(-0.3 * i)
        u, qkv4, qkv16, vt = _proj(h, attn_norm_gain[i], w_in, i, col_scale, cos, sa, sb, b, DIFF_QB)
        u3 = u.reshape(b, s, IN_COLS)
        g1 = _dswa_group(u3.reshape(b, 1, s, IN_COLS), a_splits)
        g4 = _dswa_group(qkv4, a_splits)
        g16 = _dswa_group(qkv16, a_splits)
        lam_p = jnp.stack([lambda_q1[i], lambda_k1[i], lambda_q2[i], lambda_k2[i]]).astype(F32)
        ob = _diff(u3, vt, lam_p, subln_gain[i], lambda_init, DIFF_QB)
        h = _post(h, u, g1, g4, g16, ob.reshape(b * s, WIDTH), p[i].reshape(b * s, -1),
                  w_out, ple_norm_gain[i], w_ple_gate, w_ple, i, final_norm_gain, final=i == depth - 1, batch=b)
    return h.reshape(b, s, d)
```

```python
import functools
import math

import jax
import jax.numpy as jnp
from jax import lax
from jax.experimental import pallas as pl
from jax.experimental.pallas import tpu as pltpu

HEAD_DIM = 64
LANES = 128
DSWA_W = 128
DSWA_UNROLL = 4
N_HEADS_DIFF = 4
DIFF_QB = 512
VT_ROWS = LANES + 16
WIDTH = 512
IN_COLS = 8 * WIDTH
ROPE_THETA = 500000.0
ROPE_DIM = HEAD_DIM // 4
RMS_EPS = 1e-6
SUBLN_EPS = 1e-5
NEG = -1e30
LOG2_E = math.log2(math.e)
VMEM_LIMIT = 48 * 1024 * 1024
PROJ_VMEM_LIMIT = 58 * 1024 * 1024

COL_QA, COL_KA, COL_VA, COL_GA, COL_QB, COL_KB, COL_VB, COL_GB = range(8)
ROPE_SPLITS = (COL_QA, COL_KA, COL_QB, COL_KB)

BF16 = jnp.bfloat16
F32 = jnp.float32


def _params(*sem):
    return pltpu.CompilerParams(dimension_semantics=sem, vmem_limit_bytes=VMEM_LIMIT)


def _proj_kernel(x_ref, g_ref, w32_ref, ws_ref, cos_ref, sa_ref, sb_ref, o_ref, d4_ref, d16_ref, vt_ref, buf_sc,
                 buf4_sc, w_ref, *, qb):
    tm = x_ref.shape[0]

    @pl.when(pl.program_id(0) == 0)
    def _():
        for c in range(IN_COLS // WIDTH):
            cols = slice(c * WIDTH, (c + 1) * WIDTH)
            w_ref[:, cols] = (w32_ref[0, :, cols] * ws_ref[:, cols]).astype(BF16)

    x = x_ref[...]
    ms = jnp.mean(x * x, axis=-1, keepdims=True)
    xn = (x * lax.rsqrt(ms + RMS_EPS) * g_ref[...]).astype(BF16)
    cos, sa, sb = cos_ref[...], sa_ref[...], sb_ref[...]
    for c in range(IN_COLS // WIDTH):
        cols = slice(c * WIDTH, (c + 1) * WIDTH)
        acc = jnp.dot(xn, w_ref[:, cols], preferred_element_type=F32)
        if c in ROPE_SPLITS:
            blocks = []
            for j in range(WIDTH // LANES):
                blk = acc[:, j * LANES:(j + 1) * LANES]
                blocks.append(blk * cos + pltpu.roll(blk, LANES - ROPE_DIM // 2, 1) * sa
                              + pltpu.roll(blk, ROPE_DIM // 2, 1) * sb)
            acc = jnp.concatenate(blocks, axis=1)
        o_ref[:, cols] = acc.astype(BF16)
        if c in (COL_QA, COL_KA, COL_VA):
            n4, n16 = tm // 4, tm // 16
            for j in range(WIDTH // LANES):
                lanes = slice(c * WIDTH + j * LANES, c * WIDTH + (j + 1) * LANES)
                buf_sc[j] = acc[:, j * LANES:(j + 1) * LANES]
                for r4 in range(4):
                    rows4 = buf_sc[j, pl.ds(r4, n4, stride=4), :]
                    d4_ref[0, r4, :, lanes] = rows4.astype(BF16)
                    buf4_sc[j, r4 * n4:(r4 + 1) * n4, :] = rows4
                for r4 in range(4):
                    for k in range(4):
                        rows16 = buf4_sc[j, pl.ds(r4 * n4 + k, n16, stride=4), :]
                        d16_ref[0, 4 * k + r4, :, lanes] = rows16.astype(BF16)
        if c == COL_VB:
            ones_row = (lax.broadcasted_iota(jnp.int32, (VT_ROWS - LANES, qb), 0) == 0).astype(BF16)
            for kb in range(tm // qb):
                for hd in range(N_HEADS_DIFF):
                    blk = acc[kb * qb:(kb + 1) * qb, hd * LANES:(hd + 1) * LANES]
                    vt_ref[0, hd, kb, 0:LANES, :] = blk.T.astype(BF16)
                    vt_ref[0, hd, kb, LANES:VT_ROWS, :] = ones_row


def _proj(h, gain, w_in, layer, col_scale, cos, sa, sb, batch, qb, tm=512):
    m, d = h.shape
    seq = m // batch
    nt = seq // tm
    row = lambda i: (i, 0)
    tab = lambda i: (i % nt, 0)
    res = lambda i: (i // nt, 0, i % nt, 0)
    return pl.pallas_call(
        functools.partial(_proj_kernel, qb=qb),
        grid=(m // tm,),
        in_specs=[pl.BlockSpec((tm, d), row),
                  pl.BlockSpec((1, d), lambda i: (0, 0)),
                  pl.BlockSpec((1, d, IN_COLS), lambda i: (layer, 0, 0), pipeline_mode=pl.Buffered(1)),
                  pl.BlockSpec((1, IN_COLS), lambda i: (0, 0)),
                  pl.BlockSpec((tm, LANES), tab),
                  pl.BlockSpec((tm, LANES), tab),
                  pl.BlockSpec((tm, LANES), tab)],
        out_specs=[pl.BlockSpec((tm, IN_COLS), row),
                   pl.BlockSpec((1, 4, tm // 4, 3 * WIDTH), res),
                   pl.BlockSpec((1, 16, tm // 16, 3 * WIDTH), res),
                   pl.BlockSpec((1, N_HEADS_DIFF, tm // qb, VT_ROWS, qb), lambda i: (i // nt, 0, i % nt, 0, 0))],
        out_shape=[jax.ShapeDtypeStruct((m, IN_COLS), BF16),
                   jax.ShapeDtypeStruct((batch, 4, seq // 4, 3 * WIDTH), BF16),
                   jax.ShapeDtypeStruct((batch, 16, seq // 16, 3 * WIDTH), BF16),
                   jax.ShapeDtypeStruct((batch, N_HEADS_DIFF, seq // qb, VT_ROWS, qb), BF16)],
        scratch_shapes=[pltpu.VMEM((WIDTH // LANES, tm, LANES), F32),
                        pltpu.VMEM((WIDTH // LANES, tm, LANES), F32),
                        pltpu.VMEM((d, IN_COLS), BF16)],
        compiler_params=pltpu.CompilerParams(dimension_semantics=("arbitrary",),
                                             vmem_limit_bytes=PROJ_VMEM_LIMIT),
        name="proj",
    )(h, gain.reshape(1, d), w_in, col_scale.reshape(1, IN_COLS), cos, sa, sb)


def _rope_tables(seq):
    half = ROPE_DIM // 2
    inv = jnp.power(ROPE_THETA, -jnp.arange(half, dtype=F32) * (2.0 / ROPE_DIM))
    ang = jnp.arange(seq).astype(F32)[:, None] * inv[None, :]
    cos, sin = jnp.cos(ang), jnp.sin(ang)
    ch = jnp.arange(LANES) % HEAD_DIM
    cos_l = jnp.take(cos, ch % half, axis=1)
    sin_l = jnp.take(sin, ch % half, axis=1)
    cos_t = jnp.where(ch < ROPE_DIM, cos_l, 1.0)
    sa = jnp.where(ch < half, -sin_l, 0.0)
    sb = jnp.where((ch >= half) & (ch < ROPE_DIM), sin_l, 0.0)
    return cos_t, sa, sb


def _dswa_kernel(q_ref, kc_ref, kp_ref, vc_ref, vp_ref, o_ref, l_ref, vt_sc, bias_sc, k_sc, sa_sc, sb_sc, *,
                 lc):
    w = DSWA_W
    nblk = lc // w
    first_chunk = pl.program_id(2) == 0
    head_a = lax.broadcasted_iota(jnp.int32, (1, LANES), 1) < HEAD_DIM
    kj = lax.broadcasted_iota(jnp.int32, (2 * w, 2 * w), 0)
    qi = lax.broadcasted_iota(jnp.int32, (2 * w, 2 * w), 1) & (w - 1)
    dist = w + qi - kj
    band = (dist >= 0) & (dist <= w)
    bias_sc[0] = jnp.where(band, 0.0, NEG)
    bias_sc[1] = jnp.where(band & ((kj >= w) | jnp.logical_not(first_chunk)), 0.0, NEG)

    def transposed(v):
        return v.astype(F32).T.astype(BF16)

    for hp in range(WIDTH // LANES):
        lanes = slice(hp * LANES, (hp + 1) * LANES)
        vt_sc[0, lanes, :] = transposed(vp_ref[0, 0, :, lanes])
        for n in range(nblk):
            vt_sc[n + 1, lanes, :] = transposed(vc_ref[0, 0, n * w:(n + 1) * w, lanes])

    k_sc[0:w, :] = kp_ref[0, 0]
    k_sc[w:w + lc, :] = kc_ref[0, 0]

    def scores(n, hp):
        lanes = slice(hp * LANES, (hp + 1) * LANES)
        row0 = pl.multiple_of(n * w, w)
        qb = q_ref[0, 0, pl.ds(row0, w), lanes]
        zero = jnp.zeros_like(qb)
        qs = jnp.concatenate([jnp.where(head_a, qb, zero), jnp.where(head_a, zero, qb)], axis=0)
        kk = k_sc[pl.ds(row0, 2 * w), lanes]
        bias = bias_sc[jnp.where(n == 0, 1, 0)]
        return lax.dot_general(kk, qs, (((1,), (1,)), ((), ())), preferred_element_type=F32) + bias

    def finish(n, hp, s):
        lanes = slice(hp * LANES, (hp + 1) * LANES)
        row0 = pl.multiple_of(n * w, w)
        m = jnp.max(s, axis=0, keepdims=True)
        p = jnp.exp2(s - m)
        den = jnp.sum(p, axis=0, keepdims=True)
        vv = jnp.concatenate([vt_sc[n, lanes, :], vt_sc[n + 1, lanes, :]], axis=1)
        o = jnp.dot(vv, p.astype(BF16), preferred_element_type=F32) * (1.0 / den)
        lse = m + jnp.log2(den)
        o_sel = jnp.concatenate([o[:HEAD_DIM, :w], o[HEAD_DIM:, w:]], axis=0)
        l_sel = jnp.concatenate([jnp.broadcast_to(lse[:, :w], (HEAD_DIM, w)),
                                 jnp.broadcast_to(lse[:, w:], (HEAD_DIM, w))], axis=0)
        o_ref[0, 0, pl.ds(row0, w), lanes] = o_sel.T.astype(o_ref.dtype)
        l_ref[0, 0, pl.ds(row0, w), lanes] = l_sel.T

    nhp = WIDTH // LANES

    def issue(n, s_ref):
        for hp in range(nhp):
            s_ref[hp] = scores(n, hp)

    def consume(n, s_ref):
        for hp in range(nhp):
            finish(n, hp, s_ref[hp])

    issue(0, sa_sc)

    def group(j, carry):
        n0 = DSWA_UNROLL * j
        for i in range(DSWA_UNROLL):
            cur, nxt = (sa_sc, sb_sc) if i % 2 == 0 else (sb_sc, sa_sc)
            issue(jnp.minimum(n0 + i + 1, nblk - 1), nxt)
            consume(n0 + i, cur)
        return carry

    lax.fori_loop(0, nblk // DSWA_UNROLL, group, 0)


def _dswa_group(qkv, splits):
    b, d, sub, _ = qkv.shape
    lc = min(sub, 1024)
    per = lc // DSWA_W
    assert per % DSWA_UNROLL == 0, "the kernel walks query blocks in groups of DSWA_UNROLL"
    sq, sk, sv = splits

    def cur(split):
        return pl.BlockSpec((1, 1, lc, WIDTH), lambda bi, r, c: (bi, r, c, split))

    def prv(split):
        return pl.BlockSpec((1, 1, DSWA_W, WIDTH), lambda bi, r, c: (bi, r, jnp.maximum(c * per - 1, 0), split))

    out = pl.BlockSpec((1, 1, lc, WIDTH), lambda bi, r, c: (bi, r, c, 0))
    return pl.pallas_call(
        functools.partial(_dswa_kernel, lc=lc),
        grid=(b, d, sub // lc),
        in_specs=[cur(sq), cur(sk), prv(sk), cur(sv), prv(sv)],
        out_specs=[out, out],
        out_shape=[jax.ShapeDtypeStruct((b, d, sub, WIDTH), BF16),
                   jax.ShapeDtypeStruct((b, d, sub, WIDTH), F32)],
        scratch_shapes=[pltpu.VMEM((per + 1, WIDTH, DSWA_W), BF16),
                        pltpu.VMEM((2, 2 * DSWA_W, 2 * DSWA_W), F32),
                        pltpu.VMEM((lc + DSWA_W, WIDTH), BF16),
                        pltpu.VMEM((WIDTH // LANES, 2 * DSWA_W, 2 * DSWA_W), F32),
                        pltpu.VMEM((WIDTH // LANES, 2 * DSWA_W, 2 * DSWA_W), F32)],
        compiler_params=_params("arbitrary", "arbitrary", "arbitrary"),
        name=f"dswa_d{d}",
    )(qkv, qkv, qkv, qkv, qkv)


def _diff_kernel(lam_ref, gain_ref, q_ref, k_ref, vt_ref, o_ref, m_sc, acc_sc, sa_sc, sb_sc, *, qb,
                 lambda_init):
    nh = q_ref.shape[2] // LANES
    heads = range(nh)
    head_a = lax.broadcasted_iota(jnp.int32, (1, LANES), 1) < HEAD_DIM
    key = lax.broadcasted_iota(jnp.int32, (qb, 2 * qb), 0)
    qry = lax.broadcasted_iota(jnp.int32, (qb, 2 * qb), 1) & (qb - 1)
    lam_p = lam_ref[...]
    lam = (jnp.exp(jnp.sum(lam_p[0:1] * lam_p[1:2], axis=1, keepdims=True))
           - jnp.exp(jnp.sum(lam_p[2:3] * lam_p[3:4], axis=1, keepdims=True)) + lambda_init)

    def query_block(qi, carry):
        q_rows = pl.ds(pl.multiple_of(qi * qb, qb), qb)

        def stacked_queries(h):
            q = q_ref[0, q_rows, h * LANES:(h + 1) * LANES]
            zero = jnp.zeros_like(q)
            return jnp.concatenate([jnp.where(head_a, q, zero), jnp.where(head_a, zero, q)], axis=0)

        qs = [stacked_queries(h) for h in heads]
        m_sc[...] = jnp.full(m_sc.shape, NEG, F32)
        acc_sc[...] = jnp.zeros(acc_sc.shape, F32)

        def issue(blk, s_ref):
            rows = pl.ds(pl.multiple_of(blk * qb, qb), qb)
            for h in heads:
                kb = k_ref[0, rows, h * LANES:(h + 1) * LANES]
                s_ref[h] = lax.dot_general(kb, qs[h], (((1,), (1,)), ((), ())), preferred_element_type=F32)

        def consume(blk, s_ref, mask):
            for h in heads:
                s = s_ref[h]
                if mask is not None:
                    s = jnp.where(mask, s, NEG)
                m_old = m_sc[h]
                m_new = jnp.maximum(m_old, jnp.max(s, axis=0, keepdims=True))
                alpha = jnp.exp2(m_old - m_new)
                p = jnp.exp2(s - m_new).astype(BF16)
                acc_sc[h] = alpha * acc_sc[h] + jnp.dot(vt_ref[0, h, blk], p, preferred_element_type=F32)
                m_sc[h] = m_new

        last = jnp.maximum(qi - 1, 0)
        issue(qi, sa_sc)
        issue(0, sb_sc)
        consume(qi, sa_sc, key <= qry)

        def pair(j, c):
            b0 = 2 * j
            issue(b0 + 1, sa_sc)
            consume(b0, sb_sc, None)
            issue(jnp.minimum(b0 + 2, last), sb_sc)
            consume(b0 + 1, sa_sc, None)
            return c

        lax.fori_loop(0, qi // 2, pair, 0)

        @pl.when(qi % 2 == 1)
        def _():
            consume(qi - 1, sb_sc, None)

        for h in heads:
            o_all = acc_sc[h, 0:LANES, :] / acc_sc[h, LANES:LANES + 1, :]
            o = o_all[:, :qb] - lam * o_all[:, qb:]
            o = o * lax.rsqrt(jnp.mean(o * o, axis=0, keepdims=True) + SUBLN_EPS) * gain_ref[...]
            o_ref[0, q_rows, h * LANES:(h + 1) * LANES] = (o * (1.0 - lambda_init)).T.astype(o_ref.dtype)
        return carry

    lax.fori_loop(0, q_ref.shape[1] // qb, query_block, 0)


def _diff(u3, vt, lam_p, subln_gain, lambda_init, qb, nh=2):
    b, s, _ = u3.shape
    per = WIDTH // (nh * LANES)
    nb = s // qb
    seq = lambda split: pl.BlockSpec((1, s, nh * LANES), lambda bi, g: (bi, 0, split * per + g))
    return pl.pallas_call(
        functools.partial(_diff_kernel, qb=qb, lambda_init=lambda_init),
        grid=(b, N_HEADS_DIFF // nh),
        in_specs=[pl.BlockSpec(lam_p.shape, lambda bi, g: (0, 0)),
                  pl.BlockSpec((LANES, 1), lambda bi, g: (0, 0)),
                  seq(COL_QB), seq(COL_KB),
                  pl.BlockSpec((1, nh, nb, VT_ROWS, qb), lambda bi, g: (bi, g, 0, 0, 0))],
        out_specs=pl.BlockSpec((1, s, nh * LANES), lambda bi, g: (bi, 0, g)),
        out_shape=jax.ShapeDtypeStruct((b, s, WIDTH), BF16),
        scratch_shapes=[pltpu.VMEM((nh, 1, 2 * qb), F32),
                        pltpu.VMEM((nh, VT_ROWS, 2 * qb), F32),
                        pltpu.VMEM((nh, qb, 2 * qb), F32), pltpu.VMEM((nh, qb, 2 * qb), F32)],
        compiler_params=_params("arbitrary", "arbitrary"),
        name="diff",
    )(lam_p, subln_gain.reshape(LANES, 1), u3, u3, vt)


def _silu_of_half(half):
    return half + half * jnp.tanh(half)


def _post_kernel(h_ref, ga_ref, gb_ref, o1_ref, l1_ref, o4_ref, l4_ref, o16_ref, l16_ref, ob_ref, p_ref,
                 wo32_ref, ng_ref, wg32_ref, wp32_ref, fg_ref, out_ref, on_sc, ln_sc, o4_sc, l4_sc,
                 wo_ref, wg_ref, wp_ref, *, final):
    tm = h_ref.shape[0]

    @pl.when(pl.program_id(0) == 0)
    def _():
        wo_ref[...] = wo32_ref[0].astype(BF16)
        wg_ref[...] = (0.5 * wg32_ref[0]).astype(BF16)
        wp_ref[...] = (0.5 * wp32_ref[0]).astype(BF16)

    n4, n16 = tm // 4, tm // 16
    merged = []
    for j in range(WIDTH // LANES):
        lanes = slice(j * LANES, (j + 1) * LANES)
        for r4 in range(4):
            for k in range(4):
                o4_sc[pl.ds(k, n16, stride=4), :] = o16_ref[0, 4 * k + r4, :, lanes].astype(F32)
                l4_sc[pl.ds(k, n16, stride=4), :] = l16_ref[0, 4 * k + r4, :, lanes]
            o16, l16 = o4_sc[...], l4_sc[...]
            o4, l4 = o4_ref[0, r4, :, lanes].astype(F32), l4_ref[0, r4, :, lanes]
            top = jnp.maximum(l4, l16)
            w4, w16 = jnp.exp2(l4 - top), jnp.exp2(l16 - top)
            den = w4 + w16
            on_sc[pl.ds(r4, n4, stride=4), :] = (w4 * o4 + w16 * o16) / den
            ln_sc[pl.ds(r4, n4, stride=4), :] = top + jnp.log2(den)
        o1, l1 = o1_ref[:, lanes].astype(F32), l1_ref[:, lanes]
        ox, lx = on_sc[...], ln_sc[...]
        top = jnp.maximum(l1, lx)
        w1, wx = jnp.exp2(l1 - top), jnp.exp2(lx - top)
        merged.append((w1 * o1 + wx * ox) / (w1 + wx))
    oa = jnp.concatenate(merged, axis=1)

    ya = (oa * _silu_of_half(ga_ref[...].astype(F32))).astype(BF16)
    yb = (ob_ref[...].astype(F32) * _silu_of_half(gb_ref[...].astype(F32))).astype(BF16)
    y = (jnp.dot(ya, wo_ref[0:WIDTH, :], preferred_element_type=F32)
         + jnp.dot(yb, wo_ref[WIDTH:2 * WIDTH, :], preferred_element_type=F32))
    h1 = h_ref[...] + y
    n = h1 * lax.rsqrt(jnp.mean(h1 * h1, axis=-1, keepdims=True) + RMS_EPS) * ng_ref[...]
    t = jnp.tanh(jnp.dot(n.astype(BF16), wg_ref[...], preferred_element_type=F32))
    ple_half = jnp.dot(p_ref[0].astype(BF16), wp_ref[...], preferred_element_type=F32)
    h2 = h1 + ple_half + ple_half * t
    if final:
        h2 = h2 * lax.rsqrt(jnp.mean(h2 * h2, axis=-1, keepdims=True) + RMS_EPS) * fg_ref[...]
    out_ref[...] = h2


def _post(h, u, g1, g4, g16, ob, p, w_out, ple_gain, w_gate, w_ple, layer, final_gain, final, batch, tm=512):
    m, d = h.shape
    whole = lambda w: pl.BlockSpec((1,) + w.shape[1:], lambda i: (layer, 0, 0), pipeline_mode=pl.Buffered(1))
    nt = m // batch // tm
    row = lambda i: (i, 0)
    const = lambda i: (0, 0)
    res = lambda i: (i // nt, 0, i % nt, 0)
    grp = lambda dil: pl.BlockSpec((1, dil, tm // dil, WIDTH), res)
    return pl.pallas_call(
        functools.partial(_post_kernel, final=final),
        grid=(m // tm,),
        in_specs=[pl.BlockSpec((tm, d), row),
                  pl.BlockSpec((tm, WIDTH), lambda i: (i, COL_GA)),
                  pl.BlockSpec((tm, WIDTH), lambda i: (i, COL_GB)),
                  pl.BlockSpec((tm, WIDTH), row), pl.BlockSpec((tm, WIDTH), row),
                  grp(4), grp(4), grp(16), grp(16),
                  pl.BlockSpec((tm, WIDTH), row),
                  pl.BlockSpec((1, tm, p.shape[2]), lambda i: (layer, i, 0)),
                  whole(w_out),
                  pl.BlockSpec((1, d), const),
                  whole(w_gate),
                  whole(w_ple),
                  pl.BlockSpec((1, d), const)],
        out_specs=pl.BlockSpec((tm, d), row),
        out_shape=jax.ShapeDtypeStruct((m, d), F32),
        scratch_shapes=[pltpu.VMEM((tm, LANES), F32), pltpu.VMEM((tm, LANES), F32),
                        pltpu.VMEM((tm // 4, LANES), F32), pltpu.VMEM((tm // 4, LANES), F32),
                        pltpu.VMEM(w_out.shape[1:], BF16), pltpu.VMEM(w_gate.shape[1:], BF16),
                        pltpu.VMEM(w_ple.shape[1:], BF16)],
        compiler_params=_params("arbitrary"),
        name="post",
    )(h, u, u, g1[0].reshape(m, WIDTH), g1[1].reshape(m, WIDTH), g4[0], g4[1], g16[0], g16[1], ob, p,
      w_out, ple_gain.reshape(1, d), w_gate, w_ple, final_gain.reshape(1, d))


def kernel(x, p, attn_norm_gain, w_in, w_out, lambda_q1, lambda_k1, lambda_q2, lambda_k2, subln_gain,
           ple_norm_gain, w_ple_gate, w_ple, final_norm_gain):
    b, s, d = x.shape
    depth = w_in.shape[0]
    h = x.reshape(b * s, d)
    cos, sa, sb = _rope_tables(s)
    col = jnp.arange(IN_COLS) // WIDTH
    col_scale = jnp.where((col == COL_QA) | (col == COL_QB), HEAD_DIM ** -0.5 * LOG2_E,
                          jnp.where((col == COL_GA) | (col == COL_GB), 0.5, 1.0)).astype(F32)
    a_splits = (COL_QA, COL_KA, COL_VA)
    for i in range(depth):
        lambda_init = 0.8 - 0.6 * math.exp(-0.3 * i)
        u, qkv4, qkv16, vt = _proj(h, attn_norm_gain[i], w_in, i, col_scale, cos, sa, sb, b, DIFF_QB)
        u3 = u.reshape(b, s, IN_COLS)
        g1 = _dswa_group(u3.reshape(b, 1, s, IN_COLS), a_splits)
        g4 = _dswa_group(qkv4, a_splits)
        g16 = _dswa_group(qkv16, a_splits)
        lam_p = jnp.stack([lambda_q1[i], lambda_k1[i], lambda_q2[i], lambda_k2[i]]).astype(F32)
        ob = _diff(u3, vt, lam_p, subln_gain[i], lambda_init, DIFF_QB)
        h = _post(h, u, g1, g4, g16, ob.reshape(b * s, WIDTH), p.reshape(depth, b * s, -1),
                  w_out, ple_norm_gain[i], w_ple_gate, w_ple, i, final_norm_gain, final=i == depth - 1, batch=b)
    return h.reshape(b, s, d)
```

```python
import functools
import math

import jax
import jax.numpy as jnp
from jax import lax
from jax.experimental import pallas as pl
from jax.experimental.pallas import tpu as pltpu

HEAD_DIM = 64
LANES = 128
DSWA_W = 128
DSWA_UNROLL = 4
N_HEADS_DIFF = 4
DIFF_QB = 512
VT_ROWS = LANES + 16
WIDTH = 512
IN_COLS = 8 * WIDTH
ROPE_THETA = 500000.0
ROPE_DIM = HEAD_DIM // 4
RMS_EPS = 1e-6
SUBLN_EPS = 1e-5
NEG = -1e30
LOG2_E = math.log2(math.e)
VMEM_LIMIT = 48 * 1024 * 1024
PROJ_VMEM_LIMIT = 58 * 1024 * 1024

COL_QA, COL_KA, COL_VA, COL_GA, COL_QB, COL_KB, COL_VB, COL_GB = range(8)
ROPE_SPLITS = (COL_QA, COL_KA, COL_QB, COL_KB)

BF16 = jnp.bfloat16
F32 = jnp.float32


def _params(*sem):
    return pltpu.CompilerParams(dimension_semantics=sem, vmem_limit_bytes=VMEM_LIMIT)


def _proj_kernel(x_ref, g_ref, w32_ref, ws_ref, cos_ref, sa_ref, sb_ref, o_ref, d4_ref, d16_ref, vt_ref, buf_sc,
                 buf4_sc, w_ref, *, qb):
    tm = x_ref.shape[0]

    @pl.when(pl.program_id(0) == 0)
    def _():
        for c in range(IN_COLS // WIDTH):
            cols = slice(c * WIDTH, (c + 1) * WIDTH)
            w_ref[:, cols] = (w32_ref[0, :, cols] * ws_ref[:, cols]).astype(BF16)

    x = x_ref[...]
    ms = jnp.mean(x * x, axis=-1, keepdims=True)
    xn = (x * lax.rsqrt(ms + RMS_EPS) * g_ref[...]).astype(BF16)
    cos, sa, sb = cos_ref[...], sa_ref[...], sb_ref[...]
    for c in range(IN_COLS // WIDTH):
        cols = slice(c * WIDTH, (c + 1) * WIDTH)
        acc = jnp.dot(xn, w_ref[:, cols], preferred_element_type=F32)
        if c in ROPE_SPLITS:
            blocks = []
            for j in range(WIDTH // LANES):
                blk = acc[:, j * LANES:(j + 1) * LANES]
                blocks.append(blk * cos + pltpu.roll(blk, LANES - ROPE_DIM // 2, 1) * sa
                              + pltpu.roll(blk, ROPE_DIM // 2, 1) * sb)
            acc = jnp.concatenate(blocks, axis=1)
        o_ref[:, cols] = acc.astype(BF16)
        if c in (COL_QA, COL_KA, COL_VA):
            n4, n16 = tm // 4, tm // 16
            for j in range(WIDTH // LANES):
                lanes = slice(c * WIDTH + j * LANES, c * WIDTH + (j + 1) * LANES)
                buf_sc[j] = acc[:, j * LANES:(j + 1) * LANES]
                for r4 in range(4):
                    rows4 = buf_sc[j, pl.ds(r4, n4, stride=4), :]
                    d4_ref[0, r4, :, lanes] = rows4.astype(BF16)
                    buf4_sc[j, r4 * n4:(r4 + 1) * n4, :] = rows4
                for r4 in range(4):
                    for k in range(4):
                        rows16 = buf4_sc[j, pl.ds(r4 * n4 + k, n16, stride=4), :]
                        d16_ref[0, 4 * k + r4, :, lanes] = rows16.astype(BF16)
        if c == COL_VB:
            ones_row = (lax.broadcasted_iota(jnp.int32, (VT_ROWS - LANES, qb), 0) == 0).astype(BF16)
            for kb in range(tm // qb):
                for hd in range(N_HEADS_DIFF):
                    blk = acc[kb * qb:(kb + 1) * qb, hd * LANES:(hd + 1) * LANES]
                    vt_ref[0, hd, kb, 0:LANES, :] = blk.T.astype(BF16)
                    vt_ref[0, hd, kb, LANES:VT_ROWS, :] = ones_row


def _proj(h, gain, w_in, layer, col_scale, cos, sa, sb, batch, qb, tm=512):
    m, d = h.shape
    seq = m // batch
    nt = seq // tm
    row = lambda i: (i, 0)
    tab = lambda i: (i % nt, 0)
    res = lambda i: (i // nt, 0, i % nt, 0)
    return pl.pallas_call(
        functools.partial(_proj_kernel, qb=qb),
        grid=(m // tm,),
        in_specs=[pl.BlockSpec((tm, d), row),
                  pl.BlockSpec((1, d), lambda i: (0, 0)),
                  pl.BlockSpec((1, d, IN_COLS), lambda i: (layer, 0, 0), pipeline_mode=pl.Buffered(1)),
                  pl.BlockSpec((1, IN_COLS), lambda i: (0, 0)),
                  pl.BlockSpec((tm, LANES), tab),
                  pl.BlockSpec((tm, LANES), tab),
                  pl.BlockSpec((tm, LANES), tab)],
        out_specs=[pl.BlockSpec((tm, IN_COLS), row),
                   pl.BlockSpec((1, 4, tm // 4, 3 * WIDTH), res),
                   pl.BlockSpec((1, 16, tm // 16, 3 * WIDTH), res),
                   pl.BlockSpec((1, N_HEADS_DIFF, tm // qb, VT_ROWS, qb), lambda i: (i // nt, 0, i % nt, 0, 0))],
        out_shape=[jax.ShapeDtypeStruct((m, IN_COLS), BF16),
                   jax.ShapeDtypeStruct((batch, 4, seq // 4, 3 * WIDTH), BF16),
                   jax.ShapeDtypeStruct((batch, 16, seq // 16, 3 * WIDTH), BF16),
                   jax.ShapeDtypeStruct((batch, N_HEADS_DIFF, seq // qb, VT_ROWS, qb), BF16)],
        scratch_shapes=[pltpu.VMEM((WIDTH // LANES, tm, LANES), F32),
                        pltpu.VMEM((WIDTH // LANES, tm, LANES), F32),
                        pltpu.VMEM((d, IN_COLS), BF16)],
        compiler_params=pltpu.CompilerParams(dimension_semantics=("arbitrary",),
                                             vmem_limit_bytes=PROJ_VMEM_LIMIT),
        name="proj",
    )(h, gain.reshape(1, d), w_in, col_scale.reshape(1, IN_COLS), cos, sa, sb)


def _rope_tables(seq):
    half = ROPE_DIM // 2
    inv = jnp.power(ROPE_THETA, -jnp.arange(half, dtype=F32) * (2.0 / ROPE_DIM))
    ang = jnp.arange(seq).astype(F32)[:, None] * inv[None, :]
    cos, sin = jnp.cos(ang), jnp.sin(ang)
    ch = jnp.arange(LANES) % HEAD_DIM
    cos_l = jnp.take(cos, ch % half, axis=1)
    sin_l = jnp.take(sin, ch % half, axis=1)
    cos_t = jnp.where(ch < ROPE_DIM, cos_l, 1.0)
    sa = jnp.where(ch < half, -sin_l, 0.0)
    sb = jnp.where((ch >= half) & (ch < ROPE_DIM), sin_l, 0.0)
    return cos_t, sa, sb


def _dswa_kernel(q_ref, kc_ref, kp_ref, vc_ref, vp_ref, o_ref, l_ref, vt_sc, bias_sc, k_sc, sa_sc, sb_sc, *,
                 lc):
    w = DSWA_W
    nblk = lc // w
    first_chunk = pl.program_id(2) == 0
    head_a = lax.broadcasted_iota(jnp.int32, (1, LANES), 1) < HEAD_DIM
    kj = lax.broadcasted_iota(jnp.int32, (2 * w, 2 * w), 0)
    qi = lax.broadcasted_iota(jnp.int32, (2 * w, 2 * w), 1) & (w - 1)
    dist = w + qi - kj
    band = (dist >= 0) & (dist <= w)
    bias_sc[0] = jnp.where(band, 0.0, NEG)
    bias_sc[1] = jnp.where(band & ((kj >= w) | jnp.logical_not(first_chunk)), 0.0, NEG)

    def transposed(v):
        return v.T

    ones_row = (lax.broadcasted_iota(jnp.int32, (VT_ROWS - LANES, w), 0) == 0).astype(BF16)
    for hp in range(WIDTH // LANES):
        lanes = slice(hp * LANES, (hp + 1) * LANES)
        vt_sc[0, hp, 0:LANES, :] = transposed(vp_ref[0, 0, :, lanes])
        vt_sc[0, hp, LANES:VT_ROWS, :] = ones_row
        for n in range(nblk):
            vt_sc[n + 1, hp, 0:LANES, :] = transposed(vc_ref[0, 0, n * w:(n + 1) * w, lanes])
            vt_sc[n + 1, hp, LANES:VT_ROWS, :] = ones_row

    k_sc[0:w, :] = kp_ref[0, 0]
    k_sc[w:w + lc, :] = kc_ref[0, 0]

    def scores(n, hp):
        lanes = slice(hp * LANES, (hp + 1) * LANES)
        row0 = pl.multiple_of(n * w, w)
        qb = q_ref[0, 0, pl.ds(row0, w), lanes]
        zero = jnp.zeros_like(qb)
        qs = jnp.concatenate([jnp.where(head_a, qb, zero), jnp.where(head_a, zero, qb)], axis=0)
        kk = k_sc[pl.ds(row0, 2 * w), lanes]
        return lax.dot_general(kk, qs, (((1,), (1,)), ((), ())), preferred_element_type=F32)

    def finish(n, hp, raw):
        lanes = slice(hp * LANES, (hp + 1) * LANES)
        row0 = pl.multiple_of(n * w, w)
        bias = bias_sc[jnp.where(n == 0, 1, 0)]
        ms, ps = [], []
        for half in (slice(0, w), slice(w, 2 * w)):
            s = raw[:, half] + bias[:, half]
            m_h = jnp.max(s, axis=0, keepdims=True)
            ms.append(m_h)
            ps.append(jnp.exp2(s - m_h).astype(BF16))
        m = jnp.concatenate(ms, axis=1)
        p = jnp.concatenate(ps, axis=1)
        vv = jnp.concatenate([vt_sc[n, hp], vt_sc[n + 1, hp]], axis=1)
        pv = jnp.dot(vv, p, preferred_element_type=F32)
        den = pv[LANES:LANES + 1, :]
        o = pv[0:LANES, :] * (1.0 / den)
        lse = m + jnp.log2(den)
        o_sel = jnp.concatenate([o[:HEAD_DIM, :w], o[HEAD_DIM:, w:]], axis=0)
        l_sel = jnp.concatenate([jnp.broadcast_to(lse[:, :w], (HEAD_DIM, w)),
                                 jnp.broadcast_to(lse[:, w:], (HEAD_DIM, w))], axis=0)
        o_ref[0, 0, pl.ds(row0, w), lanes] = o_sel.T.astype(o_ref.dtype)
        l_ref[0, 0, pl.ds(row0, w), lanes] = l_sel.T

    nhp = WIDTH // LANES

    def issue(n, s_ref):
        for hp in range(nhp):
            s_ref[hp] = scores(n, hp)

    def consume(n, s_ref):
        for hp in range(nhp):
            finish(n, hp, s_ref[hp])

    issue(0, sa_sc)

    def group(j, carry):
        n0 = DSWA_UNROLL * j
        for i in range(DSWA_UNROLL):
            cur, nxt = (sa_sc, sb_sc) if i % 2 == 0 else (sb_sc, sa_sc)
            issue(jnp.minimum(n0 + i + 1, nblk - 1), nxt)
            consume(n0 + i, cur)
        return carry

    lax.fori_loop(0, nblk // DSWA_UNROLL, group, 0)


def _dswa_group(qkv, splits):
    b, d, sub, _ = qkv.shape
    lc = min(sub, 1024)
    per = lc // DSWA_W
    assert per % DSWA_UNROLL == 0, "the kernel walks query blocks in groups of DSWA_UNROLL"
    sq, sk, sv = splits

    def cur(split):
        return pl.BlockSpec((1, 1, lc, WIDTH), lambda bi, r, c: (bi, r, c, split))

    def prv(split):
        return pl.BlockSpec((1, 1, DSWA_W, WIDTH), lambda bi, r, c: (bi, r, jnp.maximum(c * per - 1, 0), split))

    out = pl.BlockSpec((1, 1, lc, WIDTH), lambda bi, r, c: (bi, r, c, 0))
    return pl.pallas_call(
        functools.partial(_dswa_kernel, lc=lc),
        grid=(b, d, sub // lc),
        in_specs=[cur(sq), cur(sk), prv(sk), cur(sv), prv(sv)],
        out_specs=[out, out],
        out_shape=[jax.ShapeDtypeStruct((b, d, sub, WIDTH), BF16),
                   jax.ShapeDtypeStruct((b, d, sub, WIDTH), F32)],
        scratch_shapes=[pltpu.VMEM((per + 1, WIDTH // LANES, VT_ROWS, DSWA_W), BF16),
                        pltpu.VMEM((2, 2 * DSWA_W, 2 * DSWA_W), F32),
                        pltpu.VMEM((lc + DSWA_W, WIDTH), BF16),
                        pltpu.VMEM((WIDTH // LANES, 2 * DSWA_W, 2 * DSWA_W), F32),
                        pltpu.VMEM((WIDTH // LANES, 2 * DSWA_W, 2 * DSWA_W), F32)],
        compiler_params=_params("arbitrary", "arbitrary", "arbitrary"),
        name=f"dswa_d{d}",
    )(qkv, qkv, qkv, qkv, qkv)


def _diff_kernel(lam_ref, gain_ref, q_ref, k_ref, vt_ref, o_ref, m_sc, acc_sc, sa_sc, sb_sc, *, qb,
                 lambda_init):
    nh = q_ref.shape[2] // LANES
    heads = range(nh)
    head_a = lax.broadcasted_iota(jnp.int32, (1, LANES), 1) < HEAD_DIM
    key = lax.broadcasted_iota(jnp.int32, (qb, 2 * qb), 0)
    qry = lax.broadcasted_iota(jnp.int32, (qb, 2 * qb), 1) & (qb - 1)
    lam_p = lam_ref[...]
    lam = (jnp.exp(jnp.sum(lam_p[0:1] * lam_p[1:2], axis=1, keepdims=True))
           - jnp.exp(jnp.sum(lam_p[2:3] * lam_p[3:4], axis=1, keepdims=True)) + lambda_init)

    def query_block(qi, carry):
        q_rows = pl.ds(pl.multiple_of(qi * qb, qb), qb)

        def stacked_queries(h):
            q = q_ref[0, q_rows, h * LANES:(h + 1) * LANES]
            zero = jnp.zeros_like(q)
            return jnp.concatenate([jnp.where(head_a, q, zero), jnp.where(head_a, zero, q)], axis=0)

        qs = [stacked_queries(h) for h in heads]
        m_sc[...] = jnp.full(m_sc.shape, NEG, F32)
        acc_sc[...] = jnp.zeros(acc_sc.shape, F32)

        def issue(blk, s_ref):
            rows = pl.ds(pl.multiple_of(blk * qb, qb), qb)
            for h in heads:
                kb = k_ref[0, rows, h * LANES:(h + 1) * LANES]
                s_ref[h] = lax.dot_general(kb, qs[h], (((1,), (1,)), ((), ())), preferred_element_type=F32)

        def consume(blk, s_ref, mask):
            for h in heads:
                s = s_ref[h]
                if mask is not None:
                    s = jnp.where(mask, s, NEG)
                m_old = m_sc[h]
                m_new = jnp.maximum(m_old, jnp.max(s, axis=0, keepdims=True))
                alpha = jnp.exp2(m_old - m_new)
                p = jnp.exp2(s - m_new).astype(BF16)
                acc_sc[h] = alpha * acc_sc[h] + jnp.dot(vt_ref[0, h, blk], p, preferred_element_type=F32)
                m_sc[h] = m_new

        last = jnp.maximum(qi - 1, 0)
        issue(qi, sa_sc)
        issue(0, sb_sc)
        consume(qi, sa_sc, key <= qry)

        def pair(j, c):
            b0 = 2 * j
            issue(b0 + 1, sa_sc)
            consume(b0, sb_sc, None)
            issue(jnp.minimum(b0 + 2, last), sb_sc)
            consume(b0 + 1, sa_sc, None)
            return c

        lax.fori_loop(0, qi // 2, pair, 0)

        @pl.when(qi % 2 == 1)
        def _():
            consume(qi - 1, sb_sc, None)

        for h in heads:
            o_all = acc_sc[h, 0:LANES, :] / acc_sc[h, LANES:LANES + 1, :]
            o = o_all[:, :qb] - lam * o_all[:, qb:]
            o = o * lax.rsqrt(jnp.mean(o * o, axis=0, keepdims=True) + SUBLN_EPS) * gain_ref[...]
            o_ref[0, q_rows, h * LANES:(h + 1) * LANES] = (o * (1.0 - lambda_init)).T.astype(o_ref.dtype)
        return carry

    lax.fori_loop(0, q_ref.shape[1] // qb, query_block, 0)


def _diff(u3, vt, lam_p, subln_gain, lambda_init, qb, nh=2):
    b, s, _ = u3.shape
    per = WIDTH // (nh * LANES)
    nb = s // qb
    seq = lambda split: pl.BlockSpec((1, s, nh * LANES), lambda bi, g: (bi, 0, split * per + g))
    return pl.pallas_call(
        functools.partial(_diff_kernel, qb=qb, lambda_init=lambda_init),
        grid=(b, N_HEADS_DIFF // nh),
        in_specs=[pl.BlockSpec(lam_p.shape, lambda bi, g: (0, 0)),
                  pl.BlockSpec((LANES, 1), lambda bi, g: (0, 0)),
                  seq(COL_QB), seq(COL_KB),
                  pl.BlockSpec((1, nh, nb, VT_ROWS, qb), lambda bi, g: (bi, g, 0, 0, 0))],
        out_specs=pl.BlockSpec((1, s, nh * LANES), lambda bi, g: (bi, 0, g)),
        out_shape=jax.ShapeDtypeStruct((b, s, WIDTH), BF16),
        scratch_shapes=[pltpu.VMEM((nh, 1, 2 * qb), F32),
                        pltpu.VMEM((nh, VT_ROWS, 2 * qb), F32),
                        pltpu.VMEM((nh, qb, 2 * qb), F32), pltpu.VMEM((nh, qb, 2 * qb), F32)],
        compiler_params=_params("arbitrary", "arbitrary"),
        name="diff",
    )(lam_p, subln_gain.reshape(LANES, 1), u3, u3, vt)


def _silu_of_half(half):
    return half + half * jnp.tanh(half)


def _post_kernel(h_ref, ga_ref, gb_ref, o1_ref, l1_ref, o4_ref, l4_ref, o16_ref, l16_ref, ob_ref, p_ref,
                 wo32_ref, ng_ref, wg32_ref, wp32_ref, fg_ref, out_ref, on_sc, ln_sc, o4_sc, l4_sc,
                 wo_ref, wg_ref, wp_ref, *, final):
    tm = h_ref.shape[0]

    @pl.when(pl.program_id(0) == 0)
    def _():
        wo_ref[...] = wo32_ref[0].astype(BF16)
        wg_ref[...] = (0.5 * wg32_ref[0]).astype(BF16)
        wp_ref[...] = (0.5 * wp32_ref[0]).astype(BF16)

    n4, n16 = tm // 4, tm // 16
    merged = []
    for j in range(WIDTH // LANES):
        lanes = slice(j * LANES, (j + 1) * LANES)
        for r4 in range(4):
            for k in range(4):
                o4_sc[pl.ds(k, n16, stride=4), :] = o16_ref[0, 4 * k + r4, :, lanes].astype(F32)
                l4_sc[pl.ds(k, n16, stride=4), :] = l16_ref[0, 4 * k + r4, :, lanes]
            o16, l16 = o4_sc[...], l4_sc[...]
            o4, l4 = o4_ref[0, r4, :, lanes].astype(F32), l4_ref[0, r4, :, lanes]
            top = jnp.maximum(l4, l16)
            w4, w16 = jnp.exp2(l4 - top), jnp.exp2(l16 - top)
            den = w4 + w16
            on_sc[pl.ds(r4, n4, stride=4), :] = (w4 * o4 + w16 * o16) / den
            ln_sc[pl.ds(r4, n4, stride=4), :] = top + jnp.log2(den)
        o1, l1 = o1_ref[:, lanes].astype(F32), l1_ref[:, lanes]
        ox, lx = on_sc[...], ln_sc[...]
        top = jnp.maximum(l1, lx)
        w1, wx = jnp.exp2(l1 - top), jnp.exp2(lx - top)
        merged.append((w1 * o1 + wx * ox) / (w1 + wx))
    oa = jnp.concatenate(merged, axis=1)

    ya = (oa * _silu_of_half(ga_ref[...].astype(F32))).astype(BF16)
    yb = (ob_ref[...].astype(F32) * _silu_of_half(gb_ref[...].astype(F32))).astype(BF16)
    y = (jnp.dot(ya, wo_ref[0:WIDTH, :], preferred_element_type=F32)
         + jnp.dot(yb, wo_ref[WIDTH:2 * WIDTH, :], preferred_element_type=F32))
    h1 = h_ref[...] + y
    n = h1 * lax.rsqrt(jnp.mean(h1 * h1, axis=-1, keepdims=True) + RMS_EPS) * ng_ref[...]
    t = jnp.tanh(jnp.dot(n.astype(BF16), wg_ref[...], preferred_element_type=F32))
    ple_half = jnp.dot(p_ref[0].astype(BF16), wp_ref[...], preferred_element_type=F32)
    h2 = h1 + ple_half + ple_half * t
    if final:
        h2 = h2 * lax.rsqrt(jnp.mean(h2 * h2, axis=-1, keepdims=True) + RMS_EPS) * fg_ref[...]
    out_ref[...] = h2


def _post(h, u, g1, g4, g16, ob, p, w_out, ple_gain, w_gate, w_ple, layer, final_gain, final, batch, tm=512):
    m, d = h.shape
    whole = lambda w: pl.BlockSpec((1,) + w.shape[1:], lambda i: (layer, 0, 0), pipeline_mode=pl.Buffered(1))
    nt = m // batch // tm
    row = lambda i: (i, 0)
    const = lambda i: (0, 0)
    res = lambda i: (i // nt, 0, i % nt, 0)
    grp = lambda dil: pl.BlockSpec((1, dil, tm // dil, WIDTH), res)
    return pl.pallas_call(
        functools.partial(_post_kernel, final=final),
        grid=(m // tm,),
        in_specs=[pl.BlockSpec((tm, d), row),
                  pl.BlockSpec((tm, WIDTH), lambda i: (i, COL_GA)),
                  pl.BlockSpec((tm, WIDTH), lambda i: (i, COL_GB)),
                  pl.BlockSpec((tm, WIDTH), row), pl.BlockSpec((tm, WIDTH), row),
                  grp(4), grp(4), grp(16), grp(16),
                  pl.BlockSpec((tm, WIDTH), row),
                  pl.BlockSpec((1, tm, p.shape[2]), lambda i: (layer, i, 0)),
                  whole(w_out),
                  pl.BlockSpec((1, d), const),
                  whole(w_gate),
                  whole(w_ple),
                  pl.BlockSpec((1, d), const)],
        out_specs=pl.BlockSpec((tm, d), row),
        out_shape=jax.ShapeDtypeStruct((m, d), F32),
        scratch_shapes=[pltpu.VMEM((tm, LANES), F32), pltpu.VMEM((tm, LANES), F32),
                        pltpu.VMEM((tm // 4, LANES), F32), pltpu.VMEM((tm // 4, LANES), F32),
                        pltpu.VMEM(w_out.shape[1:], BF16), pltpu.VMEM(w_gate.shape[1:], BF16),
                        pltpu.VMEM(w_ple.shape[1:], BF16)],
        compiler_params=_params("arbitrary"),
        name="post",
    )(h, u, u, g1[0].reshape(m, WIDTH), g1[1].reshape(m, WIDTH), g4[0], g4[1], g16[0], g16[1], ob, p,
      w_out, ple_gain.reshape(1, d), w_gate, w_ple, final_gain.reshape(1, d))


def kernel(x, p, attn_norm_gain, w_in, w_out, lambda_q1, lambda_k1, lambda_q2, lambda_k2, subln_gain,
           ple_norm_gain, w_ple_gate, w_ple, final_norm_gain):
    b, s, d = x.shape
    depth = w_in.shape[0]
    h = x.reshape(b * s, d)
    cos, sa, sb = _rope_tables(s)
    col = jnp.arange(IN_COLS) // WIDTH
    col_scale = jnp.where((col == COL_QA) | (col == COL_QB), HEAD_DIM ** -0.5 * LOG2_E,
                          jnp.where((col == COL_GA) | (col == COL_GB), 0.5, 1.0)).astype(F32)
    a_splits = (COL_QA, COL_KA, COL_VA)
    for i in range(depth):
        lambda_init = 0.8 - 0.6 * math.exp(-0.3 * i)
        u, qkv4, qkv16, vt = _proj(h, attn_norm_gain[i], w_in, i, col_scale, cos, sa, sb, b, DIFF_QB)
        u3 = u.reshape(b, s, IN_COLS)
        g1 = _dswa_group(u3.reshape(b, 1, s, IN_COLS), a_splits)
        g4 = _dswa_group(qkv4, a_splits)
        g16 = _dswa_group(qkv16, a_splits)
        lam_p = jnp.stack([lambda_q1[i], lambda_k1[i], lambda_q2[i], lambda_k2[i]]).astype(F32)
        ob = _diff(u3, vt, lam_p, subln_gain[i], lambda_init, DIFF_QB)
        h = _post(h, u, g1, g4, g16, ob.reshape(b * s, WIDTH), p.reshape(depth, b * s, -1),
                  w_out, ple_norm_gain[i], w_ple_gate, w_ple, i, final_norm_gain, final=i == depth - 1, batch=b)
    return h.reshape(b, s, d)
```

```python
import functools
import math

import jax
import jax.numpy as jnp
from jax import lax
from jax.experimental import pallas as pl
from jax.experimental.pallas import tpu as pltpu

HEAD_DIM = 64
LANES = 128
DSWA_W = 128
DSWA_UNROLL = 4
N_HEADS_DIFF = 4
DIFF_QB = 512
POST_PARTS = 2
VT_ROWS = LANES + 16
WIDTH = 512
IN_COLS = 8 * WIDTH
ROPE_THETA = 500000.0
ROPE_DIM = HEAD_DIM // 4
RMS_EPS = 1e-6
SUBLN_EPS = 1e-5
NEG = -1e30
LOG2_E = math.log2(math.e)
VMEM_LIMIT = 48 * 1024 * 1024
PROJ_VMEM_LIMIT = 58 * 1024 * 1024

COL_QA, COL_KA, COL_VA, COL_GA, COL_QB, COL_KB, COL_VB, COL_GB = range(8)
ROPE_SPLITS = (COL_QA, COL_KA, COL_QB, COL_KB)

BF16 = jnp.bfloat16
F32 = jnp.float32


def _params(*sem):
    return pltpu.CompilerParams(dimension_semantics=sem, vmem_limit_bytes=VMEM_LIMIT)


def _proj_kernel(x_ref, g_ref, w32_ref, ws_ref, cos_ref, sa_ref, sb_ref, o_ref, d4_ref, d16_ref, vt_ref, buf_sc,
                 buf4_sc, w_ref, *, qb):
    tm = x_ref.shape[0]

    @pl.when(pl.program_id(0) == 0)
    def _():
        for c in range(IN_COLS // WIDTH):
            cols = slice(c * WIDTH, (c + 1) * WIDTH)
            w_ref[:, cols] = (w32_ref[0, :, cols] * ws_ref[:, cols]).astype(BF16)

    x = x_ref[...]
    ms = jnp.mean(x * x, axis=-1, keepdims=True)
    xn = (x * lax.rsqrt(ms + RMS_EPS) * g_ref[...]).astype(BF16)
    cos, sa, sb = cos_ref[...], sa_ref[...], sb_ref[...]
    for c in range(IN_COLS // WIDTH):
        cols = slice(c * WIDTH, (c + 1) * WIDTH)
        acc = jnp.dot(xn, w_ref[:, cols], preferred_element_type=F32)
        if c in ROPE_SPLITS:
            blocks = []
            for j in range(WIDTH // LANES):
                blk = acc[:, j * LANES:(j + 1) * LANES]
                blocks.append(blk * cos + pltpu.roll(blk, LANES - ROPE_DIM // 2, 1) * sa
                              + pltpu.roll(blk, ROPE_DIM // 2, 1) * sb)
            acc = jnp.concatenate(blocks, axis=1)
        o_ref[:, cols] = acc.astype(BF16)
        if c in (COL_QA, COL_KA, COL_VA):
            n4, n16 = tm // 4, tm // 16
            for j in range(WIDTH // LANES):
                lanes = slice(c * WIDTH + j * LANES, c * WIDTH + (j + 1) * LANES)
                buf_sc[j] = acc[:, j * LANES:(j + 1) * LANES]
                for r4 in range(4):
                    rows4 = buf_sc[j, pl.ds(r4, n4, stride=4), :]
                    d4_ref[0, r4, :, lanes] = rows4.astype(BF16)
                    buf4_sc[j, r4 * n4:(r4 + 1) * n4, :] = rows4
                for r4 in range(4):
                    for k in range(4):
                        rows16 = buf4_sc[j, pl.ds(r4 * n4 + k, n16, stride=4), :]
                        d16_ref[0, 4 * k + r4, :, lanes] = rows16.astype(BF16)
        if c == COL_VB:
            ones_row = (lax.broadcasted_iota(jnp.int32, (VT_ROWS - LANES, qb), 0) == 0).astype(BF16)
            for kb in range(tm // qb):
                for hd in range(N_HEADS_DIFF):
                    blk = acc[kb * qb:(kb + 1) * qb, hd * LANES:(hd + 1) * LANES]
                    vt_ref[0, hd, kb, 0:LANES, :] = blk.T.astype(BF16)
                    vt_ref[0, hd, kb, LANES:VT_ROWS, :] = ones_row


def _proj(h, gain, w_in, layer, col_scale, cos, sa, sb, batch, qb, tm=512):
    m, d = h.shape
    seq = m // batch
    nt = seq // tm
    row = lambda i: (i, 0)
    tab = lambda i: (i % nt, 0)
    res = lambda i: (i // nt, 0, i % nt, 0)
    return pl.pallas_call(
        functools.partial(_proj_kernel, qb=qb),
        grid=(m // tm,),
        in_specs=[pl.BlockSpec((tm, d), row),
                  pl.BlockSpec((1, d), lambda i: (0, 0)),
                  pl.BlockSpec((1, d, IN_COLS), lambda i: (layer, 0, 0), pipeline_mode=pl.Buffered(1)),
                  pl.BlockSpec((1, IN_COLS), lambda i: (0, 0)),
                  pl.BlockSpec((tm, LANES), tab),
                  pl.BlockSpec((tm, LANES), tab),
                  pl.BlockSpec((tm, LANES), tab)],
        out_specs=[pl.BlockSpec((tm, IN_COLS), row),
                   pl.BlockSpec((1, 4, tm // 4, 3 * WIDTH), res),
                   pl.BlockSpec((1, 16, tm // 16, 3 * WIDTH), res),
                   pl.BlockSpec((1, N_HEADS_DIFF, tm // qb, VT_ROWS, qb), lambda i: (i // nt, 0, i % nt, 0, 0))],
        out_shape=[jax.ShapeDtypeStruct((m, IN_COLS), BF16),
                   jax.ShapeDtypeStruct((batch, 4, seq // 4, 3 * WIDTH), BF16),
                   jax.ShapeDtypeStruct((batch, 16, seq // 16, 3 * WIDTH), BF16),
                   jax.ShapeDtypeStruct((batch, N_HEADS_DIFF, seq // qb, VT_ROWS, qb), BF16)],
        scratch_shapes=[pltpu.VMEM((WIDTH // LANES, tm, LANES), F32),
                        pltpu.VMEM((WIDTH // LANES, tm, LANES), F32),
                        pltpu.VMEM((d, IN_COLS), BF16)],
        compiler_params=pltpu.CompilerParams(dimension_semantics=("arbitrary",),
                                             vmem_limit_bytes=PROJ_VMEM_LIMIT),
        name="proj",
    )(h, gain.reshape(1, d), w_in, col_scale.reshape(1, IN_COLS), cos, sa, sb)


def _rope_tables(seq):
    half = ROPE_DIM // 2
    inv = jnp.power(ROPE_THETA, -jnp.arange(half, dtype=F32) * (2.0 / ROPE_DIM))
    ang = jnp.arange(seq).astype(F32)[:, None] * inv[None, :]
    cos, sin = jnp.cos(ang), jnp.sin(ang)
    ch = jnp.arange(LANES) % HEAD_DIM
    cos_l = jnp.take(cos, ch % half, axis=1)
    sin_l = jnp.take(sin, ch % half, axis=1)
    cos_t = jnp.where(ch < ROPE_DIM, cos_l, 1.0)
    sa = jnp.where(ch < half, -sin_l, 0.0)
    sb = jnp.where((ch >= half) & (ch < ROPE_DIM), sin_l, 0.0)
    return cos_t, sa, sb


def _dswa_kernel(q_ref, kc_ref, kp_ref, vc_ref, vp_ref, o_ref, l_ref, vt_sc, bias_sc, k_sc, sa_sc, sb_sc, *,
                 lc):
    w = DSWA_W
    nblk = lc // w
    first_chunk = pl.program_id(2) == 0
    head_a = lax.broadcasted_iota(jnp.int32, (1, LANES), 1) < HEAD_DIM
    kj = lax.broadcasted_iota(jnp.int32, (2 * w, 2 * w), 0)
    qi = lax.broadcasted_iota(jnp.int32, (2 * w, 2 * w), 1) & (w - 1)
    dist = w + qi - kj
    band = (dist >= 0) & (dist <= w)
    bias_sc[0] = jnp.where(band, 0.0, NEG)
    bias_sc[1] = jnp.where(band & ((kj >= w) | jnp.logical_not(first_chunk)), 0.0, NEG)

    def transposed(v):
        return v.T

    ones_row = (lax.broadcasted_iota(jnp.int32, (VT_ROWS - LANES, w), 0) == 0).astype(BF16)
    for hp in range(WIDTH // LANES):
        lanes = slice(hp * LANES, (hp + 1) * LANES)
        vt_sc[0, hp, 0:LANES, :] = transposed(vp_ref[0, 0, :, lanes])
        vt_sc[0, hp, LANES:VT_ROWS, :] = ones_row
        for n in range(nblk):
            vt_sc[n + 1, hp, 0:LANES, :] = transposed(vc_ref[0, 0, n * w:(n + 1) * w, lanes])
            vt_sc[n + 1, hp, LANES:VT_ROWS, :] = ones_row

    k_sc[0:w, :] = kp_ref[0, 0]
    k_sc[w:w + lc, :] = kc_ref[0, 0]

    def scores(n, hp):
        lanes = slice(hp * LANES, (hp + 1) * LANES)
        row0 = pl.multiple_of(n * w, w)
        qb = q_ref[0, 0, pl.ds(row0, w), lanes]
        zero = jnp.zeros_like(qb)
        qs = jnp.concatenate([jnp.where(head_a, qb, zero), jnp.where(head_a, zero, qb)], axis=0)
        kk = k_sc[pl.ds(row0, 2 * w), lanes]
        return lax.dot_general(kk, qs, (((1,), (1,)), ((), ())), preferred_element_type=F32)

    def finish(n, hp, raw):
        lanes = slice(hp * LANES, (hp + 1) * LANES)
        row0 = pl.multiple_of(n * w, w)
        bias = bias_sc[jnp.where(n == 0, 1, 0)]
        ms, ps = [], []
        for half in (slice(0, w), slice(w, 2 * w)):
            s = raw[:, half] + bias[:, half]
            m_h = jnp.max(s, axis=0, keepdims=True)
            ms.append(m_h)
            ps.append(jnp.exp2(s - m_h).astype(BF16))
        m = jnp.concatenate(ms, axis=1)
        p = jnp.concatenate(ps, axis=1)
        vv = jnp.concatenate([vt_sc[n, hp], vt_sc[n + 1, hp]], axis=1)
        pv = jnp.dot(vv, p, preferred_element_type=F32)
        den = pv[LANES:LANES + 1, :]
        o = pv[0:LANES, :] * (1.0 / den)
        lse = m + jnp.log2(den)
        o_sel = jnp.concatenate([o[:HEAD_DIM, :w], o[HEAD_DIM:, w:]], axis=0)
        l_sel = jnp.concatenate([jnp.broadcast_to(lse[:, :w], (HEAD_DIM, w)),
                                 jnp.broadcast_to(lse[:, w:], (HEAD_DIM, w))], axis=0)
        o_ref[0, 0, pl.ds(row0, w), lanes] = o_sel.T.astype(o_ref.dtype)
        l_ref[0, 0, pl.ds(row0, w), lanes] = l_sel.T

    nhp = WIDTH // LANES

    def issue(n, s_ref):
        for hp in range(nhp):
            s_ref[hp] = scores(n, hp)

    def consume(n, s_ref):
        for hp in range(nhp):
            finish(n, hp, s_ref[hp])

    issue(0, sa_sc)

    def group(j, carry):
        n0 = DSWA_UNROLL * j
        for i in range(DSWA_UNROLL):
            cur, nxt = (sa_sc, sb_sc) if i % 2 == 0 else (sb_sc, sa_sc)
            issue(jnp.minimum(n0 + i + 1, nblk - 1), nxt)
            consume(n0 + i, cur)
        return carry

    lax.fori_loop(0, nblk // DSWA_UNROLL, group, 0)


def _dswa_group(qkv, splits):
    b, d, sub, _ = qkv.shape
    lc = min(sub, 1024)
    per = lc // DSWA_W
    assert per % DSWA_UNROLL == 0, "the kernel walks query blocks in groups of DSWA_UNROLL"
    sq, sk, sv = splits

    def cur(split):
        return pl.BlockSpec((1, 1, lc, WIDTH), lambda bi, r, c: (bi, r, c, split))

    def prv(split):
        return pl.BlockSpec((1, 1, DSWA_W, WIDTH), lambda bi, r, c: (bi, r, jnp.maximum(c * per - 1, 0), split))

    out = pl.BlockSpec((1, 1, lc, WIDTH), lambda bi, r, c: (bi, r, c, 0))
    return pl.pallas_call(
        functools.partial(_dswa_kernel, lc=lc),
        grid=(b, d, sub // lc),
        in_specs=[cur(sq), cur(sk), prv(sk), cur(sv), prv(sv)],
        out_specs=[out, out],
        out_shape=[jax.ShapeDtypeStruct((b, d, sub, WIDTH), BF16),
                   jax.ShapeDtypeStruct((b, d, sub, WIDTH), F32)],
        scratch_shapes=[pltpu.VMEM((per + 1, WIDTH // LANES, VT_ROWS, DSWA_W), BF16),
                        pltpu.VMEM((2, 2 * DSWA_W, 2 * DSWA_W), F32),
                        pltpu.VMEM((lc + DSWA_W, WIDTH), BF16),
                        pltpu.VMEM((WIDTH // LANES, 2 * DSWA_W, 2 * DSWA_W), F32),
                        pltpu.VMEM((WIDTH // LANES, 2 * DSWA_W, 2 * DSWA_W), F32)],
        compiler_params=_params("arbitrary", "arbitrary", "arbitrary"),
        name=f"dswa_d{d}",
    )(qkv, qkv, qkv, qkv, qkv)


def _diff_kernel(lam_ref, gain_ref, q_ref, k_ref, vt_ref, o_ref, m_sc, acc_sc, sa_sc, sb_sc, *, qb,
                 lambda_init):
    nh = q_ref.shape[2] // LANES
    heads = range(nh)
    head_a = lax.broadcasted_iota(jnp.int32, (1, LANES), 1) < HEAD_DIM
    key = lax.broadcasted_iota(jnp.int32, (qb, 2 * qb), 0)
    qry = lax.broadcasted_iota(jnp.int32, (qb, 2 * qb), 1) & (qb - 1)
    lam_p = lam_ref[...]
    lam = (jnp.exp(jnp.sum(lam_p[0:1] * lam_p[1:2], axis=1, keepdims=True))
           - jnp.exp(jnp.sum(lam_p[2:3] * lam_p[3:4], axis=1, keepdims=True)) + lambda_init)

    def query_block(qi, carry):
        q_rows = pl.ds(pl.multiple_of(qi * qb, qb), qb)

        def stacked_queries(h):
            q = q_ref[0, q_rows, h * LANES:(h + 1) * LANES]
            zero = jnp.zeros_like(q)
            return jnp.concatenate([jnp.where(head_a, q, zero), jnp.where(head_a, zero, q)], axis=0)

        qs = [stacked_queries(h) for h in heads]
        m_sc[...] = jnp.full(m_sc.shape, NEG, F32)
        acc_sc[...] = jnp.zeros(acc_sc.shape, F32)

        def issue(blk, s_ref):
            rows = pl.ds(pl.multiple_of(blk * qb, qb), qb)
            for h in heads:
                kb = k_ref[0, rows, h * LANES:(h + 1) * LANES]
                s_ref[h] = lax.dot_general(kb, qs[h], (((1,), (1,)), ((), ())), preferred_element_type=F32)

        def consume(blk, s_ref, mask):
            for h in heads:
                s = s_ref[h]
                if mask is not None:
                    s = jnp.where(mask, s, NEG)
                m_old = m_sc[h]
                m_new = jnp.maximum(m_old, jnp.max(s, axis=0, keepdims=True))
                alpha = jnp.exp2(m_old - m_new)
                p = jnp.exp2(s - m_new).astype(BF16)
                acc_sc[h] = alpha * acc_sc[h] + jnp.dot(vt_ref[0, h, blk], p, preferred_element_type=F32)
                m_sc[h] = m_new

        last = jnp.maximum(qi - 1, 0)
        issue(qi, sa_sc)
        issue(0, sb_sc)
        consume(qi, sa_sc, key <= qry)

        def pair(j, c):
            b0 = 2 * j
            issue(b0 + 1, sa_sc)
            consume(b0, sb_sc, None)
            issue(jnp.minimum(b0 + 2, last), sb_sc)
            consume(b0 + 1, sa_sc, None)
            return c

        lax.fori_loop(0, qi // 2, pair, 0)

        @pl.when(qi % 2 == 1)
        def _():
            consume(qi - 1, sb_sc, None)

        for h in heads:
            o_all = acc_sc[h, 0:LANES, :] / acc_sc[h, LANES:LANES + 1, :]
            o = o_all[:, :qb] - lam * o_all[:, qb:]
            o = o * lax.rsqrt(jnp.mean(o * o, axis=0, keepdims=True) + SUBLN_EPS) * gain_ref[...]
            o_ref[0, q_rows, h * LANES:(h + 1) * LANES] = (o * (1.0 - lambda_init)).T.astype(o_ref.dtype)
        return carry

    lax.fori_loop(0, q_ref.shape[1] // qb, query_block, 0)


def _diff(u3, vt, lam_p, subln_gain, lambda_init, qb, nh=2):
    b, s, _ = u3.shape
    per = WIDTH // (nh * LANES)
    nb = s // qb
    seq = lambda split: pl.BlockSpec((1, s, nh * LANES), lambda bi, g: (bi, 0, split * per + g))
    return pl.pallas_call(
        functools.partial(_diff_kernel, qb=qb, lambda_init=lambda_init),
        grid=(b, N_HEADS_DIFF // nh),
        in_specs=[pl.BlockSpec(lam_p.shape, lambda bi, g: (0, 0)),
                  pl.BlockSpec((LANES, 1), lambda bi, g: (0, 0)),
                  seq(COL_QB), seq(COL_KB),
                  pl.BlockSpec((1, nh, nb, VT_ROWS, qb), lambda bi, g: (bi, g, 0, 0, 0))],
        out_specs=pl.BlockSpec((1, s, nh * LANES), lambda bi, g: (bi, 0, g)),
        out_shape=jax.ShapeDtypeStruct((b, s, WIDTH), BF16),
        scratch_shapes=[pltpu.VMEM((nh, 1, 2 * qb), F32),
                        pltpu.VMEM((nh, VT_ROWS, 2 * qb), F32),
                        pltpu.VMEM((nh, qb, 2 * qb), F32), pltpu.VMEM((nh, qb, 2 * qb), F32)],
        compiler_params=_params("arbitrary", "arbitrary"),
        name="diff",
    )(lam_p, subln_gain.reshape(LANES, 1), u3, u3, vt)


def _silu_of_half(half):
    return half + half * jnp.tanh(half)


def _post_kernel(h_ref, ga_ref, gb_ref, o1_ref, l1_ref, o4_ref, l4_ref, o16_ref, l16_ref, ob_ref, p_ref,
                 wo32_ref, ng_ref, wg32_ref, wp32_ref, fg_ref, out_ref, on_sc, ln_sc, o4_sc, l4_sc,
                 wo_ref, wg_ref, wp_ref, *, final):
    tm = h_ref.shape[0]

    @pl.when(pl.program_id(0) == 0)
    def _():
        wo_ref[...] = wo32_ref[0].astype(BF16)
        wg_ref[...] = (0.5 * wg32_ref[0]).astype(BF16)
        wp_ref[...] = (0.5 * wp32_ref[0]).astype(BF16)

    hr = tm // POST_PARTS
    n4, n16 = hr // 4, hr // 16

    def merge(part):
        rows, rows4, rows16 = (pl.ds(part * n, n) for n in (hr, n4, n16))
        merged = []
        for j in range(WIDTH // LANES):
            lanes = slice(j * LANES, (j + 1) * LANES)
            for r4 in range(4):
                for k in range(4):
                    o4_sc[pl.ds(k, n16, stride=4), :] = o16_ref[0, 4 * k + r4, rows16, lanes].astype(F32)
                    l4_sc[pl.ds(k, n16, stride=4), :] = l16_ref[0, 4 * k + r4, rows16, lanes]
                o16, l16 = o4_sc[...], l4_sc[...]
                o4, l4 = o4_ref[0, r4, rows4, lanes].astype(F32), l4_ref[0, r4, rows4, lanes]
                top = jnp.maximum(l4, l16)
                w4, w16 = jnp.exp2(l4 - top), jnp.exp2(l16 - top)
                den = w4 + w16
                on_sc[pl.ds(r4, n4, stride=4), :] = (w4 * o4 + w16 * o16) / den
                ln_sc[pl.ds(r4, n4, stride=4), :] = top + jnp.log2(den)
            o1, l1 = o1_ref[rows, lanes].astype(F32), l1_ref[rows, lanes]
            ox, lx = on_sc[...], ln_sc[...]
            top = jnp.maximum(l1, lx)
            w1, wx = jnp.exp2(l1 - top), jnp.exp2(lx - top)
            merged.append((w1 * o1 + wx * ox) / (w1 + wx))
        return jnp.concatenate(merged, axis=1)

    def mix(part, oa):
        rows = pl.ds(part * hr, hr)
        ya = (oa * _silu_of_half(ga_ref[rows, :].astype(F32))).astype(BF16)
        yb = (ob_ref[rows, :].astype(F32) * _silu_of_half(gb_ref[rows, :].astype(F32))).astype(BF16)
        y = (jnp.dot(ya, wo_ref[0:WIDTH, :], preferred_element_type=F32)
             + jnp.dot(yb, wo_ref[WIDTH:2 * WIDTH, :], preferred_element_type=F32))
        return h_ref[rows, :] + y

    def embed(part, h1):
        rows = pl.ds(part * hr, hr)
        n = h1 * lax.rsqrt(jnp.mean(h1 * h1, axis=-1, keepdims=True) + RMS_EPS) * ng_ref[...]
        t = jnp.tanh(jnp.dot(n.astype(BF16), wg_ref[...], preferred_element_type=F32))
        ple_half = jnp.dot(p_ref[0, rows, :].astype(BF16), wp_ref[...], preferred_element_type=F32)
        h2 = h1 + ple_half + ple_half * t
        if final:
            h2 = h2 * lax.rsqrt(jnp.mean(h2 * h2, axis=-1, keepdims=True) + RMS_EPS) * fg_ref[...]
        out_ref[rows, :] = h2

    h1 = [mix(part, merge(part)) for part in range(POST_PARTS)]
    for part in range(POST_PARTS):
        embed(part, h1[part])


def _post(h, u, g1, g4, g16, ob, p, w_out, ple_gain, w_gate, w_ple, layer, final_gain, final, batch, tm=512):
    m, d = h.shape
    whole = lambda w: pl.BlockSpec((1,) + w.shape[1:], lambda i: (layer, 0, 0), pipeline_mode=pl.Buffered(1))
    nt = m // batch // tm
    row = lambda i: (i, 0)
    const = lambda i: (0, 0)
    res = lambda i: (i // nt, 0, i % nt, 0)
    grp = lambda dil: pl.BlockSpec((1, dil, tm // dil, WIDTH), res)
    return pl.pallas_call(
        functools.partial(_post_kernel, final=final),
        grid=(m // tm,),
        in_specs=[pl.BlockSpec((tm, d), row),
                  pl.BlockSpec((tm, WIDTH), lambda i: (i, COL_GA)),
                  pl.BlockSpec((tm, WIDTH), lambda i: (i, COL_GB)),
                  pl.BlockSpec((tm, WIDTH), row), pl.BlockSpec((tm, WIDTH), row),
                  grp(4), grp(4), grp(16), grp(16),
                  pl.BlockSpec((tm, WIDTH), row),
                  pl.BlockSpec((1, tm, p.shape[2]), lambda i: (layer, i, 0)),
                  whole(w_out),
                  pl.BlockSpec((1, d), const),
                  whole(w_gate),
                  whole(w_ple),
                  pl.BlockSpec((1, d), const)],
        out_specs=pl.BlockSpec((tm, d), row),
        out_shape=jax.ShapeDtypeStruct((m, d), F32),
        scratch_shapes=[pltpu.VMEM((tm // POST_PARTS, LANES), F32), pltpu.VMEM((tm // POST_PARTS, LANES), F32),
                        pltpu.VMEM((tm // POST_PARTS // 4, LANES), F32),
                        pltpu.VMEM((tm // POST_PARTS // 4, LANES), F32),
                        pltpu.VMEM(w_out.shape[1:], BF16), pltpu.VMEM(w_gate.shape[1:], BF16),
                        pltpu.VMEM(w_ple.shape[1:], BF16)],
        compiler_params=_params("arbitrary"),
        name="post",
    )(h, u, u, g1[0].reshape(m, WIDTH), g1[1].reshape(m, WIDTH), g4[0], g4[1], g16[0], g16[1], ob, p,
      w_out, ple_gain.reshape(1, d), w_gate, w_ple, final_gain.reshape(1, d))


def kernel(x, p, attn_norm_gain, w_in, w_out, lambda_q1, lambda_k1, lambda_q2, lambda_k2, subln_gain,
           ple_norm_gain, w_ple_gate, w_ple, final_norm_gain):
    b, s, d = x.shape
    depth = w_in.shape[0]
    h = x.reshape(b * s, d)
    cos, sa, sb = _rope_tables(s)
    col = jnp.arange(IN_COLS) // WIDTH
    col_scale = jnp.where((col == COL_QA) | (col == COL_QB), HEAD_DIM ** -0.5 * LOG2_E,
                          jnp.where((col == COL_GA) | (col == COL_GB), 0.5, 1.0)).astype(F32)
    a_splits = (COL_QA, COL_KA, COL_VA)
    for i in range(depth):
        lambda_init = 0.8 - 0.6 * math.exp(-0.3 * i)
        u, qkv4, qkv16, vt = _proj(h, attn_norm_gain[i], w_in, i, col_scale, cos, sa, sb, b, DIFF_QB)
        u3 = u.reshape(b, s, IN_COLS)
        g1 = _dswa_group(u3.reshape(b, 1, s, IN_COLS), a_splits)
        g4 = _dswa_group(qkv4, a_splits)
        g16 = _dswa_group(qkv16, a_splits)
        lam_p = jnp.stack([lambda_q1[i], lambda_k1[i], lambda_q2[i], lambda_k2[i]]).astype(F32)
        ob = _diff(u3, vt, lam_p, subln_gain[i], lambda_init, DIFF_QB)
        h = _post(h, u, g1, g4, g16, ob.reshape(b * s, WIDTH), p.reshape(depth, b * s, -1),
                  w_out, ple_norm_gain[i], w_ple_gate, w_ple, i, final_norm_gain, final=i == depth - 1, batch=b)
    return h.reshape(b, s, d)
```

```python
import functools
import math

import jax
import jax.numpy as jnp
from jax import lax
from jax.experimental import pallas as pl
from jax.experimental.pallas import tpu as pltpu

HEAD_DIM = 64
LANES = 128
DSWA_W = 128
DSWA_UNROLL = 4
N_HEADS_DIFF = 4
DIFF_QB = 512
POST_PARTS = 2
VT_ROWS = LANES + 16
WIDTH = 512
IN_COLS = 8 * WIDTH
ROPE_THETA = 500000.0
ROPE_DIM = HEAD_DIM // 4
RMS_EPS = 1e-6
SUBLN_EPS = 1e-5
NEG = -1e30
LOG2_E = math.log2(math.e)
VMEM_LIMIT = 48 * 1024 * 1024
PROJ_VMEM_LIMIT = 58 * 1024 * 1024

COL_QA, COL_KA, COL_VA, COL_GA, COL_QB, COL_KB, COL_VB, COL_GB = range(8)
ROPE_SPLITS = (COL_QA, COL_KA, COL_QB, COL_KB)

BF16 = jnp.bfloat16
F32 = jnp.float32


def _params(*sem):
    return pltpu.CompilerParams(dimension_semantics=sem, vmem_limit_bytes=VMEM_LIMIT)


def _proj_kernel(x_ref, g_ref, w32_ref, ws_ref, cos_ref, sa_ref, sb_ref, o_ref, d4_ref, d16_ref, vt_ref, qt_ref,
                 buf_sc, buf4_sc, w_ref, *, qb):
    tm = x_ref.shape[0]

    @pl.when(pl.program_id(0) == 0)
    def _():
        for c in range(IN_COLS // WIDTH):
            cols = slice(c * WIDTH, (c + 1) * WIDTH)
            w_ref[:, cols] = (w32_ref[0, :, cols] * ws_ref[:, cols]).astype(BF16)

    x = x_ref[...]
    ms = jnp.mean(x * x, axis=-1, keepdims=True)
    xn = (x * lax.rsqrt(ms + RMS_EPS) * g_ref[...]).astype(BF16)
    cos, sa, sb = cos_ref[...], sa_ref[...], sb_ref[...]
    for c in range(IN_COLS // WIDTH):
        cols = slice(c * WIDTH, (c + 1) * WIDTH)
        acc = jnp.dot(xn, w_ref[:, cols], preferred_element_type=F32)
        if c in ROPE_SPLITS:
            blocks = []
            for j in range(WIDTH // LANES):
                blk = acc[:, j * LANES:(j + 1) * LANES]
                blocks.append(blk * cos + pltpu.roll(blk, LANES - ROPE_DIM // 2, 1) * sa
                              + pltpu.roll(blk, ROPE_DIM // 2, 1) * sb)
            acc = jnp.concatenate(blocks, axis=1)
        o_ref[:, cols] = acc.astype(BF16)
        if c in (COL_QA, COL_KA, COL_VA):
            n4, n16 = tm // 4, tm // 16
            for j in range(WIDTH // LANES):
                lanes = slice(c * WIDTH + j * LANES, c * WIDTH + (j + 1) * LANES)
                buf_sc[j] = acc[:, j * LANES:(j + 1) * LANES]
                for r4 in range(4):
                    rows4 = buf_sc[j, pl.ds(r4, n4, stride=4), :]
                    d4_ref[0, r4, :, lanes] = rows4.astype(BF16)
                    buf4_sc[j, r4 * n4:(r4 + 1) * n4, :] = rows4
                for r4 in range(4):
                    for k in range(4):
                        rows16 = buf4_sc[j, pl.ds(r4 * n4 + k, n16, stride=4), :]
                        d16_ref[0, 4 * k + r4, :, lanes] = rows16.astype(BF16)
        if c == COL_QB:
            for kb in range(tm // qb):
                for hd in range(N_HEADS_DIFF):
                    blk = acc[kb * qb:(kb + 1) * qb, hd * LANES:(hd + 1) * LANES]
                    qt_ref[0, hd, kb] = blk.T.astype(BF16)
        if c == COL_VB:
            ones_row = (lax.broadcasted_iota(jnp.int32, (VT_ROWS - LANES, qb), 0) == 0).astype(BF16)
            for kb in range(tm // qb):
                for hd in range(N_HEADS_DIFF):
                    blk = acc[kb * qb:(kb + 1) * qb, hd * LANES:(hd + 1) * LANES]
                    vt_ref[0, hd, kb, 0:LANES, :] = blk.T.astype(BF16)
                    vt_ref[0, hd, kb, LANES:VT_ROWS, :] = ones_row


def _proj(h, gain, w_in, layer, col_scale, cos, sa, sb, batch, qb, tm=512):
    m, d = h.shape
    seq = m // batch
    nt = seq // tm
    row = lambda i: (i, 0)
    tab = lambda i: (i % nt, 0)
    res = lambda i: (i // nt, 0, i % nt, 0)
    return pl.pallas_call(
        functools.partial(_proj_kernel, qb=qb),
        grid=(m // tm,),
        in_specs=[pl.BlockSpec((tm, d), row),
                  pl.BlockSpec((1, d), lambda i: (0, 0)),
                  pl.BlockSpec((1, d, IN_COLS), lambda i: (layer, 0, 0), pipeline_mode=pl.Buffered(1)),
                  pl.BlockSpec((1, IN_COLS), lambda i: (0, 0)),
                  pl.BlockSpec((tm, LANES), tab),
                  pl.BlockSpec((tm, LANES), tab),
                  pl.BlockSpec((tm, LANES), tab)],
        out_specs=[pl.BlockSpec((tm, IN_COLS), row),
                   pl.BlockSpec((1, 4, tm // 4, 3 * WIDTH), res),
                   pl.BlockSpec((1, 16, tm // 16, 3 * WIDTH), res),
                   pl.BlockSpec((1, N_HEADS_DIFF, tm // qb, VT_ROWS, qb), lambda i: (i // nt, 0, i % nt, 0, 0)),
                   pl.BlockSpec((1, N_HEADS_DIFF, tm // qb, LANES, qb), lambda i: (i // nt, 0, i % nt, 0, 0))],
        out_shape=[jax.ShapeDtypeStruct((m, IN_COLS), BF16),
                   jax.ShapeDtypeStruct((batch, 4, seq // 4, 3 * WIDTH), BF16),
                   jax.ShapeDtypeStruct((batch, 16, seq // 16, 3 * WIDTH), BF16),
                   jax.ShapeDtypeStruct((batch, N_HEADS_DIFF, seq // qb, VT_ROWS, qb), BF16),
                   jax.ShapeDtypeStruct((batch, N_HEADS_DIFF, seq // qb, LANES, qb), BF16)],
        scratch_shapes=[pltpu.VMEM((WIDTH // LANES, tm, LANES), F32),
                        pltpu.VMEM((WIDTH // LANES, tm, LANES), F32),
                        pltpu.VMEM((d, IN_COLS), BF16)],
        compiler_params=pltpu.CompilerParams(dimension_semantics=("arbitrary",),
                                             vmem_limit_bytes=PROJ_VMEM_LIMIT),
        name="proj",
    )(h, gain.reshape(1, d), w_in, col_scale.reshape(1, IN_COLS), cos, sa, sb)


def _rope_tables(seq):
    half = ROPE_DIM // 2
    inv = jnp.power(ROPE_THETA, -jnp.arange(half, dtype=F32) * (2.0 / ROPE_DIM))
    ang = jnp.arange(seq).astype(F32)[:, None] * inv[None, :]
    cos, sin = jnp.cos(ang), jnp.sin(ang)
    ch = jnp.arange(LANES) % HEAD_DIM
    cos_l = jnp.take(cos, ch % half, axis=1)
    sin_l = jnp.take(sin, ch % half, axis=1)
    cos_t = jnp.where(ch < ROPE_DIM, cos_l, 1.0)
    sa = jnp.where(ch < half, -sin_l, 0.0)
    sb = jnp.where((ch >= half) & (ch < ROPE_DIM), sin_l, 0.0)
    return cos_t, sa, sb


def _dswa_kernel(q_ref, kc_ref, kp_ref, vc_ref, vp_ref, o_ref, l_ref, vt_sc, bias_sc, k_sc, sa_sc, sb_sc, *,
                 lc):
    w = DSWA_W
    nblk = lc // w
    first_chunk = pl.program_id(2) == 0
    head_a = lax.broadcasted_iota(jnp.int32, (1, LANES), 1) < HEAD_DIM
    kj = lax.broadcasted_iota(jnp.int32, (2 * w, 2 * w), 0)
    qi = lax.broadcasted_iota(jnp.int32, (2 * w, 2 * w), 1) & (w - 1)
    dist = w + qi - kj
    band = (dist >= 0) & (dist <= w)
    bias_sc[0] = jnp.where(band, 0.0, NEG)
    bias_sc[1] = jnp.where(band & ((kj >= w) | jnp.logical_not(first_chunk)), 0.0, NEG)

    def transposed(v):
        return v.T

    ones_row = (lax.broadcasted_iota(jnp.int32, (VT_ROWS - LANES, w), 0) == 0).astype(BF16)
    for hp in range(WIDTH // LANES):
        lanes = slice(hp * LANES, (hp + 1) * LANES)
        vt_sc[0, hp, 0:LANES, :] = transposed(vp_ref[0, 0, :, lanes])
        vt_sc[0, hp, LANES:VT_ROWS, :] = ones_row
        for n in range(nblk):
            vt_sc[n + 1, hp, 0:LANES, :] = transposed(vc_ref[0, 0, n * w:(n + 1) * w, lanes])
            vt_sc[n + 1, hp, LANES:VT_ROWS, :] = ones_row

    k_sc[0:w, :] = kp_ref[0, 0]
    k_sc[w:w + lc, :] = kc_ref[0, 0]

    def scores(n, hp):
        lanes = slice(hp * LANES, (hp + 1) * LANES)
        row0 = pl.multiple_of(n * w, w)
        qb = q_ref[0, 0, pl.ds(row0, w), lanes]
        zero = jnp.zeros_like(qb)
        qs = jnp.concatenate([jnp.where(head_a, qb, zero), jnp.where(head_a, zero, qb)], axis=0)
        kk = k_sc[pl.ds(row0, 2 * w), lanes]
        return lax.dot_general(kk, qs, (((1,), (1,)), ((), ())), preferred_element_type=F32)

    def finish(n, hp, raw):
        lanes = slice(hp * LANES, (hp + 1) * LANES)
        row0 = pl.multiple_of(n * w, w)
        bias = bias_sc[jnp.where(n == 0, 1, 0)]
        ms, ps = [], []
        for half in (slice(0, w), slice(w, 2 * w)):
            s = raw[:, half] + bias[:, half]
            m_h = jnp.max(s, axis=0, keepdims=True)
            ms.append(m_h)
            ps.append(jnp.exp2(s - m_h).astype(BF16))
        m = jnp.concatenate(ms, axis=1)
        p = jnp.concatenate(ps, axis=1)
        vv = jnp.concatenate([vt_sc[n, hp], vt_sc[n + 1, hp]], axis=1)
        pv = jnp.dot(vv, p, preferred_element_type=F32)
        den = pv[LANES:LANES + 1, :]
        o = pv[0:LANES, :] * (1.0 / den)
        lse = m + jnp.log2(den)
        o_sel = jnp.concatenate([o[:HEAD_DIM, :w], o[HEAD_DIM:, w:]], axis=0)
        l_sel = jnp.concatenate([jnp.broadcast_to(lse[:, :w], (HEAD_DIM, w)),
                                 jnp.broadcast_to(lse[:, w:], (HEAD_DIM, w))], axis=0)
        o_ref[0, 0, pl.ds(row0, w), lanes] = o_sel.T.astype(o_ref.dtype)
        l_ref[0, 0, pl.ds(row0, w), lanes] = l_sel.T

    nhp = WIDTH // LANES

    def issue(n, s_ref):
        for hp in range(nhp):
            s_ref[hp] = scores(n, hp)

    def consume(n, s_ref):
        for hp in range(nhp):
            finish(n, hp, s_ref[hp])

    issue(0, sa_sc)

    def group(j, carry):
        n0 = DSWA_UNROLL * j
        for i in range(DSWA_UNROLL):
            cur, nxt = (sa_sc, sb_sc) if i % 2 == 0 else (sb_sc, sa_sc)
            issue(jnp.minimum(n0 + i + 1, nblk - 1), nxt)
            consume(n0 + i, cur)
        return carry

    lax.fori_loop(0, nblk // DSWA_UNROLL, group, 0)


def _dswa_group(qkv, splits):
    b, d, sub, _ = qkv.shape
    lc = min(sub, 1024)
    per = lc // DSWA_W
    assert per % DSWA_UNROLL == 0, "the kernel walks query blocks in groups of DSWA_UNROLL"
    sq, sk, sv = splits

    def cur(split):
        return pl.BlockSpec((1, 1, lc, WIDTH), lambda bi, r, c: (bi, r, c, split))

    def prv(split):
        return pl.BlockSpec((1, 1, DSWA_W, WIDTH), lambda bi, r, c: (bi, r, jnp.maximum(c * per - 1, 0), split))

    out = pl.BlockSpec((1, 1, lc, WIDTH), lambda bi, r, c: (bi, r, c, 0))
    return pl.pallas_call(
        functools.partial(_dswa_kernel, lc=lc),
        grid=(b, d, sub // lc),
        in_specs=[cur(sq), cur(sk), prv(sk), cur(sv), prv(sv)],
        out_specs=[out, out],
        out_shape=[jax.ShapeDtypeStruct((b, d, sub, WIDTH), BF16),
                   jax.ShapeDtypeStruct((b, d, sub, WIDTH), F32)],
        scratch_shapes=[pltpu.VMEM((per + 1, WIDTH // LANES, VT_ROWS, DSWA_W), BF16),
                        pltpu.VMEM((2, 2 * DSWA_W, 2 * DSWA_W), F32),
                        pltpu.VMEM((lc + DSWA_W, WIDTH), BF16),
                        pltpu.VMEM((WIDTH // LANES, 2 * DSWA_W, 2 * DSWA_W), F32),
                        pltpu.VMEM((WIDTH // LANES, 2 * DSWA_W, 2 * DSWA_W), F32)],
        compiler_params=_params("arbitrary", "arbitrary", "arbitrary"),
        name=f"dswa_d{d}",
    )(qkv, qkv, qkv, qkv, qkv)


def _diff_kernel(lam_ref, gain_ref, qt_ref, k_ref, vt_ref, o_ref, m_sc, acc_sc, sa_sc, sb_sc, *, qb,
                 lambda_init):
    nh = qt_ref.shape[1]
    heads = range(nh)
    comp_a = lax.broadcasted_iota(jnp.int32, (LANES, 1), 0) < HEAD_DIM
    key = lax.broadcasted_iota(jnp.int32, (qb, 2 * qb), 0)
    qry = lax.broadcasted_iota(jnp.int32, (qb, 2 * qb), 1) & (qb - 1)
    lam_p = lam_ref[...]
    lam = (jnp.exp(jnp.sum(lam_p[0:1] * lam_p[1:2], axis=1, keepdims=True))
           - jnp.exp(jnp.sum(lam_p[2:3] * lam_p[3:4], axis=1, keepdims=True)) + lambda_init)

    def query_block(qi, carry):
        q_rows = pl.ds(pl.multiple_of(qi * qb, qb), qb)

        def stacked_queries(h):
            qt = qt_ref[0, h, qi]
            zero = jnp.zeros_like(qt)
            return jnp.concatenate([jnp.where(comp_a, qt, zero), jnp.where(comp_a, zero, qt)], axis=1)

        qs = [stacked_queries(h) for h in heads]
        m_sc[...] = jnp.full(m_sc.shape, NEG, F32)
        acc_sc[...] = jnp.zeros(acc_sc.shape, F32)

        def issue(blk, s_ref):
            rows = pl.ds(pl.multiple_of(blk * qb, qb), qb)
            for h in heads:
                kb = k_ref[0, rows, h * LANES:(h + 1) * LANES]
                s_ref[h] = jnp.dot(kb, qs[h], preferred_element_type=F32)

        def consume(blk, s_ref, mask):
            for h in heads:
                s = s_ref[h]
                if mask is not None:
                    s = jnp.where(mask, s, NEG)
                m_old = m_sc[h]
                m_new = jnp.maximum(m_old, jnp.max(s, axis=0, keepdims=True))
                alpha = jnp.exp2(m_old - m_new)
                p = jnp.exp2(s - m_new).astype(BF16)
                acc_sc[h] = alpha * acc_sc[h] + jnp.dot(vt_ref[0, h, blk], p, preferred_element_type=F32)
                m_sc[h] = m_new

        last = jnp.maximum(qi - 1, 0)
        issue(qi, sa_sc)
        issue(0, sb_sc)
        consume(qi, sa_sc, key <= qry)

        def pair(j, c):
            b0 = 2 * j
            issue(b0 + 1, sa_sc)
            consume(b0, sb_sc, None)
            issue(jnp.minimum(b0 + 2, last), sb_sc)
            consume(b0 + 1, sa_sc, None)
            return c

        lax.fori_loop(0, qi // 2, pair, 0)

        @pl.when(qi % 2 == 1)
        def _():
            consume(qi - 1, sb_sc, None)

        for h in heads:
            o_all = acc_sc[h, 0:LANES, :] / acc_sc[h, LANES:LANES + 1, :]
            o = o_all[:, :qb] - lam * o_all[:, qb:]
            o = o * lax.rsqrt(jnp.mean(o * o, axis=0, keepdims=True) + SUBLN_EPS) * gain_ref[...]
            o_ref[0, q_rows, h * LANES:(h + 1) * LANES] = (o * (1.0 - lambda_init)).T.astype(o_ref.dtype)
        return carry

    lax.fori_loop(0, qt_ref.shape[2], query_block, 0)


def _diff(u3, qt, vt, lam_p, subln_gain, lambda_init, qb, nh=2):
    b, s, _ = u3.shape
    per = WIDTH // (nh * LANES)
    nb = s // qb
    seq = lambda split: pl.BlockSpec((1, s, nh * LANES), lambda bi, g: (bi, 0, split * per + g))
    return pl.pallas_call(
        functools.partial(_diff_kernel, qb=qb, lambda_init=lambda_init),
        grid=(b, N_HEADS_DIFF // nh),
        in_specs=[pl.BlockSpec(lam_p.shape, lambda bi, g: (0, 0)),
                  pl.BlockSpec((LANES, 1), lambda bi, g: (0, 0)),
                  pl.BlockSpec((1, nh, nb, LANES, qb), lambda bi, g: (bi, g, 0, 0, 0)),
                  seq(COL_KB),
                  pl.BlockSpec((1, nh, nb, VT_ROWS, qb), lambda bi, g: (bi, g, 0, 0, 0))],
        out_specs=pl.BlockSpec((1, s, nh * LANES), lambda bi, g: (bi, 0, g)),
        out_shape=jax.ShapeDtypeStruct((b, s, WIDTH), BF16),
        scratch_shapes=[pltpu.VMEM((nh, 1, 2 * qb), F32),
                        pltpu.VMEM((nh, VT_ROWS, 2 * qb), F32),
                        pltpu.VMEM((nh, qb, 2 * qb), F32), pltpu.VMEM((nh, qb, 2 * qb), F32)],
        compiler_params=_params("arbitrary", "arbitrary"),
        name="diff",
    )(lam_p, subln_gain.reshape(LANES, 1), qt, u3, vt)


def _silu_of_half(half):
    return half + half * jnp.tanh(half)


def _post_kernel(h_ref, ga_ref, gb_ref, o1_ref, l1_ref, o4_ref, l4_ref, o16_ref, l16_ref, ob_ref, p_ref,
                 wo32_ref, ng_ref, wg32_ref, wp32_ref, fg_ref, out_ref, on_sc, ln_sc, o4_sc, l4_sc,
                 wo_ref, wg_ref, wp_ref, *, final):
    tm = h_ref.shape[0]

    @pl.when(pl.program_id(0) == 0)
    def _():
        wo_ref[...] = wo32_ref[0].astype(BF16)
        wg_ref[...] = (0.5 * wg32_ref[0]).astype(BF16)
        wp_ref[...] = (0.5 * wp32_ref[0]).astype(BF16)

    hr = tm // POST_PARTS
    n4, n16 = hr // 4, hr // 16

    def merge(part):
        rows, rows4, rows16 = (pl.ds(part * n, n) for n in (hr, n4, n16))
        merged = []
        for j in range(WIDTH // LANES):
            lanes = slice(j * LANES, (j + 1) * LANES)
            for r4 in range(4):
                for k in range(4):
                    o4_sc[pl.ds(k, n16, stride=4), :] = o16_ref[0, 4 * k + r4, rows16, lanes].astype(F32)
                    l4_sc[pl.ds(k, n16, stride=4), :] = l16_ref[0, 4 * k + r4, rows16, lanes]
                o16, l16 = o4_sc[...], l4_sc[...]
                o4, l4 = o4_ref[0, r4, rows4, lanes].astype(F32), l4_ref[0, r4, rows4, lanes]
                top = jnp.maximum(l4, l16)
                w4, w16 = jnp.exp2(l4 - top), jnp.exp2(l16 - top)
                den = w4 + w16
                on_sc[pl.ds(r4, n4, stride=4), :] = (w4 * o4 + w16 * o16) / den
                ln_sc[pl.ds(r4, n4, stride=4), :] = top + jnp.log2(den)
            o1, l1 = o1_ref[rows, lanes].astype(F32), l1_ref[rows, lanes]
            ox, lx = on_sc[...], ln_sc[...]
            top = jnp.maximum(l1, lx)
            w1, wx = jnp.exp2(l1 - top), jnp.exp2(lx - top)
            merged.append((w1 * o1 + wx * ox) / (w1 + wx))
        return jnp.concatenate(merged, axis=1)

    def mix(part, oa):
        rows = pl.ds(part * hr, hr)
        ya = (oa * _silu_of_half(ga_ref[rows, :].astype(F32))).astype(BF16)
        yb = (ob_ref[rows, :].astype(F32) * _silu_of_half(gb_ref[rows, :].astype(F32))).astype(BF16)
        y = (jnp.dot(ya, wo_ref[0:WIDTH, :], preferred_element_type=F32)
             + jnp.dot(yb, wo_ref[WIDTH:2 * WIDTH, :], preferred_element_type=F32))
        return h_ref[rows, :] + y

    def embed(part, h1):
        rows = pl.ds(part * hr, hr)
        n = h1 * lax.rsqrt(jnp.mean(h1 * h1, axis=-1, keepdims=True) + RMS_EPS) * ng_ref[...]
        t = jnp.tanh(jnp.dot(n.astype(BF16), wg_ref[...], preferred_element_type=F32))
        ple_half = jnp.dot(p_ref[0, rows, :].astype(BF16), wp_ref[...], preferred_element_type=F32)
        h2 = h1 + ple_half + ple_half * t
        if final:
            h2 = h2 * lax.rsqrt(jnp.mean(h2 * h2, axis=-1, keepdims=True) + RMS_EPS) * fg_ref[...]
        out_ref[rows, :] = h2

    h1 = [mix(part, merge(part)) for part in range(POST_PARTS)]
    for part in range(POST_PARTS):
        embed(part, h1[part])


def _post(h, u, g1, g4, g16, ob, p, w_out, ple_gain, w_gate, w_ple, layer, final_gain, final, batch, tm=512):
    m, d = h.shape
    whole = lambda w: pl.BlockSpec((1,) + w.shape[1:], lambda i: (layer, 0, 0), pipeline_mode=pl.Buffered(1))
    nt = m // batch // tm
    row = lambda i: (i, 0)
    const = lambda i: (0, 0)
    res = lambda i: (i // nt, 0, i % nt, 0)
    grp = lambda dil: pl.BlockSpec((1, dil, tm // dil, WIDTH), res)
    return pl.pallas_call(
        functools.partial(_post_kernel, final=final),
        grid=(m // tm,),
        in_specs=[pl.BlockSpec((tm, d), row),
                  pl.BlockSpec((tm, WIDTH), lambda i: (i, COL_GA)),
                  pl.BlockSpec((tm, WIDTH), lambda i: (i, COL_GB)),
                  pl.BlockSpec((tm, WIDTH), row), pl.BlockSpec((tm, WIDTH), row),
                  grp(4), grp(4), grp(16), grp(16),
                  pl.BlockSpec((tm, WIDTH), row),
                  pl.BlockSpec((1, tm, p.shape[2]), lambda i: (layer, i, 0)),
                  whole(w_out),
                  pl.BlockSpec((1, d), const),
                  whole(w_gate),
                  whole(w_ple),
                  pl.BlockSpec((1, d), const)],
        out_specs=pl.BlockSpec((tm, d), row),
        out_shape=jax.ShapeDtypeStruct((m, d), F32),
        scratch_shapes=[pltpu.VMEM((tm // POST_PARTS, LANES), F32), pltpu.VMEM((tm // POST_PARTS, LANES), F32),
                        pltpu.VMEM((tm // POST_PARTS // 4, LANES), F32),
                        pltpu.VMEM((tm // POST_PARTS // 4, LANES), F32),
                        pltpu.VMEM(w_out.shape[1:], BF16), pltpu.VMEM(w_gate.shape[1:], BF16),
                        pltpu.VMEM(w_ple.shape[1:], BF16)],
        compiler_params=_params("arbitrary"),
        name="post",
    )(h, u, u, g1[0].reshape(m, WIDTH), g1[1].reshape(m, WIDTH), g4[0], g4[1], g16[0], g16[1], ob, p,
      w_out, ple_gain.reshape(1, d), w_gate, w_ple, final_gain.reshape(1, d))


def kernel(x, p, attn_norm_gain, w_in, w_out, lambda_q1, lambda_k1, lambda_q2, lambda_k2, subln_gain,
           ple_norm_gain, w_ple_gate, w_ple, final_norm_gain):
    b, s, d = x.shape
    depth = w_in.shape[0]
    h = x.reshape(b * s, d)
    cos, sa, sb = _rope_tables(s)
    col = jnp.arange(IN_COLS) // WIDTH
    col_scale = jnp.where((col == COL_QA) | (col == COL_QB), HEAD_DIM ** -0.5 * LOG2_E,
                          jnp.where((col == COL_GA) | (col == COL_GB), 0.5, 1.0)).astype(F32)
    a_splits = (COL_QA, COL_KA, COL_VA)
    for i in range(depth):
        lambda_init = 0.8 - 0.6 * math.exp(-0.3 * i)
        u, qkv4, qkv16, vt, qt = _proj(h, attn_norm_gain[i], w_in, i, col_scale, cos, sa, sb, b, DIFF_QB)
        u3 = u.reshape(b, s, IN_COLS)
        g1 = _dswa_group(u3.reshape(b, 1, s, IN_COLS), a_splits)
        g4 = _dswa_group(qkv4, a_splits)
        g16 = _dswa_group(qkv16, a_splits)
        lam_p = jnp.stack([lambda_q1[i], lambda_k1[i], lambda_q2[i], lambda_k2[i]]).astype(F32)
        ob = _diff(u3, qt, vt, lam_p, subln_gain[i], lambda_init, DIFF_QB)
        h = _post(h, u, g1, g4, g16, ob.reshape(b * s, WIDTH), p.reshape(depth, b * s, -1),
                  w_out, ple_norm_gain[i], w_ple_gate, w_ple, i, final_norm_gain, final=i == depth - 1, batch=b)
    return h.reshape(b, s, d)
```

```python
import functools
import math

import jax
import jax.numpy as jnp
from jax import lax
from jax.experimental import pallas as pl
from jax.experimental.pallas import tpu as pltpu

HEAD_DIM = 64
LANES = 128
DSWA_W = 128
DSWA_UNROLL = 8
N_HEADS_DIFF = 4
DIFF_QB = 512
POST_PARTS = 2
VT_ROWS = LANES + 16
WIDTH = 512
IN_COLS = 8 * WIDTH
ROPE_THETA = 500000.0
ROPE_DIM = HEAD_DIM // 4
RMS_EPS = 1e-6
SUBLN_EPS = 1e-5
NEG = -1e30
LOG2_E = math.log2(math.e)
VMEM_LIMIT = 48 * 1024 * 1024
PROJ_VMEM_LIMIT = 58 * 1024 * 1024

COL_QA, COL_KA, COL_VA, COL_GA, COL_QB, COL_KB, COL_VB, COL_GB = range(8)
ROPE_SPLITS = (COL_QA, COL_KA, COL_QB, COL_KB)

BF16 = jnp.bfloat16
F32 = jnp.float32


def _params(*sem):
    return pltpu.CompilerParams(dimension_semantics=sem, vmem_limit_bytes=VMEM_LIMIT)


def _proj_kernel(x_ref, g_ref, w32_ref, ws_ref, cos_ref, sa_ref, sb_ref, o_ref, d4_ref, d16_ref, vt_ref, qt_ref,
                 buf_sc, buf4_sc, w_ref, *, qb):
    tm = x_ref.shape[0]

    @pl.when(pl.program_id(0) == 0)
    def _():
        for c in range(IN_COLS // WIDTH):
            cols = slice(c * WIDTH, (c + 1) * WIDTH)
            w_ref[:, cols] = (w32_ref[0, :, cols] * ws_ref[:, cols]).astype(BF16)

    x = x_ref[...]
    ms = jnp.mean(x * x, axis=-1, keepdims=True)
    xn = (x * lax.rsqrt(ms + RMS_EPS) * g_ref[...]).astype(BF16)
    cos, sa, sb = cos_ref[...], sa_ref[...], sb_ref[...]
    for c in range(IN_COLS // WIDTH):
        cols = slice(c * WIDTH, (c + 1) * WIDTH)
        acc = jnp.dot(xn, w_ref[:, cols], preferred_element_type=F32)
        if c in ROPE_SPLITS:
            blocks = []
            for j in range(WIDTH // LANES):
                blk = acc[:, j * LANES:(j + 1) * LANES]
                blocks.append(blk * cos + pltpu.roll(blk, LANES - ROPE_DIM // 2, 1) * sa
                              + pltpu.roll(blk, ROPE_DIM // 2, 1) * sb)
            acc = jnp.concatenate(blocks, axis=1)
        o_ref[:, cols] = acc.astype(BF16)
        if c in (COL_QA, COL_KA, COL_VA):
            n4, n16 = tm // 4, tm // 16
            for j in range(WIDTH // LANES):
                lanes = slice(c * WIDTH + j * LANES, c * WIDTH + (j + 1) * LANES)
                buf_sc[j] = acc[:, j * LANES:(j + 1) * LANES]
                for r4 in range(4):
                    rows4 = buf_sc[j, pl.ds(r4, n4, stride=4), :]
                    d4_ref[0, r4, :, lanes] = rows4.astype(BF16)
                    buf4_sc[j, r4 * n4:(r4 + 1) * n4, :] = rows4
                for r4 in range(4):
                    for k in range(4):
                        rows16 = buf4_sc[j, pl.ds(r4 * n4 + k, n16, stride=4), :]
                        d16_ref[0, 4 * k + r4, :, lanes] = rows16.astype(BF16)
        if c == COL_QB:
            for kb in range(tm // qb):
                for hd in range(N_HEADS_DIFF):
                    blk = acc[kb * qb:(kb + 1) * qb, hd * LANES:(hd + 1) * LANES]
                    qt_ref[0, hd, kb] = blk.T.astype(BF16)
        if c == COL_VB:
            ones_row = (lax.broadcasted_iota(jnp.int32, (VT_ROWS - LANES, qb), 0) == 0).astype(BF16)
            for kb in range(tm // qb):
                for hd in range(N_HEADS_DIFF):
                    blk = acc[kb * qb:(kb + 1) * qb, hd * LANES:(hd + 1) * LANES]
                    vt_ref[0, hd, kb, 0:LANES, :] = blk.T.astype(BF16)
                    vt_ref[0, hd, kb, LANES:VT_ROWS, :] = ones_row


def _proj(h, gain, w_in, layer, col_scale, cos, sa, sb, batch, qb, tm=512):
    m, d = h.shape
    seq = m // batch
    nt = seq // tm
    row = lambda i: (i, 0)
    tab = lambda i: (i % nt, 0)
    res = lambda i: (i // nt, 0, i % nt, 0)
    return pl.pallas_call(
        functools.partial(_proj_kernel, qb=qb),
        grid=(m // tm,),
        in_specs=[pl.BlockSpec((tm, d), row),
                  pl.BlockSpec((1, d), lambda i: (0, 0)),
                  pl.BlockSpec((1, d, IN_COLS), lambda i: (layer, 0, 0), pipeline_mode=pl.Buffered(1)),
                  pl.BlockSpec((1, IN_COLS), lambda i: (0, 0)),
                  pl.BlockSpec((tm, LANES), tab),
                  pl.BlockSpec((tm, LANES), tab),
                  pl.BlockSpec((tm, LANES), tab)],
        out_specs=[pl.BlockSpec((tm, IN_COLS), row),
                   pl.BlockSpec((1, 4, tm // 4, 3 * WIDTH), res),
                   pl.BlockSpec((1, 16, tm // 16, 3 * WIDTH), res),
                   pl.BlockSpec((1, N_HEADS_DIFF, tm // qb, VT_ROWS, qb), lambda i: (i // nt, 0, i % nt, 0, 0)),
                   pl.BlockSpec((1, N_HEADS_DIFF, tm // qb, LANES, qb), lambda i: (i // nt, 0, i % nt, 0, 0))],
        out_shape=[jax.ShapeDtypeStruct((m, IN_COLS), BF16),
                   jax.ShapeDtypeStruct((batch, 4, seq // 4, 3 * WIDTH), BF16),
                   jax.ShapeDtypeStruct((batch, 16, seq // 16, 3 * WIDTH), BF16),
                   jax.ShapeDtypeStruct((batch, N_HEADS_DIFF, seq // qb, VT_ROWS, qb), BF16),
                   jax.ShapeDtypeStruct((batch, N_HEADS_DIFF, seq // qb, LANES, qb), BF16)],
        scratch_shapes=[pltpu.VMEM((WIDTH // LANES, tm, LANES), F32),
                        pltpu.VMEM((WIDTH // LANES, tm, LANES), F32),
                        pltpu.VMEM((d, IN_COLS), BF16)],
        compiler_params=pltpu.CompilerParams(dimension_semantics=("arbitrary",),
                                             vmem_limit_bytes=PROJ_VMEM_LIMIT),
        name="proj",
    )(h, gain.reshape(1, d), w_in, col_scale.reshape(1, IN_COLS), cos, sa, sb)


def _rope_tables(seq):
    half = ROPE_DIM // 2
    inv = jnp.power(ROPE_THETA, -jnp.arange(half, dtype=F32) * (2.0 / ROPE_DIM))
    ang = jnp.arange(seq).astype(F32)[:, None] * inv[None, :]
    cos, sin = jnp.cos(ang), jnp.sin(ang)
    ch = jnp.arange(LANES) % HEAD_DIM
    cos_l = jnp.take(cos, ch % half, axis=1)
    sin_l = jnp.take(sin, ch % half, axis=1)
    cos_t = jnp.where(ch < ROPE_DIM, cos_l, 1.0)
    sa = jnp.where(ch < half, -sin_l, 0.0)
    sb = jnp.where((ch >= half) & (ch < ROPE_DIM), sin_l, 0.0)
    return cos_t, sa, sb


def _dswa_kernel(q_ref, kc_ref, kp_ref, vc_ref, vp_ref, o_ref, l_ref, vt_sc, bias_sc, k_sc, sa_sc, sb_sc, *,
                 lc):
    w = DSWA_W
    nblk = lc // w
    first_chunk = pl.program_id(2) == 0
    head_a = lax.broadcasted_iota(jnp.int32, (1, LANES), 1) < HEAD_DIM
    kj = lax.broadcasted_iota(jnp.int32, (2 * w, 2 * w), 0)
    qi = lax.broadcasted_iota(jnp.int32, (2 * w, 2 * w), 1) & (w - 1)
    dist = w + qi - kj
    band = (dist >= 0) & (dist <= w)
    bias_sc[0] = jnp.where(band, 0.0, NEG)
    bias_sc[1] = jnp.where(band & ((kj >= w) | jnp.logical_not(first_chunk)), 0.0, NEG)

    def transposed(v):
        return v.T

    ones_row = (lax.broadcasted_iota(jnp.int32, (VT_ROWS - LANES, w), 0) == 0).astype(BF16)
    for hp in range(WIDTH // LANES):
        lanes = slice(hp * LANES, (hp + 1) * LANES)
        vt_sc[0, hp, 0:LANES, :] = transposed(vp_ref[0, 0, :, lanes])
        vt_sc[0, hp, LANES:VT_ROWS, :] = ones_row
        for n in range(nblk):
            vt_sc[n + 1, hp, 0:LANES, :] = transposed(vc_ref[0, 0, n * w:(n + 1) * w, lanes])
            vt_sc[n + 1, hp, LANES:VT_ROWS, :] = ones_row

    k_sc[0:w, :] = kp_ref[0, 0]
    k_sc[w:w + lc, :] = kc_ref[0, 0]

    def scores(n, hp):
        lanes = slice(hp * LANES, (hp + 1) * LANES)
        row0 = pl.multiple_of(n * w, w)
        qb = q_ref[0, 0, pl.ds(row0, w), lanes]
        zero = jnp.zeros_like(qb)
        qs = jnp.concatenate([jnp.where(head_a, qb, zero), jnp.where(head_a, zero, qb)], axis=0)
        kk = k_sc[pl.ds(row0, 2 * w), lanes]
        return lax.dot_general(kk, qs, (((1,), (1,)), ((), ())), preferred_element_type=F32)

    def finish(n, hp, raw):
        lanes = slice(hp * LANES, (hp + 1) * LANES)
        row0 = pl.multiple_of(n * w, w)
        bias = bias_sc[jnp.where(n == 0, 1, 0)]
        ms, ps = [], []
        for half in (slice(0, w), slice(w, 2 * w)):
            s = raw[:, half] + bias[:, half]
            m_h = jnp.max(s, axis=0, keepdims=True)
            ms.append(m_h)
            ps.append(jnp.exp2(s - m_h).astype(BF16))
        m = jnp.concatenate(ms, axis=1)
        p = jnp.concatenate(ps, axis=1)
        vv = jnp.concatenate([vt_sc[n, hp], vt_sc[n + 1, hp]], axis=1)
        pv = jnp.dot(vv, p, preferred_element_type=F32)
        den = pv[LANES:LANES + 1, :]
        o = pv[0:LANES, :] * (1.0 / den)
        lse = m + jnp.log2(den)
        o_sel = jnp.concatenate([o[:HEAD_DIM, :w], o[HEAD_DIM:, w:]], axis=0)
        l_sel = jnp.concatenate([jnp.broadcast_to(lse[:, :w], (HEAD_DIM, w)),
                                 jnp.broadcast_to(lse[:, w:], (HEAD_DIM, w))], axis=0)
        o_ref[0, 0, pl.ds(row0, w), lanes] = o_sel.T.astype(o_ref.dtype)
        l_ref[0, 0, pl.ds(row0, w), lanes] = l_sel.T

    nhp = WIDTH // LANES

    def issue(n, s_ref):
        for hp in range(nhp):
            s_ref[hp] = scores(n, hp)

    def consume(n, s_ref):
        for hp in range(nhp):
            finish(n, hp, s_ref[hp])

    issue(0, sa_sc)

    unroll = math.gcd(nblk, DSWA_UNROLL)

    def group(j, carry):
        n0 = unroll * j
        for i in range(unroll):
            cur, nxt = (sa_sc, sb_sc) if i % 2 == 0 else (sb_sc, sa_sc)
            issue(jnp.minimum(n0 + i + 1, nblk - 1), nxt)
            consume(n0 + i, cur)
        return carry

    lax.fori_loop(0, nblk // unroll, group, 0)


def _dswa_group(qkv, splits):
    b, d, sub, _ = qkv.shape
    lc = min(sub, 1024)
    per = lc // DSWA_W
    assert per % 2 == 0, "the kernel walks query blocks in even groups"
    sq, sk, sv = splits

    def cur(split):
        return pl.BlockSpec((1, 1, lc, WIDTH), lambda bi, r, c: (bi, r, c, split))

    def prv(split):
        return pl.BlockSpec((1, 1, DSWA_W, WIDTH), lambda bi, r, c: (bi, r, jnp.maximum(c * per - 1, 0), split))

    out = pl.BlockSpec((1, 1, lc, WIDTH), lambda bi, r, c: (bi, r, c, 0))
    return pl.pallas_call(
        functools.partial(_dswa_kernel, lc=lc),
        grid=(b, d, sub // lc),
        in_specs=[cur(sq), cur(sk), prv(sk), cur(sv), prv(sv)],
        out_specs=[out, out],
        out_shape=[jax.ShapeDtypeStruct((b, d, sub, WIDTH), BF16),
                   jax.ShapeDtypeStruct((b, d, sub, WIDTH), F32)],
        scratch_shapes=[pltpu.VMEM((per + 1, WIDTH // LANES, VT_ROWS, DSWA_W), BF16),
                        pltpu.VMEM((2, 2 * DSWA_W, 2 * DSWA_W), F32),
                        pltpu.VMEM((lc + DSWA_W, WIDTH), BF16),
                        pltpu.VMEM((WIDTH // LANES, 2 * DSWA_W, 2 * DSWA_W), F32),
                        pltpu.VMEM((WIDTH // LANES, 2 * DSWA_W, 2 * DSWA_W), F32)],
        compiler_params=_params("arbitrary", "arbitrary", "arbitrary"),
        name=f"dswa_d{d}",
    )(qkv, qkv, qkv, qkv, qkv)


def _diff_kernel(lam_ref, gain_ref, qt_ref, k_ref, vt_ref, o_ref, m_sc, acc_sc, sa_sc, sb_sc, *, qb,
                 lambda_init):
    nh = qt_ref.shape[1]
    heads = range(nh)
    comp_a = lax.broadcasted_iota(jnp.int32, (LANES, 1), 0) < HEAD_DIM
    key = lax.broadcasted_iota(jnp.int32, (qb, 2 * qb), 0)
    qry = lax.broadcasted_iota(jnp.int32, (qb, 2 * qb), 1) & (qb - 1)
    lam_p = lam_ref[...]
    lam = (jnp.exp(jnp.sum(lam_p[0:1] * lam_p[1:2], axis=1, keepdims=True))
           - jnp.exp(jnp.sum(lam_p[2:3] * lam_p[3:4], axis=1, keepdims=True)) + lambda_init)

    def query_block(qi, carry):
        q_rows = pl.ds(pl.multiple_of(qi * qb, qb), qb)

        def stacked_queries(h):
            qt = qt_ref[0, h, qi]
            zero = jnp.zeros_like(qt)
            return jnp.concatenate([jnp.where(comp_a, qt, zero), jnp.where(comp_a, zero, qt)], axis=1)

        qs = [stacked_queries(h) for h in heads]
        m_sc[...] = jnp.full(m_sc.shape, NEG, F32)
        acc_sc[...] = jnp.zeros(acc_sc.shape, F32)

        def issue(blk, s_ref):
            rows = pl.ds(pl.multiple_of(blk * qb, qb), qb)
            for h in heads:
                kb = k_ref[0, rows, h * LANES:(h + 1) * LANES]
                s_ref[h] = jnp.dot(kb, qs[h], preferred_element_type=F32)

        def consume(blk, s_ref, mask):
            for h in heads:
                s = s_ref[h]
                if mask is not None:
                    s = jnp.where(mask, s, NEG)
                m_old = m_sc[h]
                m_new = jnp.maximum(m_old, jnp.max(s, axis=0, keepdims=True))
                alpha = jnp.exp2(m_old - m_new)
                p = jnp.exp2(s - m_new).astype(BF16)
                acc_sc[h] = alpha * acc_sc[h] + jnp.dot(vt_ref[0, h, blk], p, preferred_element_type=F32)
                m_sc[h] = m_new

        last = jnp.maximum(qi - 1, 0)
        issue(qi, sa_sc)
        issue(0, sb_sc)
        consume(qi, sa_sc, key <= qry)

        def pair(j, c):
            b0 = 2 * j
            issue(b0 + 1, sa_sc)
            consume(b0, sb_sc, None)
            issue(jnp.minimum(b0 + 2, last), sb_sc)
            consume(b0 + 1, sa_sc, None)
            return c

        lax.fori_loop(0, qi // 2, pair, 0)

        @pl.when(qi % 2 == 1)
        def _():
            consume(qi - 1, sb_sc, None)

        for h in heads:
            o_all = acc_sc[h, 0:LANES, :] / acc_sc[h, LANES:LANES + 1, :]
            o = o_all[:, :qb] - lam * o_all[:, qb:]
            o = o * lax.rsqrt(jnp.mean(o * o, axis=0, keepdims=True) + SUBLN_EPS) * gain_ref[...]
            o_ref[0, q_rows, h * LANES:(h + 1) * LANES] = (o * (1.0 - lambda_init)).T.astype(o_ref.dtype)
        return carry

    lax.fori_loop(0, qt_ref.shape[2], query_block, 0)


def _diff(u3, qt, vt, lam_p, subln_gain, lambda_init, qb, nh=2):
    b, s, _ = u3.shape
    per = WIDTH // (nh * LANES)
    nb = s // qb
    seq = lambda split: pl.BlockSpec((1, s, nh * LANES), lambda bi, g: (bi, 0, split * per + g))
    return pl.pallas_call(
        functools.partial(_diff_kernel, qb=qb, lambda_init=lambda_init),
        grid=(b, N_HEADS_DIFF // nh),
        in_specs=[pl.BlockSpec(lam_p.shape, lambda bi, g: (0, 0)),
                  pl.BlockSpec((LANES, 1), lambda bi, g: (0, 0)),
                  pl.BlockSpec((1, nh, nb, LANES, qb), lambda bi, g: (bi, g, 0, 0, 0)),
                  seq(COL_KB),
                  pl.BlockSpec((1, nh, nb, VT_ROWS, qb), lambda bi, g: (bi, g, 0, 0, 0))],
        out_specs=pl.BlockSpec((1, s, nh * LANES), lambda bi, g: (bi, 0, g)),
        out_shape=jax.ShapeDtypeStruct((b, s, WIDTH), BF16),
        scratch_shapes=[pltpu.VMEM((nh, 1, 2 * qb), F32),
                        pltpu.VMEM((nh, VT_ROWS, 2 * qb), F32),
                        pltpu.VMEM((nh, qb, 2 * qb), F32), pltpu.VMEM((nh, qb, 2 * qb), F32)],
        compiler_params=_params("arbitrary", "arbitrary"),
        name="diff",
    )(lam_p, subln_gain.reshape(LANES, 1), qt, u3, vt)


def _silu_of_half(half):
    return half + half * jnp.tanh(half)


def _post_kernel(h_ref, ga_ref, gb_ref, o1_ref, l1_ref, o4_ref, l4_ref, o16_ref, l16_ref, ob_ref, p_ref,
                 wo32_ref, ng_ref, wg32_ref, wp32_ref, fg_ref, out_ref, on_sc, ln_sc, o4_sc, l4_sc,
                 wo_ref, wg_ref, wp_ref, *, final):
    tm = h_ref.shape[0]

    @pl.when(pl.program_id(0) == 0)
    def _():
        wo_ref[...] = wo32_ref[0].astype(BF16)
        wg_ref[...] = (0.5 * wg32_ref[0]).astype(BF16)
        wp_ref[...] = (0.5 * wp32_ref[0]).astype(BF16)

    hr = tm // POST_PARTS
    n4, n16 = hr // 4, hr // 16

    def merge(part):
        rows, rows4, rows16 = (pl.ds(part * n, n) for n in (hr, n4, n16))
        merged = []
        for j in range(WIDTH // LANES):
            lanes = slice(j * LANES, (j + 1) * LANES)
            for r4 in range(4):
                for k in range(4):
                    o4_sc[pl.ds(k, n16, stride=4), :] = o16_ref[0, 4 * k + r4, rows16, lanes].astype(F32)
                    l4_sc[pl.ds(k, n16, stride=4), :] = l16_ref[0, 4 * k + r4, rows16, lanes]
                o16, l16 = o4_sc[...], l4_sc[...]
                o4, l4 = o4_ref[0, r4, rows4, lanes].astype(F32), l4_ref[0, r4, rows4, lanes]
                top = jnp.maximum(l4, l16)
                w4, w16 = jnp.exp2(l4 - top), jnp.exp2(l16 - top)
                den = w4 + w16
                on_sc[pl.ds(r4, n4, stride=4), :] = (w4 * o4 + w16 * o16) / den
                ln_sc[pl.ds(r4, n4, stride=4), :] = top + jnp.log2(den)
            o1, l1 = o1_ref[rows, lanes].astype(F32), l1_ref[rows, lanes]
            ox, lx = on_sc[...], ln_sc[...]
            top = jnp.maximum(l1, lx)
            w1, wx = jnp.exp2(l1 - top), jnp.exp2(lx - top)
            merged.append((w1 * o1 + wx * ox) / (w1 + wx))
        return jnp.concatenate(merged, axis=1)

    def mix(part, oa):
        rows = pl.ds(part * hr, hr)
        ya = (oa * _silu_of_half(ga_ref[rows, :].astype(F32))).astype(BF16)
        yb = (ob_ref[rows, :].astype(F32) * _silu_of_half(gb_ref[rows, :].astype(F32))).astype(BF16)
        y = (jnp.dot(ya, wo_ref[0:WIDTH, :], preferred_element_type=F32)
             + jnp.dot(yb, wo_ref[WIDTH:2 * WIDTH, :], preferred_element_type=F32))
        return h_ref[rows, :] + y

    def embed(part, h1):
        rows = pl.ds(part * hr, hr)
        n = h1 * lax.rsqrt(jnp.mean(h1 * h1, axis=-1, keepdims=True) + RMS_EPS) * ng_ref[...]
        t = jnp.tanh(jnp.dot(n.astype(BF16), wg_ref[...], preferred_element_type=F32))
        ple_half = jnp.dot(p_ref[0, rows, :].astype(BF16), wp_ref[...], preferred_element_type=F32)
        h2 = h1 + ple_half + ple_half * t
        if final:
            h2 = h2 * lax.rsqrt(jnp.mean(h2 * h2, axis=-1, keepdims=True) + RMS_EPS) * fg_ref[...]
        out_ref[rows, :] = h2

    h1 = [mix(part, merge(part)) for part in range(POST_PARTS)]
    for part in range(POST_PARTS):
        embed(part, h1[part])


def _post(h, u, g1, g4, g16, ob, p, w_out, ple_gain, w_gate, w_ple, layer, final_gain, final, batch, tm=512):
    m, d = h.shape
    whole = lambda w: pl.BlockSpec((1,) + w.shape[1:], lambda i: (layer, 0, 0), pipeline_mode=pl.Buffered(1))
    nt = m // batch // tm
    row = lambda i: (i, 0)
    const = lambda i: (0, 0)
    res = lambda i: (i // nt, 0, i % nt, 0)
    grp = lambda dil: pl.BlockSpec((1, dil, tm // dil, WIDTH), res)
    return pl.pallas_call(
        functools.partial(_post_kernel, final=final),
        grid=(m // tm,),
        in_specs=[pl.BlockSpec((tm, d), row),
                  pl.BlockSpec((tm, WIDTH), lambda i: (i, COL_GA)),
                  pl.BlockSpec((tm, WIDTH), lambda i: (i, COL_GB)),
                  pl.BlockSpec((tm, WIDTH), row), pl.BlockSpec((tm, WIDTH), row),
                  grp(4), grp(4), grp(16), grp(16),
                  pl.BlockSpec((tm, WIDTH), row),
                  pl.BlockSpec((1, tm, p.shape[2]), lambda i: (layer, i, 0)),
                  whole(w_out),
                  pl.BlockSpec((1, d), const),
                  whole(w_gate),
                  whole(w_ple),
                  pl.BlockSpec((1, d), const)],
        out_specs=pl.BlockSpec((tm, d), row),
        out_shape=jax.ShapeDtypeStruct((m, d), F32),
        scratch_shapes=[pltpu.VMEM((tm // POST_PARTS, LANES), F32), pltpu.VMEM((tm // POST_PARTS, LANES), F32),
                        pltpu.VMEM((tm // POST_PARTS // 4, LANES), F32),
                        pltpu.VMEM((tm // POST_PARTS // 4, LANES), F32),
                        pltpu.VMEM(w_out.shape[1:], BF16), pltpu.VMEM(w_gate.shape[1:], BF16),
                        pltpu.VMEM(w_ple.shape[1:], BF16)],
        compiler_params=_params("arbitrary"),
        name="post",
    )(h, u, u, g1[0].reshape(m, WIDTH), g1[1].reshape(m, WIDTH), g4[0], g4[1], g16[0], g16[1], ob, p,
      w_out, ple_gain.reshape(1, d), w_gate, w_ple, final_gain.reshape(1, d))


def kernel(x, p, attn_norm_gain, w_in, w_out, lambda_q1, lambda_k1, lambda_q2, lambda_k2, subln_gain,
           ple_norm_gain, w_ple_gate, w_ple, final_norm_gain):
    b, s, d = x.shape
    depth = w_in.shape[0]
    h = x.reshape(b * s, d)
    cos, sa, sb = _rope_tables(s)
    col = jnp.arange(IN_COLS) // WIDTH
    col_scale = jnp.where((col == COL_QA) | (col == COL_QB), HEAD_DIM ** -0.5 * LOG2_E,
                          jnp.where((col == COL_GA) | (col == COL_GB), 0.5, 1.0)).astype(F32)
    a_splits = (COL_QA, COL_KA, COL_VA)
    for i in range(depth):
        lambda_init = 0.8 - 0.6 * math.exp(-0.3 * i)
        u, qkv4, qkv16, vt, qt = _proj(h, attn_norm_gain[i], w_in, i, col_scale, cos, sa, sb, b, DIFF_QB)
        u3 = u.reshape(b, s, IN_COLS)
        g1 = _dswa_group(u3.reshape(b, 1, s, IN_COLS), a_splits)
        g4 = _dswa_group(qkv4, a_splits)
        g16 = _dswa_group(qkv16, a_splits)
        lam_p = jnp.stack([lambda_q1[i], lambda_k1[i], lambda_q2[i], lambda_k2[i]]).astype(F32)
        ob = _diff(u3, qt, vt, lam_p, subln_gain[i], lambda_init, DIFF_QB)
        h = _post(h, u, g1, g4, g16, ob.reshape(b * s, WIDTH), p.reshape(depth, b * s, -1),
                  w_out, ple_norm_gain[i], w_ple_gate, w_ple, i, final_norm_gain, final=i == depth - 1, batch=b)
    return h.reshape(b, s, d)
```

```python
import functools
import math

import jax
import jax.numpy as jnp
from jax import lax
from jax.experimental import pallas as pl
from jax.experimental.pallas import tpu as pltpu

HEAD_DIM = 64
LANES = 128
DSWA_W = 128
DSWA_UNROLL = 8
N_HEADS_DIFF = 4
DIFF_QB = 512
POST_PARTS = 2
VT_ROWS = LANES + 16
WIDTH = 512
IN_COLS = 8 * WIDTH
ROPE_THETA = 500000.0
ROPE_DIM = HEAD_DIM // 4
RMS_EPS = 1e-6
SUBLN_EPS = 1e-5
NEG = -1e30
LOG2_E = math.log2(math.e)
VMEM_LIMIT = 48 * 1024 * 1024
PROJ_VMEM_LIMIT = 58 * 1024 * 1024

COL_QA, COL_KA, COL_VA, COL_GA, COL_QB, COL_KB, COL_VB, COL_GB = range(8)
ROPE_SPLITS = (COL_QA, COL_KA, COL_QB, COL_KB)

BF16 = jnp.bfloat16
F32 = jnp.float32


def _params(*sem):
    return pltpu.CompilerParams(dimension_semantics=sem, vmem_limit_bytes=VMEM_LIMIT)


def _proj_kernel(x_ref, g_ref, w32_ref, ws_ref, cos_ref, sa_ref, sb_ref, o_ref, d4_ref, d16_ref, vt_ref, qt_ref,
                 buf_sc, buf4_sc, w_ref, *, qb):
    tm = x_ref.shape[0]

    @pl.when(pl.program_id(0) == 0)
    def _():
        for c in range(IN_COLS // WIDTH):
            cols = slice(c * WIDTH, (c + 1) * WIDTH)
            w_ref[:, cols] = (w32_ref[0, :, cols] * ws_ref[:, cols]).astype(BF16)

    x = x_ref[...]
    ms = jnp.mean(x * x, axis=-1, keepdims=True)
    xn = (x * lax.rsqrt(ms + RMS_EPS) * g_ref[...]).astype(BF16)
    cos, sa, sb = cos_ref[...], sa_ref[...], sb_ref[...]
    for c in range(IN_COLS // WIDTH):
        cols = slice(c * WIDTH, (c + 1) * WIDTH)
        acc = jnp.dot(xn, w_ref[:, cols], preferred_element_type=F32)
        if c in ROPE_SPLITS:
            blocks = []
            for j in range(WIDTH // LANES):
                blk = acc[:, j * LANES:(j + 1) * LANES]
                blocks.append(blk * cos + pltpu.roll(blk, LANES - ROPE_DIM // 2, 1) * sa
                              + pltpu.roll(blk, ROPE_DIM // 2, 1) * sb)
            acc = jnp.concatenate(blocks, axis=1)
        o_ref[:, cols] = acc.astype(BF16)
        if c in (COL_QA, COL_KA, COL_VA):
            n4, n16 = tm // 4, tm // 16
            for j in range(WIDTH // LANES):
                lanes = slice(c * WIDTH + j * LANES, c * WIDTH + (j + 1) * LANES)
                buf_sc[j] = acc[:, j * LANES:(j + 1) * LANES]
                for r4 in range(4):
                    rows4 = buf_sc[j, pl.ds(r4, n4, stride=4), :]
                    d4_ref[0, r4, :, lanes] = rows4.astype(BF16)
                    buf4_sc[j, r4 * n4:(r4 + 1) * n4, :] = rows4
                for r4 in range(4):
                    for k in range(4):
                        rows16 = buf4_sc[j, pl.ds(r4 * n4 + k, n16, stride=4), :]
                        d16_ref[0, 4 * k + r4, :, lanes] = rows16.astype(BF16)
        if c == COL_QB:
            for kb in range(tm // qb):
                for hd in range(N_HEADS_DIFF):
                    blk = acc[kb * qb:(kb + 1) * qb, hd * LANES:(hd + 1) * LANES]
                    qt_ref[0, hd, kb] = blk.T.astype(BF16)
        if c == COL_VB:
            ones_row = (lax.broadcasted_iota(jnp.int32, (VT_ROWS - LANES, qb), 0) == 0).astype(BF16)
            for kb in range(tm // qb):
                for hd in range(N_HEADS_DIFF):
                    blk = acc[kb * qb:(kb + 1) * qb, hd * LANES:(hd + 1) * LANES]
                    vt_ref[0, hd, kb, 0:LANES, :] = blk.T.astype(BF16)
                    vt_ref[0, hd, kb, LANES:VT_ROWS, :] = ones_row


def _proj(h, gain, w_in, layer, col_scale, cos, sa, sb, batch, qb, tm=512):
    m, d = h.shape
    seq = m // batch
    nt = seq // tm
    row = lambda i: (i, 0)
    tab = lambda i: (i % nt, 0)
    res = lambda i: (i // nt, 0, i % nt, 0)
    return pl.pallas_call(
        functools.partial(_proj_kernel, qb=qb),
        grid=(m // tm,),
        in_specs=[pl.BlockSpec((tm, d), row),
                  pl.BlockSpec((1, d), lambda i: (0, 0)),
                  pl.BlockSpec((1, d, IN_COLS), lambda i: (layer, 0, 0), pipeline_mode=pl.Buffered(1)),
                  pl.BlockSpec((1, IN_COLS), lambda i: (0, 0)),
                  pl.BlockSpec((tm, LANES), tab),
                  pl.BlockSpec((tm, LANES), tab),
                  pl.BlockSpec((tm, LANES), tab)],
        out_specs=[pl.BlockSpec((tm, IN_COLS), row),
                   pl.BlockSpec((1, 4, tm // 4, 3 * WIDTH), res),
                   pl.BlockSpec((1, 16, tm // 16, 3 * WIDTH), res),
                   pl.BlockSpec((1, N_HEADS_DIFF, tm // qb, VT_ROWS, qb), lambda i: (i // nt, 0, i % nt, 0, 0)),
                   pl.BlockSpec((1, N_HEADS_DIFF, tm // qb, LANES, qb), lambda i: (i // nt, 0, i % nt, 0, 0))],
        out_shape=[jax.ShapeDtypeStruct((m, IN_COLS), BF16),
                   jax.ShapeDtypeStruct((batch, 4, seq // 4, 3 * WIDTH), BF16),
                   jax.ShapeDtypeStruct((batch, 16, seq // 16, 3 * WIDTH), BF16),
                   jax.ShapeDtypeStruct((batch, N_HEADS_DIFF, seq // qb, VT_ROWS, qb), BF16),
                   jax.ShapeDtypeStruct((batch, N_HEADS_DIFF, seq // qb, LANES, qb), BF16)],
        scratch_shapes=[pltpu.VMEM((WIDTH // LANES, tm, LANES), F32),
                        pltpu.VMEM((WIDTH // LANES, tm, LANES), F32),
                        pltpu.VMEM((d, IN_COLS), BF16)],
        compiler_params=pltpu.CompilerParams(dimension_semantics=("arbitrary",),
                                             vmem_limit_bytes=PROJ_VMEM_LIMIT),
        name="proj",
    )(h, gain.reshape(1, d), w_in, col_scale.reshape(1, IN_COLS), cos, sa, sb)


def _rope_tables(seq):
    half = ROPE_DIM // 2
    inv = jnp.power(ROPE_THETA, -jnp.arange(half, dtype=F32) * (2.0 / ROPE_DIM))
    ang = jnp.arange(seq).astype(F32)[:, None] * inv[None, :]
    cos, sin = jnp.cos(ang), jnp.sin(ang)
    ch = jnp.arange(LANES) % HEAD_DIM
    cos_l = jnp.take(cos, ch % half, axis=1)
    sin_l = jnp.take(sin, ch % half, axis=1)
    cos_t = jnp.where(ch < ROPE_DIM, cos_l, 1.0)
    sa = jnp.where(ch < half, -sin_l, 0.0)
    sb = jnp.where((ch >= half) & (ch < ROPE_DIM), sin_l, 0.0)
    return cos_t, sa, sb


def _dswa_kernel(q_ref, kc_ref, kp_ref, vc_ref, vp_ref, o_ref, l_ref, vt_sc, bias_sc, k_sc, sa_sc, sb_sc, *,
                 lc):
    w = DSWA_W
    nblk = lc // w
    nres = q_ref.shape[1]
    first_chunk = pl.program_id(2) == 0
    head_a = lax.broadcasted_iota(jnp.int32, (1, LANES), 1) < HEAD_DIM
    kj = lax.broadcasted_iota(jnp.int32, (2 * w, 2 * w), 0)
    qi = lax.broadcasted_iota(jnp.int32, (2 * w, 2 * w), 1) & (w - 1)
    dist = w + qi - kj
    band = (dist >= 0) & (dist <= w)
    bias_sc[0] = jnp.where(band, 0.0, NEG)
    bias_sc[1] = jnp.where(band & ((kj >= w) | jnp.logical_not(first_chunk)), 0.0, NEG)

    def transposed(v):
        return v.T

    ones_row = (lax.broadcasted_iota(jnp.int32, (VT_ROWS - LANES, w), 0) == 0).astype(BF16)
    for r in range(nres):
        for hp in range(WIDTH // LANES):
            lanes = slice(hp * LANES, (hp + 1) * LANES)
            vt_sc[r, 0, hp, 0:LANES, :] = transposed(vp_ref[0, r, :, lanes])
            vt_sc[r, 0, hp, LANES:VT_ROWS, :] = ones_row
            for n in range(nblk):
                vt_sc[r, n + 1, hp, 0:LANES, :] = transposed(vc_ref[0, r, n * w:(n + 1) * w, lanes])
                vt_sc[r, n + 1, hp, LANES:VT_ROWS, :] = ones_row
        k_sc[r, 0:w, :] = kp_ref[0, r]
        k_sc[r, w:w + lc, :] = kc_ref[0, r]

    def scores(t, hp):
        lanes = slice(hp * LANES, (hp + 1) * LANES)
        r, n = t // nblk, t % nblk
        row0 = pl.multiple_of(n * w, w)
        qb = q_ref[0, r, pl.ds(row0, w), lanes]
        zero = jnp.zeros_like(qb)
        qs = jnp.concatenate([jnp.where(head_a, qb, zero), jnp.where(head_a, zero, qb)], axis=0)
        kk = k_sc[r, pl.ds(row0, 2 * w), lanes]
        return lax.dot_general(kk, qs, (((1,), (1,)), ((), ())), preferred_element_type=F32)

    def finish(t, hp, raw):
        lanes = slice(hp * LANES, (hp + 1) * LANES)
        r, n = t // nblk, t % nblk
        row0 = pl.multiple_of(n * w, w)
        bias = bias_sc[jnp.where(n == 0, 1, 0)]
        ms, ps = [], []
        for half in (slice(0, w), slice(w, 2 * w)):
            s = raw[:, half] + bias[:, half]
            m_h = jnp.max(s, axis=0, keepdims=True)
            ms.append(m_h)
            ps.append(jnp.exp2(s - m_h).astype(BF16))
        m = jnp.concatenate(ms, axis=1)
        p = jnp.concatenate(ps, axis=1)
        vv = jnp.concatenate([vt_sc[r, n, hp], vt_sc[r, n + 1, hp]], axis=1)
        pv = jnp.dot(vv, p, preferred_element_type=F32)
        den = pv[LANES:LANES + 1, :]
        o = pv[0:LANES, :] * (1.0 / den)
        lse = m + jnp.log2(den)
        o_sel = jnp.concatenate([o[:HEAD_DIM, :w], o[HEAD_DIM:, w:]], axis=0)
        l_sel = jnp.concatenate([jnp.broadcast_to(lse[:, :w], (HEAD_DIM, w)),
                                 jnp.broadcast_to(lse[:, w:], (HEAD_DIM, w))], axis=0)
        o_ref[0, r, pl.ds(row0, w), lanes] = o_sel.T.astype(o_ref.dtype)
        l_ref[0, r, pl.ds(row0, w), lanes] = l_sel.T

    nhp = WIDTH // LANES

    def issue(n, s_ref):
        for hp in range(nhp):
            s_ref[hp] = scores(n, hp)

    def consume(n, s_ref):
        for hp in range(nhp):
            finish(n, hp, s_ref[hp])

    issue(0, sa_sc)

    total = nres * nblk
    unroll = math.gcd(total, DSWA_UNROLL)

    def group(j, carry):
        t0 = unroll * j
        for i in range(unroll):
            cur, nxt = (sa_sc, sb_sc) if i % 2 == 0 else (sb_sc, sa_sc)
            issue(jnp.minimum(t0 + i + 1, total - 1), nxt)
            consume(t0 + i, cur)
        return carry

    lax.fori_loop(0, total // unroll, group, 0)


def _dswa_group(qkv, splits):
    b, d, sub, _ = qkv.shape
    lc = min(sub, 1024)
    per = lc // DSWA_W
    nres = min(d, max(1, DSWA_UNROLL // per))
    assert (nres * per) % 2 == 0 and d % nres == 0, "the kernel walks query blocks in even groups"
    sq, sk, sv = splits

    def cur(split):
        return pl.BlockSpec((1, nres, lc, WIDTH), lambda bi, r, c: (bi, r, c, split))

    def prv(split):
        return pl.BlockSpec((1, nres, DSWA_W, WIDTH), lambda bi, r, c: (bi, r, jnp.maximum(c * per - 1, 0), split))

    out = pl.BlockSpec((1, nres, lc, WIDTH), lambda bi, r, c: (bi, r, c, 0))
    return pl.pallas_call(
        functools.partial(_dswa_kernel, lc=lc),
        grid=(b, d // nres, sub // lc),
        in_specs=[cur(sq), cur(sk), prv(sk), cur(sv), prv(sv)],
        out_specs=[out, out],
        out_shape=[jax.ShapeDtypeStruct((b, d, sub, WIDTH), BF16),
                   jax.ShapeDtypeStruct((b, d, sub, WIDTH), F32)],
        scratch_shapes=[pltpu.VMEM((nres, per + 1, WIDTH // LANES, VT_ROWS, DSWA_W), BF16),
                        pltpu.VMEM((2, 2 * DSWA_W, 2 * DSWA_W), F32),
                        pltpu.VMEM((nres, lc + DSWA_W, WIDTH), BF16),
                        pltpu.VMEM((WIDTH // LANES, 2 * DSWA_W, 2 * DSWA_W), F32),
                        pltpu.VMEM((WIDTH // LANES, 2 * DSWA_W, 2 * DSWA_W), F32)],
        compiler_params=_params("arbitrary", "arbitrary", "arbitrary"),
        name=f"dswa_d{d}",
    )(qkv, qkv, qkv, qkv, qkv)


def _diff_kernel(lam_ref, gain_ref, qt_ref, k_ref, vt_ref, o_ref, m_sc, acc_sc, sa_sc, sb_sc, *, qb,
                 lambda_init):
    nh = qt_ref.shape[1]
    heads = range(nh)
    comp_a = lax.broadcasted_iota(jnp.int32, (LANES, 1), 0) < HEAD_DIM
    key = lax.broadcasted_iota(jnp.int32, (qb, 2 * qb), 0)
    qry = lax.broadcasted_iota(jnp.int32, (qb, 2 * qb), 1) & (qb - 1)
    lam_p = lam_ref[...]
    lam = (jnp.exp(jnp.sum(lam_p[0:1] * lam_p[1:2], axis=1, keepdims=True))
           - jnp.exp(jnp.sum(lam_p[2:3] * lam_p[3:4], axis=1, keepdims=True)) + lambda_init)

    def query_block(qi, carry):
        q_rows = pl.ds(pl.multiple_of(qi * qb, qb), qb)

        def stacked_queries(h):
            qt = qt_ref[0, h, qi]
            zero = jnp.zeros_like(qt)
            return jnp.concatenate([jnp.where(comp_a, qt, zero), jnp.where(comp_a, zero, qt)], axis=1)

        qs = [stacked_queries(h) for h in heads]
        m_sc[...] = jnp.full(m_sc.shape, NEG, F32)
        acc_sc[...] = jnp.zeros(acc_sc.shape, F32)

        def issue(blk, s_ref):
            rows = pl.ds(pl.multiple_of(blk * qb, qb), qb)
            for h in heads:
                kb = k_ref[0, rows, h * LANES:(h + 1) * LANES]
                s_ref[h] = jnp.dot(kb, qs[h], preferred_element_type=F32)

        def consume(blk, s_ref, mask):
            for h in heads:
                s = s_ref[h]
                if mask is not None:
                    s = jnp.where(mask, s, NEG)
                m_old = m_sc[h]
                m_new = jnp.maximum(m_old, jnp.max(s, axis=0, keepdims=True))
                alpha = jnp.exp2(m_old - m_new)
                p = jnp.exp2(s - m_new).astype(BF16)
                acc_sc[h] = alpha * acc_sc[h] + jnp.dot(vt_ref[0, h, blk], p, preferred_element_type=F32)
                m_sc[h] = m_new

        last = jnp.maximum(qi - 1, 0)
        issue(qi, sa_sc)
        issue(0, sb_sc)
        consume(qi, sa_sc, key <= qry)

        def pair(j, c):
            b0 = 2 * j
            issue(b0 + 1, sa_sc)
            consume(b0, sb_sc, None)
            issue(jnp.minimum(b0 + 2, last), sb_sc)
            consume(b0 + 1, sa_sc, None)
            return c

        lax.fori_loop(0, qi // 2, pair, 0)

        @pl.when(qi % 2 == 1)
        def _():
            consume(qi - 1, sb_sc, None)

        for h in heads:
            o_all = acc_sc[h, 0:LANES, :] / acc_sc[h, LANES:LANES + 1, :]
            o = o_all[:, :qb] - lam * o_all[:, qb:]
            o = o * lax.rsqrt(jnp.mean(o * o, axis=0, keepdims=True) + SUBLN_EPS) * gain_ref[...]
            o_ref[0, q_rows, h * LANES:(h + 1) * LANES] = (o * (1.0 - lambda_init)).T.astype(o_ref.dtype)
        return carry

    lax.fori_loop(0, qt_ref.shape[2], query_block, 0)


def _diff(u3, qt, vt, lam_p, subln_gain, lambda_init, qb, nh=2):
    b, s, _ = u3.shape
    per = WIDTH // (nh * LANES)
    nb = s // qb
    seq = lambda split: pl.BlockSpec((1, s, nh * LANES), lambda bi, g: (bi, 0, split * per + g))
    return pl.pallas_call(
        functools.partial(_diff_kernel, qb=qb, lambda_init=lambda_init),
        grid=(b, N_HEADS_DIFF // nh),
        in_specs=[pl.BlockSpec(lam_p.shape, lambda bi, g: (0, 0)),
                  pl.BlockSpec((LANES, 1), lambda bi, g: (0, 0)),
                  pl.BlockSpec((1, nh, nb, LANES, qb), lambda bi, g: (bi, g, 0, 0, 0)),
                  seq(COL_KB),
                  pl.BlockSpec((1, nh, nb, VT_ROWS, qb), lambda bi, g: (bi, g, 0, 0, 0))],
        out_specs=pl.BlockSpec((1, s, nh * LANES), lambda bi, g: (bi, 0, g)),
        out_shape=jax.ShapeDtypeStruct((b, s, WIDTH), BF16),
        scratch_shapes=[pltpu.VMEM((nh, 1, 2 * qb), F32),
                        pltpu.VMEM((nh, VT_ROWS, 2 * qb), F32),
                        pltpu.VMEM((nh, qb, 2 * qb), F32), pltpu.VMEM((nh, qb, 2 * qb), F32)],
        compiler_params=_params("arbitrary", "arbitrary"),
        name="diff",
    )(lam_p, subln_gain.reshape(LANES, 1), qt, u3, vt)


def _silu_of_half(half):
    return half + half * jnp.tanh(half)


def _post_kernel(h_ref, ga_ref, gb_ref, o1_ref, l1_ref, o4_ref, l4_ref, o16_ref, l16_ref, ob_ref, p_ref,
                 wo32_ref, ng_ref, wg32_ref, wp32_ref, fg_ref, out_ref, on_sc, ln_sc, o4_sc, l4_sc,
                 wo_ref, wg_ref, wp_ref, *, final):
    tm = h_ref.shape[0]

    @pl.when(pl.program_id(0) == 0)
    def _():
        wo_ref[...] = wo32_ref[0].astype(BF16)
        wg_ref[...] = (0.5 * wg32_ref[0]).astype(BF16)
        wp_ref[...] = (0.5 * wp32_ref[0]).astype(BF16)

    hr = tm // POST_PARTS
    n4, n16 = hr // 4, hr // 16

    def merge(part):
        rows, rows4, rows16 = (pl.ds(part * n, n) for n in (hr, n4, n16))
        merged = []
        for j in range(WIDTH // LANES):
            lanes = slice(j * LANES, (j + 1) * LANES)
            for r4 in range(4):
                for k in range(4):
                    o4_sc[pl.ds(k, n16, stride=4), :] = o16_ref[0, 4 * k + r4, rows16, lanes].astype(F32)
                    l4_sc[pl.ds(k, n16, stride=4), :] = l16_ref[0, 4 * k + r4, rows16, lanes]
                o16, l16 = o4_sc[...], l4_sc[...]
                o4, l4 = o4_ref[0, r4, rows4, lanes].astype(F32), l4_ref[0, r4, rows4, lanes]
                top = jnp.maximum(l4, l16)
                w4, w16 = jnp.exp2(l4 - top), jnp.exp2(l16 - top)
                den = w4 + w16
                on_sc[pl.ds(r4, n4, stride=4), :] = (w4 * o4 + w16 * o16) / den
                ln_sc[pl.ds(r4, n4, stride=4), :] = top + jnp.log2(den)
            o1, l1 = o1_ref[rows, lanes].astype(F32), l1_ref[rows, lanes]
            ox, lx = on_sc[...], ln_sc[...]
            top = jnp.maximum(l1, lx)
            w1, wx = jnp.exp2(l1 - top), jnp.exp2(lx - top)
            merged.append((w1 * o1 + wx * ox) / (w1 + wx))
        return jnp.concatenate(merged, axis=1)

    def mix(part, oa):
        rows = pl.ds(part * hr, hr)
        ya = (oa * _silu_of_half(ga_ref[rows, :].astype(F32))).astype(BF16)
        yb = (ob_ref[rows, :].astype(F32) * _silu_of_half(gb_ref[rows, :].astype(F32))).astype(BF16)
        y = (jnp.dot(ya, wo_ref[0:WIDTH, :], preferred_element_type=F32)
             + jnp.dot(yb, wo_ref[WIDTH:2 * WIDTH, :], preferred_element_type=F32))
        return h_ref[rows, :] + y

    def embed(part, h1):
        rows = pl.ds(part * hr, hr)
        n = h1 * lax.rsqrt(jnp.mean(h1 * h1, axis=-1, keepdims=True) + RMS_EPS) * ng_ref[...]
        t = jnp.tanh(jnp.dot(n.astype(BF16), wg_ref[...], preferred_element_type=F32))
        ple_half = jnp.dot(p_ref[0, rows, :].astype(BF16), wp_ref[...], preferred_element_type=F32)
        h2 = h1 + ple_half + ple_half * t
        if final:
            h2 = h2 * lax.rsqrt(jnp.mean(h2 * h2, axis=-1, keepdims=True) + RMS_EPS) * fg_ref[...]
        out_ref[rows, :] = h2

    h1 = [mix(part, merge(part)) for part in range(POST_PARTS)]
    for part in range(POST_PARTS):
        embed(part, h1[part])


def _post(h, u, g1, g4, g16, ob, p, w_out, ple_gain, w_gate, w_ple, layer, final_gain, final, batch, tm=512):
    m, d = h.shape
    whole = lambda w: pl.BlockSpec((1,) + w.shape[1:], lambda i: (layer, 0, 0), pipeline_mode=pl.Buffered(1))
    nt = m // batch // tm
    row = lambda i: (i, 0)
    const = lambda i: (0, 0)
    res = lambda i: (i // nt, 0, i % nt, 0)
    grp = lambda dil: pl.BlockSpec((1, dil, tm // dil, WIDTH), res)
    return pl.pallas_call(
        functools.partial(_post_kernel, final=final),
        grid=(m // tm,),
        in_specs=[pl.BlockSpec((tm, d), row),
                  pl.BlockSpec((tm, WIDTH), lambda i: (i, COL_GA)),
                  pl.BlockSpec((tm, WIDTH), lambda i: (i, COL_GB)),
                  pl.BlockSpec((tm, WIDTH), row), pl.BlockSpec((tm, WIDTH), row),
                  grp(4), grp(4), grp(16), grp(16),
                  pl.BlockSpec((tm, WIDTH), row),
                  pl.BlockSpec((1, tm, p.shape[2]), lambda i: (layer, i, 0)),
                  whole(w_out),
                  pl.BlockSpec((1, d), const),
                  whole(w_gate),
                  whole(w_ple),
                  pl.BlockSpec((1, d), const)],
        out_specs=pl.BlockSpec((tm, d), row),
        out_shape=jax.ShapeDtypeStruct((m, d), F32),
        scratch_shapes=[pltpu.VMEM((tm // POST_PARTS, LANES), F32), pltpu.VMEM((tm // POST_PARTS, LANES), F32),
                        pltpu.VMEM((tm // POST_PARTS // 4, LANES), F32),
                        pltpu.VMEM((tm // POST_PARTS // 4, LANES), F32),
                        pltpu.VMEM(w_out.shape[1:], BF16), pltpu.VMEM(w_gate.shape[1:], BF16),
                        pltpu.VMEM(w_ple.shape[1:], BF16)],
        compiler_params=_params("arbitrary"),
        name="post",
    )(h, u, u, g1[0].reshape(m, WIDTH), g1[1].reshape(m, WIDTH), g4[0], g4[1], g16[0], g16[1], ob, p,
      w_out, ple_gain.reshape(1, d), w_gate, w_ple, final_gain.reshape(1, d))


def kernel(x, p, attn_norm_gain, w_in, w_out, lambda_q1, lambda_k1, lambda_q2, lambda_k2, subln_gain,
           ple_norm_gain, w_ple_gate, w_ple, final_norm_gain):
    b, s, d = x.shape
    depth = w_in.shape[0]
    h = x.reshape(b * s, d)
    cos, sa, sb = _rope_tables(s)
    col = jnp.arange(IN_COLS) // WIDTH
    col_scale = jnp.where((col == COL_QA) | (col == COL_QB), HEAD_DIM ** -0.5 * LOG2_E,
                          jnp.where((col == COL_GA) | (col == COL_GB), 0.5, 1.0)).astype(F32)
    a_splits = (COL_QA, COL_KA, COL_VA)
    for i in range(depth):
        lambda_init = 0.8 - 0.6 * math.exp(-0.3 * i)
        u, qkv4, qkv16, vt, qt = _proj(h, attn_norm_gain[i], w_in, i, col_scale, cos, sa, sb, b, DIFF_QB)
        u3 = u.reshape(b, s, IN_COLS)
        g1 = _dswa_group(u3.reshape(b, 1, s, IN_COLS), a_splits)
        g4 = _dswa_group(qkv4, a_splits)
        g16 = _dswa_group(qkv16, a_splits)
        lam_p = jnp.stack([lambda_q1[i], lambda_k1[i], lambda_q2[i], lambda_k2[i]]).astype(F32)
        ob = _diff(u3, qt, vt, lam_p, subln_gain[i], lambda_init, DIFF_QB)
        h = _post(h, u, g1, g4, g16, ob.reshape(b * s, WIDTH), p.reshape(depth, b * s, -1),
                  w_out, ple_norm_gain[i], w_ple_gate, w_ple, i, final_norm_gain, final=i == depth - 1, batch=b)
    return h.reshape(b, s, d)
```

```python
import functools
import math

import jax
import jax.numpy as jnp
from jax import lax
from jax.experimental import pallas as pl
from jax.experimental.pallas import tpu as pltpu

HEAD_DIM = 64
LANES = 128
DSWA_W = 128
DSWA_UNROLL = 16
N_HEADS_DIFF = 4
DIFF_QB = 512
POST_PARTS = 2
VT_ROWS = LANES + 16
WIDTH = 512
IN_COLS = 8 * WIDTH
ROPE_THETA = 500000.0
ROPE_DIM = HEAD_DIM // 4
RMS_EPS = 1e-6
SUBLN_EPS = 1e-5
NEG = -1e30
LOG2_E = math.log2(math.e)
VMEM_LIMIT = 48 * 1024 * 1024
PROJ_VMEM_LIMIT = 58 * 1024 * 1024

COL_QA, COL_KA, COL_VA, COL_GA, COL_QB, COL_KB, COL_VB, COL_GB = range(8)
ROPE_SPLITS = (COL_QA, COL_KA, COL_QB, COL_KB)

BF16 = jnp.bfloat16
F32 = jnp.float32


def _params(*sem):
    return pltpu.CompilerParams(dimension_semantics=sem, vmem_limit_bytes=VMEM_LIMIT)


def _proj_kernel(x_ref, g_ref, w32_ref, ws_ref, cos_ref, sa_ref, sb_ref, o_ref, d4_ref, d16_ref, vt_ref, qt_ref,
                 buf_sc, buf4_sc, w_ref, *, qb):
    tm = x_ref.shape[0]

    @pl.when(pl.program_id(0) == 0)
    def _():
        for c in range(IN_COLS // WIDTH):
            cols = slice(c * WIDTH, (c + 1) * WIDTH)
            w_ref[:, cols] = (w32_ref[0, :, cols] * ws_ref[:, cols]).astype(BF16)

    x = x_ref[...]
    ms = jnp.mean(x * x, axis=-1, keepdims=True)
    xn = (x * lax.rsqrt(ms + RMS_EPS) * g_ref[...]).astype(BF16)
    cos, sa, sb = cos_ref[...], sa_ref[...], sb_ref[...]
    for c in range(IN_COLS // WIDTH):
        cols = slice(c * WIDTH, (c + 1) * WIDTH)
        acc = jnp.dot(xn, w_ref[:, cols], preferred_element_type=F32)
        if c in ROPE_SPLITS:
            blocks = []
            for j in range(WIDTH // LANES):
                blk = acc[:, j * LANES:(j + 1) * LANES]
                blocks.append(blk * cos + pltpu.roll(blk, LANES - ROPE_DIM // 2, 1) * sa
                              + pltpu.roll(blk, ROPE_DIM // 2, 1) * sb)
            acc = jnp.concatenate(blocks, axis=1)
        o_ref[:, cols] = acc.astype(BF16)
        if c in (COL_QA, COL_KA, COL_VA):
            n4, n16 = tm // 4, tm // 16
            for j in range(WIDTH // LANES):
                lanes = slice(c * WIDTH + j * LANES, c * WIDTH + (j + 1) * LANES)
                buf_sc[j] = acc[:, j * LANES:(j + 1) * LANES]
                for r4 in range(4):
                    rows4 = buf_sc[j, pl.ds(r4, n4, stride=4), :]
                    d4_ref[0, r4, :, lanes] = rows4.astype(BF16)
                    buf4_sc[j, r4 * n4:(r4 + 1) * n4, :] = rows4
                for r4 in range(4):
                    for k in range(4):
                        rows16 = buf4_sc[j, pl.ds(r4 * n4 + k, n16, stride=4), :]
                        d16_ref[0, 4 * k + r4, :, lanes] = rows16.astype(BF16)
        if c == COL_QB:
            for kb in range(tm // qb):
                for hd in range(N_HEADS_DIFF):
                    blk = acc[kb * qb:(kb + 1) * qb, hd * LANES:(hd + 1) * LANES]
                    qt_ref[0, hd, kb] = blk.T.astype(BF16)
        if c == COL_VB:
            ones_row = (lax.broadcasted_iota(jnp.int32, (VT_ROWS - LANES, qb), 0) == 0).astype(BF16)
            for kb in range(tm // qb):
                for hd in range(N_HEADS_DIFF):
                    blk = acc[kb * qb:(kb + 1) * qb, hd * LANES:(hd + 1) * LANES]
                    vt_ref[0, hd, kb, 0:LANES, :] = blk.T.astype(BF16)
                    vt_ref[0, hd, kb, LANES:VT_ROWS, :] = ones_row


def _proj(h, gain, w_in, layer, col_scale, cos, sa, sb, batch, qb, tm=512):
    m, d = h.shape
    seq = m // batch
    nt = seq // tm
    row = lambda i: (i, 0)
    tab = lambda i: (i % nt, 0)
    res = lambda i: (i // nt, 0, i % nt, 0)
    return pl.pallas_call(
        functools.partial(_proj_kernel, qb=qb),
        grid=(m // tm,),
        in_specs=[pl.BlockSpec((tm, d), row),
                  pl.BlockSpec((1, d), lambda i: (0, 0)),
                  pl.BlockSpec((1, d, IN_COLS), lambda i: (layer, 0, 0), pipeline_mode=pl.Buffered(1)),
                  pl.BlockSpec((1, IN_COLS), lambda i: (0, 0)),
                  pl.BlockSpec((tm, LANES), tab),
                  pl.BlockSpec((tm, LANES), tab),
                  pl.BlockSpec((tm, LANES), tab)],
        out_specs=[pl.BlockSpec((tm, IN_COLS), row),
                   pl.BlockSpec((1, 4, tm // 4, 3 * WIDTH), res),
                   pl.BlockSpec((1, 16, tm // 16, 3 * WIDTH), res),
                   pl.BlockSpec((1, N_HEADS_DIFF, tm // qb, VT_ROWS, qb), lambda i: (i // nt, 0, i % nt, 0, 0)),
                   pl.BlockSpec((1, N_HEADS_DIFF, tm // qb, LANES, qb), lambda i: (i // nt, 0, i % nt, 0, 0))],
        out_shape=[jax.ShapeDtypeStruct((m, IN_COLS), BF16),
                   jax.ShapeDtypeStruct((batch, 4, seq // 4, 3 * WIDTH), BF16),
                   jax.ShapeDtypeStruct((batch, 16, seq // 16, 3 * WIDTH), BF16),
                   jax.ShapeDtypeStruct((batch, N_HEADS_DIFF, seq // qb, VT_ROWS, qb), BF16),
                   jax.ShapeDtypeStruct((batch, N_HEADS_DIFF, seq // qb, LANES, qb), BF16)],
        scratch_shapes=[pltpu.VMEM((WIDTH // LANES, tm, LANES), F32),
                        pltpu.VMEM((WIDTH // LANES, tm, LANES), F32),
                        pltpu.VMEM((d, IN_COLS), BF16)],
        compiler_params=pltpu.CompilerParams(dimension_semantics=("arbitrary",),
                                             vmem_limit_bytes=PROJ_VMEM_LIMIT),
        name="proj",
    )(h, gain.reshape(1, d), w_in, col_scale.reshape(1, IN_COLS), cos, sa, sb)


def _rope_tables(seq):
    half = ROPE_DIM // 2
    inv = jnp.power(ROPE_THETA, -jnp.arange(half, dtype=F32) * (2.0 / ROPE_DIM))
    ang = jnp.arange(seq).astype(F32)[:, None] * inv[None, :]
    cos, sin = jnp.cos(ang), jnp.sin(ang)
    ch = jnp.arange(LANES) % HEAD_DIM
    cos_l = jnp.take(cos, ch % half, axis=1)
    sin_l = jnp.take(sin, ch % half, axis=1)
    cos_t = jnp.where(ch < ROPE_DIM, cos_l, 1.0)
    sa = jnp.where(ch < half, -sin_l, 0.0)
    sb = jnp.where((ch >= half) & (ch < ROPE_DIM), sin_l, 0.0)
    return cos_t, sa, sb


def _dswa_kernel(q_ref, kc_ref, kp_ref, vc_ref, vp_ref, o_ref, l_ref, vt_sc, bias_sc, k_sc, sa_sc, sb_sc, *,
                 lc):
    w = DSWA_W
    nblk = lc // w
    nres = q_ref.shape[1]
    first_chunk = pl.program_id(2) == 0
    head_a = lax.broadcasted_iota(jnp.int32, (1, LANES), 1) < HEAD_DIM
    kj = lax.broadcasted_iota(jnp.int32, (2 * w, 2 * w), 0)
    qi = lax.broadcasted_iota(jnp.int32, (2 * w, 2 * w), 1) & (w - 1)
    dist = w + qi - kj
    band = (dist >= 0) & (dist <= w)
    bias_sc[0] = jnp.where(band, 0.0, NEG)
    bias_sc[1] = jnp.where(band & ((kj >= w) | jnp.logical_not(first_chunk)), 0.0, NEG)

    def transposed(v):
        return v.T

    ones_row = (lax.broadcasted_iota(jnp.int32, (VT_ROWS - LANES, w), 0) == 0).astype(BF16)
    for r in range(nres):
        for hp in range(WIDTH // LANES):
            lanes = slice(hp * LANES, (hp + 1) * LANES)
            vt_sc[r, 0, hp, 0:LANES, :] = transposed(vp_ref[0, r, :, lanes])
            vt_sc[r, 0, hp, LANES:VT_ROWS, :] = ones_row
            for n in range(nblk):
                vt_sc[r, n + 1, hp, 0:LANES, :] = transposed(vc_ref[0, r, n * w:(n + 1) * w, lanes])
                vt_sc[r, n + 1, hp, LANES:VT_ROWS, :] = ones_row
        k_sc[r, 0:w, :] = kp_ref[0, r]
        k_sc[r, w:w + lc, :] = kc_ref[0, r]

    def scores(t, hp):
        lanes = slice(hp * LANES, (hp + 1) * LANES)
        r, n = t // nblk, t % nblk
        row0 = pl.multiple_of(n * w, w)
        qb = q_ref[0, r, pl.ds(row0, w), lanes]
        zero = jnp.zeros_like(qb)
        qs = jnp.concatenate([jnp.where(head_a, qb, zero), jnp.where(head_a, zero, qb)], axis=0)
        kk = k_sc[r, pl.ds(row0, 2 * w), lanes]
        return lax.dot_general(kk, qs, (((1,), (1,)), ((), ())), preferred_element_type=F32)

    def finish(t, hp, raw):
        lanes = slice(hp * LANES, (hp + 1) * LANES)
        r, n = t // nblk, t % nblk
        row0 = pl.multiple_of(n * w, w)
        bias = bias_sc[jnp.where(n == 0, 1, 0)]
        ms, ps = [], []
        for half in (slice(0, w), slice(w, 2 * w)):
            s = raw[:, half] + bias[:, half]
            m_h = jnp.max(s, axis=0, keepdims=True)
            ms.append(m_h)
            ps.append(jnp.exp2(s - m_h).astype(BF16))
        m = jnp.concatenate(ms, axis=1)
        p = jnp.concatenate(ps, axis=1)
        vv = jnp.concatenate([vt_sc[r, n, hp], vt_sc[r, n + 1, hp]], axis=1)
        pv = jnp.dot(vv, p, preferred_element_type=F32)
        den = pv[LANES:LANES + 1, :]
        o = pv[0:LANES, :] * (1.0 / den)
        lse = m + jnp.log2(den)
        o_sel = jnp.concatenate([o[:HEAD_DIM, :w], o[HEAD_DIM:, w:]], axis=0)
        l_sel = jnp.concatenate([jnp.broadcast_to(lse[:, :w], (HEAD_DIM, w)),
                                 jnp.broadcast_to(lse[:, w:], (HEAD_DIM, w))], axis=0)
        o_ref[0, r, pl.ds(row0, w), lanes] = o_sel.T.astype(o_ref.dtype)
        l_ref[0, r, pl.ds(row0, w), lanes] = l_sel.T

    nhp = WIDTH // LANES

    def issue(n, s_ref):
        for hp in range(nhp):
            s_ref[hp] = scores(n, hp)

    def consume(n, s_ref):
        for hp in range(nhp):
            finish(n, hp, s_ref[hp])

    issue(0, sa_sc)

    total = nres * nblk
    unroll = math.gcd(total, DSWA_UNROLL)

    def group(j, carry):
        t0 = unroll * j
        for i in range(unroll):
            cur, nxt = (sa_sc, sb_sc) if i % 2 == 0 else (sb_sc, sa_sc)
            issue(jnp.minimum(t0 + i + 1, total - 1), nxt)
            consume(t0 + i, cur)
        return carry

    lax.fori_loop(0, total // unroll, group, 0)


def _dswa_group(qkv, splits):
    b, d, sub, _ = qkv.shape
    lc = min(sub, 2048)
    per = lc // DSWA_W
    nres = min(d, max(1, DSWA_UNROLL // per))
    assert (nres * per) % 2 == 0 and d % nres == 0, "the kernel walks query blocks in even groups"
    sq, sk, sv = splits

    def cur(split):
        return pl.BlockSpec((1, nres, lc, WIDTH), lambda bi, r, c: (bi, r, c, split))

    def prv(split):
        return pl.BlockSpec((1, nres, DSWA_W, WIDTH), lambda bi, r, c: (bi, r, jnp.maximum(c * per - 1, 0), split))

    out = pl.BlockSpec((1, nres, lc, WIDTH), lambda bi, r, c: (bi, r, c, 0))
    return pl.pallas_call(
        functools.partial(_dswa_kernel, lc=lc),
        grid=(b, d // nres, sub // lc),
        in_specs=[cur(sq), cur(sk), prv(sk), cur(sv), prv(sv)],
        out_specs=[out, out],
        out_shape=[jax.ShapeDtypeStruct((b, d, sub, WIDTH), BF16),
                   jax.ShapeDtypeStruct((b, d, sub, WIDTH), F32)],
        scratch_shapes=[pltpu.VMEM((nres, per + 1, WIDTH // LANES, VT_ROWS, DSWA_W), BF16),
                        pltpu.VMEM((2, 2 * DSWA_W, 2 * DSWA_W), F32),
                        pltpu.VMEM((nres, lc + DSWA_W, WIDTH), BF16),
                        pltpu.VMEM((WIDTH // LANES, 2 * DSWA_W, 2 * DSWA_W), F32),
                        pltpu.VMEM((WIDTH // LANES, 2 * DSWA_W, 2 * DSWA_W), F32)],
        compiler_params=_params("arbitrary", "arbitrary", "arbitrary"),
        name=f"dswa_d{d}",
    )(qkv, qkv, qkv, qkv, qkv)


def _diff_kernel(lam_ref, gain_ref, qt_ref, k_ref, vt_ref, o_ref, m_sc, acc_sc, sa_sc, sb_sc, *, qb,
                 lambda_init):
    nh = qt_ref.shape[1]
    heads = range(nh)
    comp_a = lax.broadcasted_iota(jnp.int32, (LANES, 1), 0) < HEAD_DIM
    key = lax.broadcasted_iota(jnp.int32, (qb, 2 * qb), 0)
    qry = lax.broadcasted_iota(jnp.int32, (qb, 2 * qb), 1) & (qb - 1)
    lam_p = lam_ref[...]
    lam = (jnp.exp(jnp.sum(lam_p[0:1] * lam_p[1:2], axis=1, keepdims=True))
           - jnp.exp(jnp.sum(lam_p[2:3] * lam_p[3:4], axis=1, keepdims=True)) + lambda_init)

    def query_block(qi, carry):
        q_rows = pl.ds(pl.multiple_of(qi * qb, qb), qb)

        def stacked_queries(h):
            qt = qt_ref[0, h, qi]
            zero = jnp.zeros_like(qt)
            return jnp.concatenate([jnp.where(comp_a, qt, zero), jnp.where(comp_a, zero, qt)], axis=1)

        qs = [stacked_queries(h) for h in heads]
        m_sc[...] = jnp.full(m_sc.shape, NEG, F32)
        acc_sc[...] = jnp.zeros(acc_sc.shape, F32)

        def issue(blk, s_ref):
            rows = pl.ds(pl.multiple_of(blk * qb, qb), qb)
            for h in heads:
                kb = k_ref[0, rows, h * LANES:(h + 1) * LANES]
                s_ref[h] = jnp.dot(kb, qs[h], preferred_element_type=F32)

        def consume(blk, s_ref, mask):
            for h in heads:
                s = s_ref[h]
                if mask is not None:
                    s = jnp.where(mask, s, NEG)
                m_old = m_sc[h]
                m_new = jnp.maximum(m_old, jnp.max(s, axis=0, keepdims=True))
                alpha = jnp.exp2(m_old - m_new)
                p = jnp.exp2(s - m_new).astype(BF16)
                acc_sc[h] = alpha * acc_sc[h] + jnp.dot(vt_ref[0, h, blk], p, preferred_element_type=F32)
                m_sc[h] = m_new

        last = jnp.maximum(qi - 1, 0)
        issue(qi, sa_sc)
        issue(0, sb_sc)
        consume(qi, sa_sc, key <= qry)

        def pair(j, c):
            b0 = 2 * j
            issue(b0 + 1, sa_sc)
            consume(b0, sb_sc, None)
            issue(jnp.minimum(b0 + 2, last), sb_sc)
            consume(b0 + 1, sa_sc, None)
            return c

        lax.fori_loop(0, qi // 2, pair, 0)

        @pl.when(qi % 2 == 1)
        def _():
            consume(qi - 1, sb_sc, None)

        for h in heads:
            o_all = acc_sc[h, 0:LANES, :] / acc_sc[h, LANES:LANES + 1, :]
            o = o_all[:, :qb] - lam * o_all[:, qb:]
            o = o * lax.rsqrt(jnp.mean(o * o, axis=0, keepdims=True) + SUBLN_EPS) * gain_ref[...]
            o_ref[0, q_rows, h * LANES:(h + 1) * LANES] = (o * (1.0 - lambda_init)).T.astype(o_ref.dtype)
        return carry

    lax.fori_loop(0, qt_ref.shape[2], query_block, 0)


def _diff(u3, qt, vt, lam_p, subln_gain, lambda_init, qb, nh=2):
    b, s, _ = u3.shape
    per = WIDTH // (nh * LANES)
    nb = s // qb
    seq = lambda split: pl.BlockSpec((1, s, nh * LANES), lambda bi, g: (bi, 0, split * per + g))
    return pl.pallas_call(
        functools.partial(_diff_kernel, qb=qb, lambda_init=lambda_init),
        grid=(b, N_HEADS_DIFF // nh),
        in_specs=[pl.BlockSpec(lam_p.shape, lambda bi, g: (0, 0)),
                  pl.BlockSpec((LANES, 1), lambda bi, g: (0, 0)),
                  pl.BlockSpec((1, nh, nb, LANES, qb), lambda bi, g: (bi, g, 0, 0, 0)),
                  seq(COL_KB),
                  pl.BlockSpec((1, nh, nb, VT_ROWS, qb), lambda bi, g: (bi, g, 0, 0, 0))],
        out_specs=pl.BlockSpec((1, s, nh * LANES), lambda bi, g: (bi, 0, g)),
        out_shape=jax.ShapeDtypeStruct((b, s, WIDTH), BF16),
        scratch_shapes=[pltpu.VMEM((nh, 1, 2 * qb), F32),
                        pltpu.VMEM((nh, VT_ROWS, 2 * qb), F32),
                        pltpu.VMEM((nh, qb, 2 * qb), F32), pltpu.VMEM((nh, qb, 2 * qb), F32)],
        compiler_params=_params("arbitrary", "arbitrary"),
        name="diff",
    )(lam_p, subln_gain.reshape(LANES, 1), qt, u3, vt)


def _silu_of_half(half):
    return half + half * jnp.tanh(half)


def _post_kernel(h_ref, ga_ref, gb_ref, o1_ref, l1_ref, o4_ref, l4_ref, o16_ref, l16_ref, ob_ref, p_ref,
                 wo32_ref, ng_ref, wg32_ref, wp32_ref, fg_ref, out_ref, on_sc, ln_sc, o4_sc, l4_sc,
                 wo_ref, wg_ref, wp_ref, *, final):
    tm = h_ref.shape[0]

    @pl.when(pl.program_id(0) == 0)
    def _():
        wo_ref[...] = wo32_ref[0].astype(BF16)
        wg_ref[...] = (0.5 * wg32_ref[0]).astype(BF16)
        wp_ref[...] = (0.5 * wp32_ref[0]).astype(BF16)

    hr = tm // POST_PARTS
    n4, n16 = hr // 4, hr // 16

    def merge(part):
        rows, rows4, rows16 = (pl.ds(part * n, n) for n in (hr, n4, n16))
        merged = []
        for j in range(WIDTH // LANES):
            lanes = slice(j * LANES, (j + 1) * LANES)
            for r4 in range(4):
                for k in range(4):
                    o4_sc[pl.ds(k, n16, stride=4), :] = o16_ref[0, 4 * k + r4, rows16, lanes].astype(F32)
                    l4_sc[pl.ds(k, n16, stride=4), :] = l16_ref[0, 4 * k + r4, rows16, lanes]
                o16, l16 = o4_sc[...], l4_sc[...]
                o4, l4 = o4_ref[0, r4, rows4, lanes].astype(F32), l4_ref[0, r4, rows4, lanes]
                top = jnp.maximum(l4, l16)
                w4, w16 = jnp.exp2(l4 - top), jnp.exp2(l16 - top)
                den = w4 + w16
                on_sc[pl.ds(r4, n4, stride=4), :] = (w4 * o4 + w16 * o16) / den
                ln_sc[pl.ds(r4, n4, stride=4), :] = top + jnp.log2(den)
            o1, l1 = o1_ref[rows, lanes].astype(F32), l1_ref[rows, lanes]
            ox, lx = on_sc[...], ln_sc[...]
            top = jnp.maximum(l1, lx)
            w1, wx = jnp.exp2(l1 - top), jnp.exp2(lx - top)
            merged.append((w1 * o1 + wx * ox) / (w1 + wx))
        return jnp.concatenate(merged, axis=1)

    def mix(part, oa):
        rows = pl.ds(part * hr, hr)
        ya = (oa * _silu_of_half(ga_ref[rows, :].astype(F32))).astype(BF16)
        yb = (ob_ref[rows, :].astype(F32) * _silu_of_half(gb_ref[rows, :].astype(F32))).astype(BF16)
        y = (jnp.dot(ya, wo_ref[0:WIDTH, :], preferred_element_type=F32)
             + jnp.dot(yb, wo_ref[WIDTH:2 * WIDTH, :], preferred_element_type=F32))
        return h_ref[rows, :] + y

    def embed(part, h1):
        rows = pl.ds(part * hr, hr)
        n = h1 * lax.rsqrt(jnp.mean(h1 * h1, axis=-1, keepdims=True) + RMS_EPS) * ng_ref[...]
        t = jnp.tanh(jnp.dot(n.astype(BF16), wg_ref[...], preferred_element_type=F32))
        ple_half = jnp.dot(p_ref[0, rows, :].astype(BF16), wp_ref[...], preferred_element_type=F32)
        h2 = h1 + ple_half + ple_half * t
        if final:
            h2 = h2 * lax.rsqrt(jnp.mean(h2 * h2, axis=-1, keepdims=True) + RMS_EPS) * fg_ref[...]
        out_ref[rows, :] = h2

    h1 = [mix(part, merge(part)) for part in range(POST_PARTS)]
    for part in range(POST_PARTS):
        embed(part, h1[part])


def _post(h, u, g1, g4, g16, ob, p, w_out, ple_gain, w_gate, w_ple, layer, final_gain, final, batch, tm=512):
    m, d = h.shape
    whole = lambda w: pl.BlockSpec((1,) + w.shape[1:], lambda i: (layer, 0, 0), pipeline_mode=pl.Buffered(1))
    nt = m // batch // tm
    row = lambda i: (i, 0)
    const = lambda i: (0, 0)
    res = lambda i: (i // nt, 0, i % nt, 0)
    grp = lambda dil: pl.BlockSpec((1, dil, tm // dil, WIDTH), res)
    return pl.pallas_call(
        functools.partial(_post_kernel, final=final),
        grid=(m // tm,),
        in_specs=[pl.BlockSpec((tm, d), row),
                  pl.BlockSpec((tm, WIDTH), lambda i: (i, COL_GA)),
                  pl.BlockSpec((tm, WIDTH), lambda i: (i, COL_GB)),
                  pl.BlockSpec((tm, WIDTH), row), pl.BlockSpec((tm, WIDTH), row),
                  grp(4), grp(4), grp(16), grp(16),
                  pl.BlockSpec((tm, WIDTH), row),
                  pl.BlockSpec((1, tm, p.shape[2]), lambda i: (layer, i, 0)),
                  whole(w_out),
                  pl.BlockSpec((1, d), const),
                  whole(w_gate),
                  whole(w_ple),
                  pl.BlockSpec((1, d), const)],
        out_specs=pl.BlockSpec((tm, d), row),
        out_shape=jax.ShapeDtypeStruct((m, d), F32),
        scratch_shapes=[pltpu.VMEM((tm // POST_PARTS, LANES), F32), pltpu.VMEM((tm // POST_PARTS, LANES), F32),
                        pltpu.VMEM((tm // POST_PARTS // 4, LANES), F32),
                        pltpu.VMEM((tm // POST_PARTS // 4, LANES), F32),
                        pltpu.VMEM(w_out.shape[1:], BF16), pltpu.VMEM(w_gate.shape[1:], BF16),
                        pltpu.VMEM(w_ple.shape[1:], BF16)],
        compiler_params=_params("arbitrary"),
        name="post",
    )(h, u, u, g1[0].reshape(m, WIDTH), g1[1].reshape(m, WIDTH), g4[0], g4[1], g16[0], g16[1], ob, p,
      w_out, ple_gain.reshape(1, d), w_gate, w_ple, final_gain.reshape(1, d))


def kernel(x, p, attn_norm_gain, w_in, w_out, lambda_q1, lambda_k1, lambda_q2, lambda_k2, subln_gain,
           ple_norm_gain, w_ple_gate, w_ple, final_norm_gain):
    b, s, d = x.shape
    depth = w_in.shape[0]
    h = x.reshape(b * s, d)
    cos, sa, sb = _rope_tables(s)
    col = jnp.arange(IN_COLS) // WIDTH
    col_scale = jnp.where((col == COL_QA) | (col == COL_QB), HEAD_DIM ** -0.5 * LOG2_E,
                          jnp.where((col == COL_GA) | (col == COL_GB), 0.5, 1.0)).astype(F32)
    a_splits = (COL_QA, COL_KA, COL_VA)
    for i in range(depth):
        lambda_init = 0.8 - 0.6 * math.exp(-0.3 * i)
        u, qkv4, qkv16, vt, qt = _proj(h, attn_norm_gain[i], w_in, i, col_scale, cos, sa, sb, b, DIFF_QB)
        u3 = u.reshape(b, s, IN_COLS)
        g1 = _dswa_group(u3.reshape(b, 1, s, IN_COLS), a_splits)
        g4 = _dswa_group(qkv4, a_splits)
        g16 = _dswa_group(qkv16, a_splits)
        lam_p = jnp.stack([lambda_q1[i], lambda_k1[i], lambda_q2[i], lambda_k2[i]]).astype(F32)
        ob = _diff(u3, qt, vt, lam_p, subln_gain[i], lambda_init, DIFF_QB)
        h = _post(h, u, g1, g4, g16, ob.reshape(b * s, WIDTH), p.reshape(depth, b * s, -1),
                  w_out, ple_norm_gain[i], w_ple_gate, w_ple, i, final_norm_gain, final=i == depth - 1, batch=b)
    return h.reshape(b, s, d)
```

```python
import functools
import math

import jax
import jax.numpy as jnp
from jax import lax
from jax.experimental import pallas as pl
from jax.experimental.pallas import tpu as pltpu

HEAD_DIM = 64
LANES = 128
DSWA_W = 128
DSWA_UNROLL = 16
N_HEADS_DIFF = 4
DIFF_QB = 512
POST_PARTS = 2
VT_ROWS = LANES + 16
WIDTH = 512
IN_COLS = 8 * WIDTH
ROPE_THETA = 500000.0
ROPE_DIM = HEAD_DIM // 4
RMS_EPS = 1e-6
SUBLN_EPS = 1e-5
NEG = -1e30
LOG2_E = math.log2(math.e)
VMEM_LIMIT = 48 * 1024 * 1024
PROJ_VMEM_LIMIT = 58 * 1024 * 1024

COL_QA, COL_KA, COL_VA, COL_GA, COL_QB, COL_KB, COL_VB, COL_GB = range(8)
ROPE_SPLITS = (COL_QA, COL_KA, COL_QB, COL_KB)

BF16 = jnp.bfloat16
F32 = jnp.float32


def _params(*sem):
    return pltpu.CompilerParams(dimension_semantics=sem, vmem_limit_bytes=VMEM_LIMIT)


def _proj_kernel(x_ref, g_ref, w32_ref, ws_ref, cos_ref, sa_ref, sb_ref, o_ref, d4_ref, d16_ref, vt_ref, qt_ref,
                 buf_sc, buf4_sc, w_ref, *, qb):
    tm = x_ref.shape[0]

    @pl.when(pl.program_id(0) == 0)
    def _():
        for c in range(IN_COLS // WIDTH):
            cols = slice(c * WIDTH, (c + 1) * WIDTH)
            w_ref[:, cols] = (w32_ref[0, :, cols] * ws_ref[:, cols]).astype(BF16)

    x = x_ref[...]
    ms = jnp.mean(x * x, axis=-1, keepdims=True)
    xn = (x * lax.rsqrt(ms + RMS_EPS) * g_ref[...]).astype(BF16)
    cos, sa, sb = cos_ref[...], sa_ref[...], sb_ref[...]
    for c in range(IN_COLS // WIDTH):
        cols = slice(c * WIDTH, (c + 1) * WIDTH)
        acc = jnp.dot(xn, w_ref[:, cols], preferred_element_type=F32)
        if c in ROPE_SPLITS:
            blocks = []
            for j in range(WIDTH // LANES):
                blk = acc[:, j * LANES:(j + 1) * LANES]
                blocks.append(blk * cos + pltpu.roll(blk, LANES - ROPE_DIM // 2, 1) * sa
                              + pltpu.roll(blk, ROPE_DIM // 2, 1) * sb)
            acc = jnp.concatenate(blocks, axis=1)
        o_ref[:, cols] = acc.astype(BF16)
        if c in (COL_QA, COL_KA, COL_VA):
            n4, n16 = tm // 4, tm // 16
            for j in range(WIDTH // LANES):
                lanes = slice(c * WIDTH + j * LANES, c * WIDTH + (j + 1) * LANES)
                buf_sc[j] = acc[:, j * LANES:(j + 1) * LANES]
                for r4 in range(4):
                    rows4 = buf_sc[j, pl.ds(r4, n4, stride=4), :]
                    d4_ref[0, r4, :, lanes] = rows4.astype(BF16)
                    buf4_sc[j, r4 * n4:(r4 + 1) * n4, :] = rows4
                for r4 in range(4):
                    for k in range(4):
                        rows16 = buf4_sc[j, pl.ds(r4 * n4 + k, n16, stride=4), :]
                        d16_ref[0, 4 * k + r4, :, lanes] = rows16.astype(BF16)
        if c == COL_QB:
            for kb in range(tm // qb):
                for hd in range(N_HEADS_DIFF):
                    blk = acc[kb * qb:(kb + 1) * qb, hd * LANES:(hd + 1) * LANES]
                    qt_ref[0, hd, kb] = blk.T.astype(BF16)
        if c == COL_VB:
            ones_row = (lax.broadcasted_iota(jnp.int32, (VT_ROWS - LANES, qb), 0) == 0).astype(BF16)
            for kb in range(tm // qb):
                for hd in range(N_HEADS_DIFF):
                    blk = acc[kb * qb:(kb + 1) * qb, hd * LANES:(hd + 1) * LANES]
                    vt_ref[0, hd, kb, 0:LANES, :] = blk.T.astype(BF16)
                    vt_ref[0, hd, kb, LANES:VT_ROWS, :] = ones_row


def _proj(h, gain, w_in, layer, col_scale, cos, sa, sb, batch, qb, tm=512):
    m, d = h.shape
    seq = m // batch
    nt = seq // tm
    row = lambda i: (i, 0)
    tab = lambda i: (i % nt, 0)
    res = lambda i: (i // nt, 0, i % nt, 0)
    return pl.pallas_call(
        functools.partial(_proj_kernel, qb=qb),
        grid=(m // tm,),
        in_specs=[pl.BlockSpec((tm, d), row),
                  pl.BlockSpec((1, d), lambda i: (0, 0)),
                  pl.BlockSpec((1, d, IN_COLS), lambda i: (layer, 0, 0), pipeline_mode=pl.Buffered(1)),
                  pl.BlockSpec((1, IN_COLS), lambda i: (0, 0)),
                  pl.BlockSpec((tm, LANES), tab),
                  pl.BlockSpec((tm, LANES), tab),
                  pl.BlockSpec((tm, LANES), tab)],
        out_specs=[pl.BlockSpec((tm, IN_COLS), row),
                   pl.BlockSpec((1, 4, tm // 4, 3 * WIDTH), res),
                   pl.BlockSpec((1, 16, tm // 16, 3 * WIDTH), res),
                   pl.BlockSpec((1, N_HEADS_DIFF, tm // qb, VT_ROWS, qb), lambda i: (i // nt, 0, i % nt, 0, 0)),
                   pl.BlockSpec((1, N_HEADS_DIFF, tm // qb, LANES, qb), lambda i: (i // nt, 0, i % nt, 0, 0))],
        out_shape=[jax.ShapeDtypeStruct((m, IN_COLS), BF16),
                   jax.ShapeDtypeStruct((batch, 4, seq // 4, 3 * WIDTH), BF16),
                   jax.ShapeDtypeStruct((batch, 16, seq // 16, 3 * WIDTH), BF16),
                   jax.ShapeDtypeStruct((batch, N_HEADS_DIFF, seq // qb, VT_ROWS, qb), BF16),
                   jax.ShapeDtypeStruct((batch, N_HEADS_DIFF, seq // qb, LANES, qb), BF16)],
        scratch_shapes=[pltpu.VMEM((WIDTH // LANES, tm, LANES), F32),
                        pltpu.VMEM((WIDTH // LANES, tm, LANES), F32),
                        pltpu.VMEM((d, IN_COLS), BF16)],
        compiler_params=pltpu.CompilerParams(dimension_semantics=("arbitrary",),
                                             vmem_limit_bytes=PROJ_VMEM_LIMIT),
        name="proj",
    )(h, gain.reshape(1, d), w_in, col_scale.reshape(1, IN_COLS), cos, sa, sb)


def _rope_tables(seq):
    half = ROPE_DIM // 2
    inv = jnp.power(ROPE_THETA, -jnp.arange(half, dtype=F32) * (2.0 / ROPE_DIM))
    ang = jnp.arange(seq).astype(F32)[:, None] * inv[None, :]
    cos, sin = jnp.cos(ang), jnp.sin(ang)
    ch = jnp.arange(LANES) % HEAD_DIM
    cos_l = jnp.take(cos, ch % half, axis=1)
    sin_l = jnp.take(sin, ch % half, axis=1)
    cos_t = jnp.where(ch < ROPE_DIM, cos_l, 1.0)
    sa = jnp.where(ch < half, -sin_l, 0.0)
    sb = jnp.where((ch >= half) & (ch < ROPE_DIM), sin_l, 0.0)
    return cos_t, sa, sb


def _dswa_kernel(q_ref, kc_ref, kp_ref, vc_ref, vp_ref, o_ref, l_ref, vt_sc, bias_sc, k_sc, sa_sc, sb_sc, *,
                 lc):
    w = DSWA_W
    nblk = lc // w
    nres = q_ref.shape[1]
    first_chunk = pl.program_id(2) == 0
    head_a = lax.broadcasted_iota(jnp.int32, (1, LANES), 1) < HEAD_DIM
    kj = lax.broadcasted_iota(jnp.int32, (2 * w, 2 * w), 0)
    qi = lax.broadcasted_iota(jnp.int32, (2 * w, 2 * w), 1) & (w - 1)
    dist = w + qi - kj
    band = (dist >= 0) & (dist <= w)
    bias_sc[0] = jnp.where(band, 0.0, NEG)
    bias_sc[1] = jnp.where(band & ((kj >= w) | jnp.logical_not(first_chunk)), 0.0, NEG)

    def transposed(v):
        return v.T

    ones_row = (lax.broadcasted_iota(jnp.int32, (VT_ROWS - LANES, w), 0) == 0).astype(BF16)
    for r in range(nres):
        for hp in range(WIDTH // LANES):
            lanes = slice(hp * LANES, (hp + 1) * LANES)
            vt_sc[r, 0, hp, 0:LANES, :] = transposed(vp_ref[0, r, :, lanes])
            vt_sc[r, 0, hp, LANES:VT_ROWS, :] = ones_row
            for n in range(nblk):
                vt_sc[r, n + 1, hp, 0:LANES, :] = transposed(vc_ref[0, r, n * w:(n + 1) * w, lanes])
                vt_sc[r, n + 1, hp, LANES:VT_ROWS, :] = ones_row
        k_sc[r, 0:w, :] = kp_ref[0, r]
        k_sc[r, w:w + lc, :] = kc_ref[0, r]

    def scores(t, hp):
        lanes = slice(hp * LANES, (hp + 1) * LANES)
        r, n = t // nblk, t % nblk
        row0 = pl.multiple_of(n * w, w)
        qb = q_ref[0, r, pl.ds(row0, w), lanes]
        zero = jnp.zeros_like(qb)
        qs = jnp.concatenate([jnp.where(head_a, qb, zero), jnp.where(head_a, zero, qb)], axis=0)
        kk = k_sc[r, pl.ds(row0, 2 * w), lanes]
        return lax.dot_general(kk, qs, (((1,), (1,)), ((), ())), preferred_element_type=F32)

    def finish(t, hp, raw):
        lanes = slice(hp * LANES, (hp + 1) * LANES)
        r, n = t // nblk, t % nblk
        row0 = pl.multiple_of(n * w, w)
        bias = bias_sc[jnp.where(n == 0, 1, 0)]
        ms, ps = [], []
        for half in (slice(0, w), slice(w, 2 * w)):
            s = raw[:, half] + bias[:, half]
            m_h = jnp.max(s, axis=0, keepdims=True)
            ms.append(m_h)
            ps.append(jnp.exp2(s - m_h).astype(BF16))
        m = jnp.concatenate(ms, axis=1)
        p = jnp.concatenate(ps, axis=1)
        vv = jnp.concatenate([vt_sc[r, n, hp], vt_sc[r, n + 1, hp]], axis=1)
        pv = jnp.dot(vv, p, preferred_element_type=F32)
        den = pv[LANES:LANES + 1, :]
        o = pv[0:LANES, :] * (1.0 / den)
        lse = m + jnp.log2(den)
        o_sel = jnp.concatenate([o[:HEAD_DIM, :w], o[HEAD_DIM:, w:]], axis=0)
        l_sel = jnp.concatenate([jnp.broadcast_to(lse[:, :w], (HEAD_DIM, w)),
                                 jnp.broadcast_to(lse[:, w:], (HEAD_DIM, w))], axis=0)
        o_ref[0, r, pl.ds(row0, w), lanes] = o_sel.T.astype(o_ref.dtype)
        l_ref[0, r, pl.ds(row0, w), lanes] = l_sel.T

    nhp = WIDTH // LANES

    def issue(n, s_ref):
        for hp in range(nhp):
            s_ref[hp] = scores(n, hp)

    def consume(n, s_ref):
        for hp in range(nhp):
            finish(n, hp, s_ref[hp])

    issue(0, sa_sc)

    total = nres * nblk
    unroll = math.gcd(total, DSWA_UNROLL)

    def group(j, carry):
        t0 = unroll * j
        for i in range(unroll):
            cur, nxt = (sa_sc, sb_sc) if i % 2 == 0 else (sb_sc, sa_sc)
            issue(jnp.minimum(t0 + i + 1, total - 1), nxt)
            consume(t0 + i, cur)
        return carry

    lax.fori_loop(0, total // unroll, group, 0)


def _dswa_group(qkv, splits):
    b, d, sub, _ = qkv.shape
    lc = min(sub, DSWA_UNROLL * DSWA_W)
    per = lc // DSWA_W
    nres = min(d, max(1, DSWA_UNROLL // per))
    assert (nres * per) % 2 == 0 and d % nres == 0, "the kernel walks query blocks in even groups"
    sq, sk, sv = splits

    def cur(split):
        return pl.BlockSpec((1, nres, lc, WIDTH), lambda bi, r, c: (bi, r, c, split))

    def prv(split):
        return pl.BlockSpec((1, nres, DSWA_W, WIDTH), lambda bi, r, c: (bi, r, jnp.maximum(c * per - 1, 0), split))

    out = pl.BlockSpec((1, nres, lc, WIDTH), lambda bi, r, c: (bi, r, c, 0))
    return pl.pallas_call(
        functools.partial(_dswa_kernel, lc=lc),
        grid=(b, d // nres, sub // lc),
        in_specs=[cur(sq), cur(sk), prv(sk), cur(sv), prv(sv)],
        out_specs=[out, out],
        out_shape=[jax.ShapeDtypeStruct((b, d, sub, WIDTH), BF16),
                   jax.ShapeDtypeStruct((b, d, sub, WIDTH), F32)],
        scratch_shapes=[pltpu.VMEM((nres, per + 1, WIDTH // LANES, VT_ROWS, DSWA_W), BF16),
                        pltpu.VMEM((2, 2 * DSWA_W, 2 * DSWA_W), F32),
                        pltpu.VMEM((nres, lc + DSWA_W, WIDTH), BF16),
                        pltpu.VMEM((WIDTH // LANES, 2 * DSWA_W, 2 * DSWA_W), F32),
                        pltpu.VMEM((WIDTH // LANES, 2 * DSWA_W, 2 * DSWA_W), F32)],
        compiler_params=_params("arbitrary", "arbitrary", "arbitrary"),
        name=f"dswa_d{d}",
    )(qkv, qkv, qkv, qkv, qkv)


def _diff_kernel(lam_ref, gain_ref, qt_ref, k_ref, vt_ref, o_ref, m_sc, acc_sc, sa_sc, sb_sc, *, qb,
                 lambda_init):
    nh = qt_ref.shape[1]
    heads = range(nh)
    comp_a = lax.broadcasted_iota(jnp.int32, (LANES, 1), 0) < HEAD_DIM
    key = lax.broadcasted_iota(jnp.int32, (qb, 2 * qb), 0)
    qry = lax.broadcasted_iota(jnp.int32, (qb, 2 * qb), 1) & (qb - 1)
    lam_p = lam_ref[...]
    lam = (jnp.exp(jnp.sum(lam_p[0:1] * lam_p[1:2], axis=1, keepdims=True))
           - jnp.exp(jnp.sum(lam_p[2:3] * lam_p[3:4], axis=1, keepdims=True)) + lambda_init)

    def query_block(qi, carry):
        q_rows = pl.ds(pl.multiple_of(qi * qb, qb), qb)

        def stacked_queries(h):
            qt = qt_ref[0, h, qi]
            zero = jnp.zeros_like(qt)
            return jnp.concatenate([jnp.where(comp_a, qt, zero), jnp.where(comp_a, zero, qt)], axis=1)

        qs = [stacked_queries(h) for h in heads]
        m_sc[...] = jnp.full(m_sc.shape, NEG, F32)
        acc_sc[...] = jnp.zeros(acc_sc.shape, F32)

        def issue(blk, s_ref):
            rows = pl.ds(pl.multiple_of(blk * qb, qb), qb)
            for h in heads:
                kb = k_ref[0, rows, h * LANES:(h + 1) * LANES]
                s_ref[h] = jnp.dot(kb, qs[h], preferred_element_type=F32)

        def consume(blk, s_ref, mask):
            for h in heads:
                s = s_ref[h]
                if mask is not None:
                    s = jnp.where(mask, s, NEG)
                m_old = m_sc[h]
                m_new = jnp.maximum(m_old, jnp.max(s, axis=0, keepdims=True))
                alpha = jnp.exp2(m_old - m_new)
                p = jnp.exp2(s - m_new).astype(BF16)
                acc_sc[h] = alpha * acc_sc[h] + jnp.dot(vt_ref[0, h, blk], p, preferred_element_type=F32)
                m_sc[h] = m_new

        last = jnp.maximum(qi - 1, 0)
        issue(qi, sa_sc)
        issue(0, sb_sc)
        consume(qi, sa_sc, key <= qry)

        def pair(j, c):
            b0 = 2 * j
            issue(b0 + 1, sa_sc)
            consume(b0, sb_sc, None)
            issue(jnp.minimum(b0 + 2, last), sb_sc)
            consume(b0 + 1, sa_sc, None)
            return c

        lax.fori_loop(0, qi // 2, pair, 0)

        @pl.when(qi % 2 == 1)
        def _():
            consume(qi - 1, sb_sc, None)

        for h in heads:
            o_all = acc_sc[h, 0:LANES, :] / acc_sc[h, LANES:LANES + 1, :]
            o = o_all[:, :qb] - lam * o_all[:, qb:]
            o = o * lax.rsqrt(jnp.mean(o * o, axis=0, keepdims=True) + SUBLN_EPS) * gain_ref[...]
            o_ref[0, q_rows, h * LANES:(h + 1) * LANES] = (o * (1.0 - lambda_init)).T.astype(o_ref.dtype)
        return carry

    lax.fori_loop(0, qt_ref.shape[2], query_block, 0)


def _diff(u3, qt, vt, lam_p, subln_gain, lambda_init, qb, nh=2):
    b, s, _ = u3.shape
    per = WIDTH // (nh * LANES)
    nb = s // qb
    seq = lambda split: pl.BlockSpec((1, s, nh * LANES), lambda bi, g: (bi, 0, split * per + g))
    return pl.pallas_call(
        functools.partial(_diff_kernel, qb=qb, lambda_init=lambda_init),
        grid=(b, N_HEADS_DIFF // nh),
        in_specs=[pl.BlockSpec(lam_p.shape, lambda bi, g: (0, 0)),
                  pl.BlockSpec((LANES, 1), lambda bi, g: (0, 0)),
                  pl.BlockSpec((1, nh, nb, LANES, qb), lambda bi, g: (bi, g, 0, 0, 0)),
                  seq(COL_KB),
                  pl.BlockSpec((1, nh, nb, VT_ROWS, qb), lambda bi, g: (bi, g, 0, 0, 0))],
        out_specs=pl.BlockSpec((1, s, nh * LANES), lambda bi, g: (bi, 0, g)),
        out_shape=jax.ShapeDtypeStruct((b, s, WIDTH), BF16),
        scratch_shapes=[pltpu.VMEM((nh, 1, 2 * qb), F32),
                        pltpu.VMEM((nh, VT_ROWS, 2 * qb), F32),
                        pltpu.VMEM((nh, qb, 2 * qb), F32), pltpu.VMEM((nh, qb, 2 * qb), F32)],
        compiler_params=_params("arbitrary", "arbitrary"),
        name="diff",
    )(lam_p, subln_gain.reshape(LANES, 1), qt, u3, vt)


def _silu_of_half(half):
    return half + half * jnp.tanh(half)


def _post_kernel(h_ref, ga_ref, gb_ref, o1_ref, l1_ref, o4_ref, l4_ref, o16_ref, l16_ref, ob_ref, p_ref,
                 wo32_ref, ng_ref, wg32_ref, wp32_ref, fg_ref, out_ref, on_sc, ln_sc, o4_sc, l4_sc,
                 wo_ref, wg_ref, wp_ref, *, final):
    tm = h_ref.shape[0]

    @pl.when(pl.program_id(0) == 0)
    def _():
        wo_ref[...] = wo32_ref[0].astype(BF16)
        wg_ref[...] = (0.5 * wg32_ref[0]).astype(BF16)
        wp_ref[...] = (0.5 * wp32_ref[0]).astype(BF16)

    hr = tm // POST_PARTS
    n4, n16 = hr // 4, hr // 16

    def merge(part):
        rows, rows4, rows16 = (pl.ds(part * n, n) for n in (hr, n4, n16))
        merged = []
        for j in range(WIDTH // LANES):
            lanes = slice(j * LANES, (j + 1) * LANES)
            for r4 in range(4):
                for k in range(4):
                    o4_sc[pl.ds(k, n16, stride=4), :] = o16_ref[0, 4 * k + r4, rows16, lanes].astype(F32)
                    l4_sc[pl.ds(k, n16, stride=4), :] = l16_ref[0, 4 * k + r4, rows16, lanes]
                o16, l16 = o4_sc[...], l4_sc[...]
                o4, l4 = o4_ref[0, r4, rows4, lanes].astype(F32), l4_ref[0, r4, rows4, lanes]
                top = jnp.maximum(l4, l16)
                w4, w16 = jnp.exp2(l4 - top), jnp.exp2(l16 - top)
                den = w4 + w16
                on_sc[pl.ds(r4, n4, stride=4), :] = (w4 * o4 + w16 * o16) / den
                ln_sc[pl.ds(r4, n4, stride=4), :] = top + jnp.log2(den)
            o1, l1 = o1_ref[rows, lanes].astype(F32), l1_ref[rows, lanes]
            ox, lx = on_sc[...], ln_sc[...]
            top = jnp.maximum(l1, lx)
            w1, wx = jnp.exp2(l1 - top), jnp.exp2(lx - top)
            merged.append((w1 * o1 + wx * ox) / (w1 + wx))
        return jnp.concatenate(merged, axis=1)

    def mix(part, oa):
        rows = pl.ds(part * hr, hr)
        ya = (oa * _silu_of_half(ga_ref[rows, :].astype(F32))).astype(BF16)
        yb = (ob_ref[rows, :].astype(F32) * _silu_of_half(gb_ref[rows, :].astype(F32))).astype(BF16)
        y = (jnp.dot(ya, wo_ref[0:WIDTH, :], preferred_element_type=F32)
             + jnp.dot(yb, wo_ref[WIDTH:2 * WIDTH, :], preferred_element_type=F32))
        return h_ref[rows, :] + y

    def embed(part, h1):
        rows = pl.ds(part * hr, hr)
        n = h1 * lax.rsqrt(jnp.mean(h1 * h1, axis=-1, keepdims=True) + RMS_EPS) * ng_ref[...]
        t = jnp.tanh(jnp.dot(n.astype(BF16), wg_ref[...], preferred_element_type=F32))
        ple_half = jnp.dot(p_ref[0, rows, :].astype(BF16), wp_ref[...], preferred_element_type=F32)
        h2 = h1 + ple_half + ple_half * t
        if final:
            h2 = h2 * lax.rsqrt(jnp.mean(h2 * h2, axis=-1, keepdims=True) + RMS_EPS) * fg_ref[...]
        out_ref[rows, :] = h2

    h1 = [mix(part, merge(part)) for part in range(POST_PARTS)]
    for part in range(POST_PARTS):
        embed(part, h1[part])


def _post(h, u, g1, g4, g16, ob, p, w_out, ple_gain, w_gate, w_ple, layer, final_gain, final, batch, tm=512):
    m, d = h.shape
    whole = lambda w: pl.BlockSpec((1,) + w.shape[1:], lambda i: (layer, 0, 0), pipeline_mode=pl.Buffered(1))
    nt = m // batch // tm
    row = lambda i: (i, 0)
    const = lambda i: (0, 0)
    res = lambda i: (i // nt, 0, i % nt, 0)
    grp = lambda dil: pl.BlockSpec((1, dil, tm // dil, WIDTH), res)
    return pl.pallas_call(
        functools.partial(_post_kernel, final=final),
        grid=(m // tm,),
        in_specs=[pl.BlockSpec((tm, d), row),
                  pl.BlockSpec((tm, WIDTH), lambda i: (i, COL_GA)),
                  pl.BlockSpec((tm, WIDTH), lambda i: (i, COL_GB)),
                  pl.BlockSpec((tm, WIDTH), row), pl.BlockSpec((tm, WIDTH), row),
                  grp(4), grp(4), grp(16), grp(16),
                  pl.BlockSpec((tm, WIDTH), row),
                  pl.BlockSpec((1, tm, p.shape[2]), lambda i: (layer, i, 0)),
                  whole(w_out),
                  pl.BlockSpec((1, d), const),
                  whole(w_gate),
                  whole(w_ple),
                  pl.BlockSpec((1, d), const)],
        out_specs=pl.BlockSpec((tm, d), row),
        out_shape=jax.ShapeDtypeStruct((m, d), F32),
        scratch_shapes=[pltpu.VMEM((tm // POST_PARTS, LANES), F32), pltpu.VMEM((tm // POST_PARTS, LANES), F32),
                        pltpu.VMEM((tm // POST_PARTS // 4, LANES), F32),
                        pltpu.VMEM((tm // POST_PARTS // 4, LANES), F32),
                        pltpu.VMEM(w_out.shape[1:], BF16), pltpu.VMEM(w_gate.shape[1:], BF16),
                        pltpu.VMEM(w_ple.shape[1:], BF16)],
        compiler_params=_params("arbitrary"),
        name="post",
    )(h, u, u, g1[0].reshape(m, WIDTH), g1[1].reshape(m, WIDTH), g4[0], g4[1], g16[0], g16[1], ob, p,
      w_out, ple_gain.reshape(1, d), w_gate, w_ple, final_gain.reshape(1, d))


def kernel(x, p, attn_norm_gain, w_in, w_out, lambda_q1, lambda_k1, lambda_q2, lambda_k2, subln_gain,
           ple_norm_gain, w_ple_gate, w_ple, final_norm_gain):
    b, s, d = x.shape
    depth = w_in.shape[0]
    h = x.reshape(b * s, d)
    cos, sa, sb = _rope_tables(s)
    col = jnp.arange(IN_COLS) // WIDTH
    col_scale = jnp.where((col == COL_QA) | (col == COL_QB), HEAD_DIM ** -0.5 * LOG2_E,
                          jnp.where((col == COL_GA) | (col == COL_GB), 0.5, 1.0)).astype(F32)
    a_splits = (COL_QA, COL_KA, COL_VA)
    for i in range(depth):
        lambda_init = 0.8 - 0.6 * math.exp(-0.3 * i)
        u, qkv4, qkv16, vt, qt = _proj(h, attn_norm_gain[i], w_in, i, col_scale, cos, sa, sb, b, DIFF_QB)
        u3 = u.reshape(b, s, IN_COLS)
        g1 = _dswa_group(u3.reshape(b, 1, s, IN_COLS), a_splits)
        g4 = _dswa_group(qkv4, a_splits)
        g16 = _dswa_group(qkv16, a_splits)
        lam_p = jnp.stack([lambda_q1[i], lambda_k1[i], lambda_q2[i], lambda_k2[i]]).astype(F32)
        ob = _diff(u3, qt, vt, lam_p, subln_gain[i], lambda_init, DIFF_QB)
        h = _post(h, u, g1, g4, g16, ob.reshape(b * s, WIDTH), p.reshape(depth, b * s, -1),
                  w_out, ple_norm_gain[i], w_ple_gate, w_ple, i, final_norm_gain, final=i == depth - 1, batch=b)
    return h.reshape(b, s, d)
```

```python
import functools
import math

import jax
import jax.numpy as jnp
from jax import lax
from jax.experimental import pallas as pl
from jax.experimental.pallas import tpu as pltpu

HEAD_DIM = 64
LANES = 128
DSWA_W = 128
DSWA_UNROLL = 16
N_HEADS_DIFF = 4
DIFF_QB = 512
POST_PARTS = 2
VT_ROWS = LANES + 16
WIDTH = 512
IN_COLS = 8 * WIDTH
ROPE_THETA = 500000.0
ROPE_DIM = HEAD_DIM // 4
RMS_EPS = 1e-6
SUBLN_EPS = 1e-5
NEG = -1e30
LOG2_E = math.log2(math.e)
VMEM_LIMIT = 48 * 1024 * 1024
PROJ_VMEM_LIMIT = 58 * 1024 * 1024

COL_QA, COL_KA, COL_VA, COL_GA, COL_QB, COL_KB, COL_VB, COL_GB = range(8)
ROPE_SPLITS = (COL_QA, COL_KA, COL_QB, COL_KB)

BF16 = jnp.bfloat16
F32 = jnp.float32


def _params(*sem):
    return pltpu.CompilerParams(dimension_semantics=sem, vmem_limit_bytes=VMEM_LIMIT)


def _proj_kernel(x_ref, g_ref, w32_ref, ws_ref, cb_ref, sn_ref, co_ref, so_ref, o_ref, d4_ref, d16_ref, vt_ref, qt_ref,
                 buf_sc, buf4_sc, w_ref, *, qb):
    tm = x_ref.shape[0]

    @pl.when(pl.program_id(0) == 0)
    def _():
        for c in range(IN_COLS // WIDTH):
            cols = slice(c * WIDTH, (c + 1) * WIDTH)
            w_ref[:, cols] = (w32_ref[0, :, cols] * ws_ref[:, cols]).astype(BF16)

    x = x_ref[...]
    ms = jnp.mean(x * x, axis=-1, keepdims=True)
    xn = (x * lax.rsqrt(ms + RMS_EPS) * g_ref[...]).astype(BF16)
    cb, sn, co, so = cb_ref[...], sn_ref[...], co_ref[0], so_ref[0]
    cos = cb * co - sn * so
    sin = sn * co + cb * so
    ch = lax.broadcasted_iota(jnp.int32, (1, LANES), 1) % HEAD_DIM
    sa = jnp.where(ch < ROPE_DIM // 2, -sin, 0.0)
    sb = jnp.where((ch >= ROPE_DIM // 2) & (ch < ROPE_DIM), sin, 0.0)
    for c in range(IN_COLS // WIDTH):
        cols = slice(c * WIDTH, (c + 1) * WIDTH)
        acc = jnp.dot(xn, w_ref[:, cols], preferred_element_type=F32)
        if c in ROPE_SPLITS:
            blocks = []
            for j in range(WIDTH // LANES):
                blk = acc[:, j * LANES:(j + 1) * LANES]
                blocks.append(blk * cos + pltpu.roll(blk, LANES - ROPE_DIM // 2, 1) * sa
                              + pltpu.roll(blk, ROPE_DIM // 2, 1) * sb)
            acc = jnp.concatenate(blocks, axis=1)
        o_ref[:, cols] = acc.astype(BF16)
        if c in (COL_QA, COL_KA, COL_VA):
            n4, n16 = tm // 4, tm // 16
            for j in range(WIDTH // LANES):
                lanes = slice(c * WIDTH + j * LANES, c * WIDTH + (j + 1) * LANES)
                buf_sc[j] = acc[:, j * LANES:(j + 1) * LANES]
                for r4 in range(4):
                    rows4 = buf_sc[j, pl.ds(r4, n4, stride=4), :]
                    d4_ref[0, r4, :, lanes] = rows4.astype(BF16)
                    buf4_sc[j, r4 * n4:(r4 + 1) * n4, :] = rows4
                for r4 in range(4):
                    for k in range(4):
                        rows16 = buf4_sc[j, pl.ds(r4 * n4 + k, n16, stride=4), :]
                        d16_ref[0, 4 * k + r4, :, lanes] = rows16.astype(BF16)
        if c == COL_QB:
            for kb in range(tm // qb):
                for hd in range(N_HEADS_DIFF):
                    blk = acc[kb * qb:(kb + 1) * qb, hd * LANES:(hd + 1) * LANES]
                    qt_ref[0, hd, kb] = blk.T.astype(BF16)
        if c == COL_VB:
            ones_row = (lax.broadcasted_iota(jnp.int32, (VT_ROWS - LANES, qb), 0) == 0).astype(BF16)
            for kb in range(tm // qb):
                for hd in range(N_HEADS_DIFF):
                    blk = acc[kb * qb:(kb + 1) * qb, hd * LANES:(hd + 1) * LANES]
                    vt_ref[0, hd, kb, 0:LANES, :] = blk.T.astype(BF16)
                    vt_ref[0, hd, kb, LANES:VT_ROWS, :] = ones_row


def _proj(h, gain, w_in, layer, col_scale, rope, batch, qb, tm=512):
    m, d = h.shape
    seq = m // batch
    nt = seq // tm
    row = lambda i: (i, 0)
    base = lambda i: (0, 0)
    off = lambda i: (i % nt, 0, 0)
    res = lambda i: (i // nt, 0, i % nt, 0)
    return pl.pallas_call(
        functools.partial(_proj_kernel, qb=qb),
        grid=(m // tm,),
        in_specs=[pl.BlockSpec((tm, d), row),
                  pl.BlockSpec((1, d), lambda i: (0, 0)),
                  pl.BlockSpec((1, d, IN_COLS), lambda i: (layer, 0, 0), pipeline_mode=pl.Buffered(1)),
                  pl.BlockSpec((1, IN_COLS), lambda i: (0, 0)),
                  pl.BlockSpec((tm, LANES), base),
                  pl.BlockSpec((tm, LANES), base),
                  pl.BlockSpec((1, 1, LANES), off),
                  pl.BlockSpec((1, 1, LANES), off)],
        out_specs=[pl.BlockSpec((tm, IN_COLS), row),
                   pl.BlockSpec((1, 4, tm // 4, 3 * WIDTH), res),
                   pl.BlockSpec((1, 16, tm // 16, 3 * WIDTH), res),
                   pl.BlockSpec((1, N_HEADS_DIFF, tm // qb, VT_ROWS, qb), lambda i: (i // nt, 0, i % nt, 0, 0)),
                   pl.BlockSpec((1, N_HEADS_DIFF, tm // qb, LANES, qb), lambda i: (i // nt, 0, i % nt, 0, 0))],
        out_shape=[jax.ShapeDtypeStruct((m, IN_COLS), BF16),
                   jax.ShapeDtypeStruct((batch, 4, seq // 4, 3 * WIDTH), BF16),
                   jax.ShapeDtypeStruct((batch, 16, seq // 16, 3 * WIDTH), BF16),
                   jax.ShapeDtypeStruct((batch, N_HEADS_DIFF, seq // qb, VT_ROWS, qb), BF16),
                   jax.ShapeDtypeStruct((batch, N_HEADS_DIFF, seq // qb, LANES, qb), BF16)],
        scratch_shapes=[pltpu.VMEM((WIDTH // LANES, tm, LANES), F32),
                        pltpu.VMEM((WIDTH // LANES, tm, LANES), F32),
                        pltpu.VMEM((d, IN_COLS), BF16)],
        compiler_params=pltpu.CompilerParams(dimension_semantics=("arbitrary",),
                                             vmem_limit_bytes=PROJ_VMEM_LIMIT),
        name="proj",
    )(h, gain.reshape(1, d), w_in, col_scale.reshape(1, IN_COLS), *rope)


def _rope_tables(seq, tm=512):
    half = ROPE_DIM // 2
    inv = jnp.power(ROPE_THETA, -jnp.arange(half, dtype=F32) * (2.0 / ROPE_DIM))
    ch = jnp.arange(LANES) % HEAD_DIM
    inv_l = jnp.where(ch < ROPE_DIM, jnp.take(inv, ch % half), 0.0)
    a_in = jnp.arange(tm).astype(F32)[:, None] * inv_l[None, :]
    a_off = (jnp.arange(seq // tm) * tm).astype(F32)[:, None, None] * inv_l[None, None, :]
    return jnp.cos(a_in), jnp.sin(a_in), jnp.cos(a_off), jnp.sin(a_off)


def _dswa_kernel(q_ref, kc_ref, kp_ref, vc_ref, vp_ref, o_ref, l_ref, vt_sc, bias_sc, k_sc, sa_sc, sb_sc, *,
                 lc):
    w = DSWA_W
    nblk = lc // w
    nres = q_ref.shape[1]
    first_chunk = pl.program_id(2) == 0
    head_a = lax.broadcasted_iota(jnp.int32, (1, LANES), 1) < HEAD_DIM
    kj = lax.broadcasted_iota(jnp.int32, (2 * w, 2 * w), 0)
    qi = lax.broadcasted_iota(jnp.int32, (2 * w, 2 * w), 1) & (w - 1)
    dist = w + qi - kj
    band = (dist >= 0) & (dist <= w)
    bias_sc[0] = jnp.where(band, 0.0, NEG)
    bias_sc[1] = jnp.where(band & ((kj >= w) | jnp.logical_not(first_chunk)), 0.0, NEG)

    def transposed(v):
        return v.T

    ones_row = (lax.broadcasted_iota(jnp.int32, (VT_ROWS - LANES, w), 0) == 0).astype(BF16)
    for r in range(nres):
        for hp in range(WIDTH // LANES):
            lanes = slice(hp * LANES, (hp + 1) * LANES)
            vt_sc[r, 0, hp, 0:LANES, :] = transposed(vp_ref[0, r, :, lanes])
            vt_sc[r, 0, hp, LANES:VT_ROWS, :] = ones_row
            for n in range(nblk):
                vt_sc[r, n + 1, hp, 0:LANES, :] = transposed(vc_ref[0, r, n * w:(n + 1) * w, lanes])
                vt_sc[r, n + 1, hp, LANES:VT_ROWS, :] = ones_row
        k_sc[r, 0:w, :] = kp_ref[0, r]
        k_sc[r, w:w + lc, :] = kc_ref[0, r]

    def scores(t, hp):
        lanes = slice(hp * LANES, (hp + 1) * LANES)
        r, n = t // nblk, t % nblk
        row0 = pl.multiple_of(n * w, w)
        qb = q_ref[0, r, pl.ds(row0, w), lanes]
        zero = jnp.zeros_like(qb)
        qs = jnp.concatenate([jnp.where(head_a, qb, zero), jnp.where(head_a, zero, qb)], axis=0)
        kk = k_sc[r, pl.ds(row0, 2 * w), lanes]
        return lax.dot_general(kk, qs, (((1,), (1,)), ((), ())), preferred_element_type=F32)

    def finish(t, hp, raw):
        lanes = slice(hp * LANES, (hp + 1) * LANES)
        r, n = t // nblk, t % nblk
        row0 = pl.multiple_of(n * w, w)
        bias = bias_sc[jnp.where(n == 0, 1, 0)]
        ms, ps = [], []
        for half in (slice(0, w), slice(w, 2 * w)):
            s = raw[:, half] + bias[:, half]
            m_h = jnp.max(s, axis=0, keepdims=True)
            ms.append(m_h)
            ps.append(jnp.exp2(s - m_h).astype(BF16))
        m = jnp.concatenate(ms, axis=1)
        p = jnp.concatenate(ps, axis=1)
        vv = jnp.concatenate([vt_sc[r, n, hp], vt_sc[r, n + 1, hp]], axis=1)
        pv = jnp.dot(vv, p, preferred_element_type=F32)
        den = pv[LANES:LANES + 1, :]
        o = pv[0:LANES, :] * (1.0 / den)
        lse = m + jnp.log2(den)
        o_sel = jnp.concatenate([o[:HEAD_DIM, :w], o[HEAD_DIM:, w:]], axis=0)
        l_sel = jnp.concatenate([jnp.broadcast_to(lse[:, :w], (HEAD_DIM, w)),
                                 jnp.broadcast_to(lse[:, w:], (HEAD_DIM, w))], axis=0)
        o_ref[0, r, pl.ds(row0, w), lanes] = o_sel.T.astype(o_ref.dtype)
        l_ref[0, r, pl.ds(row0, w), lanes] = l_sel.T

    nhp = WIDTH // LANES

    def issue(n, s_ref):
        for hp in range(nhp):
            s_ref[hp] = scores(n, hp)

    def consume(n, s_ref):
        for hp in range(nhp):
            finish(n, hp, s_ref[hp])

    issue(0, sa_sc)

    total = nres * nblk
    unroll = math.gcd(total, DSWA_UNROLL)

    def group(j, carry):
        t0 = unroll * j
        for i in range(unroll):
            cur, nxt = (sa_sc, sb_sc) if i % 2 == 0 else (sb_sc, sa_sc)
            issue(jnp.minimum(t0 + i + 1, total - 1), nxt)
            consume(t0 + i, cur)
        return carry

    lax.fori_loop(0, total // unroll, group, 0)


def _dswa_group(qkv, splits):
    b, d, sub, _ = qkv.shape
    lc = min(sub, DSWA_UNROLL * DSWA_W)
    per = lc // DSWA_W
    nres = min(d, max(1, DSWA_UNROLL // per))
    assert (nres * per) % 2 == 0 and d % nres == 0, "the kernel walks query blocks in even groups"
    sq, sk, sv = splits

    def cur(split):
        return pl.BlockSpec((1, nres, lc, WIDTH), lambda bi, r, c: (bi, r, c, split))

    def prv(split):
        return pl.BlockSpec((1, nres, DSWA_W, WIDTH), lambda bi, r, c: (bi, r, jnp.maximum(c * per - 1, 0), split))

    out = pl.BlockSpec((1, nres, lc, WIDTH), lambda bi, r, c: (bi, r, c, 0))
    return pl.pallas_call(
        functools.partial(_dswa_kernel, lc=lc),
        grid=(b, d // nres, sub // lc),
        in_specs=[cur(sq), cur(sk), prv(sk), cur(sv), prv(sv)],
        out_specs=[out, out],
        out_shape=[jax.ShapeDtypeStruct((b, d, sub, WIDTH), BF16),
                   jax.ShapeDtypeStruct((b, d, sub, WIDTH), F32)],
        scratch_shapes=[pltpu.VMEM((nres, per + 1, WIDTH // LANES, VT_ROWS, DSWA_W), BF16),
                        pltpu.VMEM((2, 2 * DSWA_W, 2 * DSWA_W), F32),
                        pltpu.VMEM((nres, lc + DSWA_W, WIDTH), BF16),
                        pltpu.VMEM((WIDTH // LANES, 2 * DSWA_W, 2 * DSWA_W), F32),
                        pltpu.VMEM((WIDTH // LANES, 2 * DSWA_W, 2 * DSWA_W), F32)],
        compiler_params=_params("arbitrary", "arbitrary", "arbitrary"),
        name=f"dswa_d{d}",
    )(qkv, qkv, qkv, qkv, qkv)


def _diff_kernel(lam_ref, gain_ref, qt_ref, k_ref, vt_ref, o_ref, m_sc, acc_sc, sa_sc, sb_sc, *, qb,
                 lambda_init):
    nh = qt_ref.shape[1]
    heads = range(nh)
    comp_a = lax.broadcasted_iota(jnp.int32, (LANES, 1), 0) < HEAD_DIM
    key = lax.broadcasted_iota(jnp.int32, (qb, 2 * qb), 0)
    qry = lax.broadcasted_iota(jnp.int32, (qb, 2 * qb), 1) & (qb - 1)
    lam_p = lam_ref[...]
    lam = (jnp.exp(jnp.sum(lam_p[0:1] * lam_p[1:2], axis=1, keepdims=True))
           - jnp.exp(jnp.sum(lam_p[2:3] * lam_p[3:4], axis=1, keepdims=True)) + lambda_init)

    def query_block(qi, carry):
        q_rows = pl.ds(pl.multiple_of(qi * qb, qb), qb)

        def stacked_queries(h):
            qt = qt_ref[0, h, qi]
            zero = jnp.zeros_like(qt)
            return jnp.concatenate([jnp.where(comp_a, qt, zero), jnp.where(comp_a, zero, qt)], axis=1)

        qs = [stacked_queries(h) for h in heads]
        m_sc[...] = jnp.full(m_sc.shape, NEG, F32)
        acc_sc[...] = jnp.zeros(acc_sc.shape, F32)

        def issue(blk, s_ref):
            rows = pl.ds(pl.multiple_of(blk * qb, qb), qb)
            for h in heads:
                kb = k_ref[0, rows, h * LANES:(h + 1) * LANES]
                s_ref[h] = jnp.dot(kb, qs[h], preferred_element_type=F32)

        def consume(blk, s_ref, mask):
            for h in heads:
                s = s_ref[h]
                if mask is not None:
                    s = jnp.where(mask, s, NEG)
                m_old = m_sc[h]
                m_new = jnp.maximum(m_old, jnp.max(s, axis=0, keepdims=True))
                alpha = jnp.exp2(m_old - m_new)
                p = jnp.exp2(s - m_new).astype(BF16)
                acc_sc[h] = alpha * acc_sc[h] + jnp.dot(vt_ref[0, h, blk], p, preferred_element_type=F32)
                m_sc[h] = m_new

        last = jnp.maximum(qi - 1, 0)
        issue(qi, sa_sc)
        issue(0, sb_sc)
        consume(qi, sa_sc, key <= qry)

        def pair(j, c):
            b0 = 2 * j
            issue(b0 + 1, sa_sc)
            consume(b0, sb_sc, None)
            issue(jnp.minimum(b0 + 2, last), sb_sc)
            consume(b0 + 1, sa_sc, None)
            return c

        lax.fori_loop(0, qi // 2, pair, 0)

        @pl.when(qi % 2 == 1)
        def _():
            consume(qi - 1, sb_sc, None)

        for h in heads:
            o_all = acc_sc[h, 0:LANES, :] / acc_sc[h, LANES:LANES + 1, :]
            o = o_all[:, :qb] - lam * o_all[:, qb:]
            o = o * lax.rsqrt(jnp.mean(o * o, axis=0, keepdims=True) + SUBLN_EPS) * gain_ref[...]
            o_ref[0, q_rows, h * LANES:(h + 1) * LANES] = (o * (1.0 - lambda_init)).T.astype(o_ref.dtype)
        return carry

    lax.fori_loop(0, qt_ref.shape[2], query_block, 0)


def _diff(u3, qt, vt, lam_p, subln_gain, lambda_init, qb, nh=2):
    b, s, _ = u3.shape
    per = WIDTH // (nh * LANES)
    nb = s // qb
    seq = lambda split: pl.BlockSpec((1, s, nh * LANES), lambda bi, g: (bi, 0, split * per + g))
    return pl.pallas_call(
        functools.partial(_diff_kernel, qb=qb, lambda_init=lambda_init),
        grid=(b, N_HEADS_DIFF // nh),
        in_specs=[pl.BlockSpec(lam_p.shape, lambda bi, g: (0, 0)),
                  pl.BlockSpec((LANES, 1), lambda bi, g: (0, 0)),
                  pl.BlockSpec((1, nh, nb, LANES, qb), lambda bi, g: (bi, g, 0, 0, 0)),
                  seq(COL_KB),
                  pl.BlockSpec((1, nh, nb, VT_ROWS, qb), lambda bi, g: (bi, g, 0, 0, 0))],
        out_specs=pl.BlockSpec((1, s, nh * LANES), lambda bi, g: (bi, 0, g)),
        out_shape=jax.ShapeDtypeStruct((b, s, WIDTH), BF16),
        scratch_shapes=[pltpu.VMEM((nh, 1, 2 * qb), F32),
                        pltpu.VMEM((nh, VT_ROWS, 2 * qb), F32),
                        pltpu.VMEM((nh, qb, 2 * qb), F32), pltpu.VMEM((nh, qb, 2 * qb), F32)],
        compiler_params=_params("arbitrary", "arbitrary"),
        name="diff",
    )(lam_p, subln_gain.reshape(LANES, 1), qt, u3, vt)


def _silu_of_half(half):
    return half + half * jnp.tanh(half)


def _post_kernel(h_ref, ga_ref, gb_ref, o1_ref, l1_ref, o4_ref, l4_ref, o16_ref, l16_ref, ob_ref, p_ref,
                 wo32_ref, ng_ref, wg32_ref, wp32_ref, fg_ref, out_ref, on_sc, ln_sc, o4_sc, l4_sc,
                 wo_ref, wg_ref, wp_ref, *, final):
    tm = h_ref.shape[0]

    @pl.when(pl.program_id(0) == 0)
    def _():
        wo_ref[...] = wo32_ref[0].astype(BF16)
        wg_ref[...] = (0.5 * wg32_ref[0]).astype(BF16)
        wp_ref[...] = (0.5 * wp32_ref[0]).astype(BF16)

    hr = tm // POST_PARTS
    n4, n16 = hr // 4, hr // 16

    def merge(part):
        rows, rows4, rows16 = (pl.ds(part * n, n) for n in (hr, n4, n16))
        merged = []
        for j in range(WIDTH // LANES):
            lanes = slice(j * LANES, (j + 1) * LANES)
            for r4 in range(4):
                for k in range(4):
                    o4_sc[pl.ds(k, n16, stride=4), :] = o16_ref[0, 4 * k + r4, rows16, lanes].astype(F32)
                    l4_sc[pl.ds(k, n16, stride=4), :] = l16_ref[0, 4 * k + r4, rows16, lanes]
                o16, l16 = o4_sc[...], l4_sc[...]
                o4, l4 = o4_ref[0, r4, rows4, lanes].astype(F32), l4_ref[0, r4, rows4, lanes]
                top = jnp.maximum(l4, l16)
                w4, w16 = jnp.exp2(l4 - top), jnp.exp2(l16 - top)
                den = w4 + w16
                on_sc[pl.ds(r4, n4, stride=4), :] = (w4 * o4 + w16 * o16) / den
                ln_sc[pl.ds(r4, n4, stride=4), :] = top + jnp.log2(den)
            o1, l1 = o1_ref[rows, lanes].astype(F32), l1_ref[rows, lanes]
            ox, lx = on_sc[...], ln_sc[...]
            top = jnp.maximum(l1, lx)
            w1, wx = jnp.exp2(l1 - top), jnp.exp2(lx - top)
            merged.append((w1 * o1 + wx * ox) / (w1 + wx))
        return jnp.concatenate(merged, axis=1)

    def mix(part, oa):
        rows = pl.ds(part * hr, hr)
        ya = (oa * _silu_of_half(ga_ref[rows, :].astype(F32))).astype(BF16)
        yb = (ob_ref[rows, :].astype(F32) * _silu_of_half(gb_ref[rows, :].astype(F32))).astype(BF16)
        y = (jnp.dot(ya, wo_ref[0:WIDTH, :], preferred_element_type=F32)
             + jnp.dot(yb, wo_ref[WIDTH:2 * WIDTH, :], preferred_element_type=F32))
        return h_ref[rows, :] + y

    def embed(part, h1):
        rows = pl.ds(part * hr, hr)
        n = h1 * lax.rsqrt(jnp.mean(h1 * h1, axis=-1, keepdims=True) + RMS_EPS) * ng_ref[...]
        t = jnp.tanh(jnp.dot(n.astype(BF16), wg_ref[...], preferred_element_type=F32))
        ple_half = jnp.dot(p_ref[0, rows, :].astype(BF16), wp_ref[...], preferred_element_type=F32)
        h2 = h1 + ple_half + ple_half * t
        if final:
            h2 = h2 * lax.rsqrt(jnp.mean(h2 * h2, axis=-1, keepdims=True) + RMS_EPS) * fg_ref[...]
        out_ref[rows, :] = h2

    h1 = [mix(part, merge(part)) for part in range(POST_PARTS)]
    for part in range(POST_PARTS):
        embed(part, h1[part])


def _post(h, u, g1, g4, g16, ob, p, w_out, ple_gain, w_gate, w_ple, layer, final_gain, final, batch, tm=512):
    m, d = h.shape
    whole = lambda w: pl.BlockSpec((1,) + w.shape[1:], lambda i: (layer, 0, 0), pipeline_mode=pl.Buffered(1))
    nt = m // batch // tm
    row = lambda i: (i, 0)
    const = lambda i: (0, 0)
    res = lambda i: (i // nt, 0, i % nt, 0)
    grp = lambda dil: pl.BlockSpec((1, dil, tm // dil, WIDTH), res)
    return pl.pallas_call(
        functools.partial(_post_kernel, final=final),
        grid=(m // tm,),
        in_specs=[pl.BlockSpec((tm, d), row),
                  pl.BlockSpec((tm, WIDTH), lambda i: (i, COL_GA)),
                  pl.BlockSpec((tm, WIDTH), lambda i: (i, COL_GB)),
                  pl.BlockSpec((tm, WIDTH), row), pl.BlockSpec((tm, WIDTH), row),
                  grp(4), grp(4), grp(16), grp(16),
                  pl.BlockSpec((tm, WIDTH), row),
                  pl.BlockSpec((1, tm, p.shape[2]), lambda i: (layer, i, 0)),
                  whole(w_out),
                  pl.BlockSpec((1, d), const),
                  whole(w_gate),
                  whole(w_ple),
                  pl.BlockSpec((1, d), const)],
        out_specs=pl.BlockSpec((tm, d), row),
        out_shape=jax.ShapeDtypeStruct((m, d), F32),
        scratch_shapes=[pltpu.VMEM((tm // POST_PARTS, LANES), F32), pltpu.VMEM((tm // POST_PARTS, LANES), F32),
                        pltpu.VMEM((tm // POST_PARTS // 4, LANES), F32),
                        pltpu.VMEM((tm // POST_PARTS // 4, LANES), F32),
                        pltpu.VMEM(w_out.shape[1:], BF16), pltpu.VMEM(w_gate.shape[1:], BF16),
                        pltpu.VMEM(w_ple.shape[1:], BF16)],
        compiler_params=_params("arbitrary"),
        name="post",
    )(h, u, u, g1[0].reshape(m, WIDTH), g1[1].reshape(m, WIDTH), g4[0], g4[1], g16[0], g16[1], ob, p,
      w_out, ple_gain.reshape(1, d), w_gate, w_ple, final_gain.reshape(1, d))


def kernel(x, p, attn_norm_gain, w_in, w_out, lambda_q1, lambda_k1, lambda_q2, lambda_k2, subln_gain,
           ple_norm_gain, w_ple_gate, w_ple, final_norm_gain):
    b, s, d = x.shape
    depth = w_in.shape[0]
    h = x.reshape(b * s, d)
    rope = _rope_tables(s)
    col = jnp.arange(IN_COLS) // WIDTH
    col_scale = jnp.where((col == COL_QA) | (col == COL_QB), HEAD_DIM ** -0.5 * LOG2_E,
                          jnp.where((col == COL_GA) | (col == COL_GB), 0.5, 1.0)).astype(F32)
    a_splits = (COL_QA, COL_KA, COL_VA)
    for i in range(depth):
        lambda_init = 0.8 - 0.6 * math.exp(-0.3 * i)
        u, qkv4, qkv16, vt, qt = _proj(h, attn_norm_gain[i], w_in, i, col_scale, rope, b, DIFF_QB)
        u3 = u.reshape(b, s, IN_COLS)
        g1 = _dswa_group(u3.reshape(b, 1, s, IN_COLS), a_splits)
        g4 = _dswa_group(qkv4, a_splits)
        g16 = _dswa_group(qkv16, a_splits)
        lam_p = jnp.stack([lambda_q1[i], lambda_k1[i], lambda_q2[i], lambda_k2[i]]).astype(F32)
        ob = _diff(u3, qt, vt, lam_p, subln_gain[i], lambda_init, DIFF_QB)
        h = _post(h, u, g1, g4, g16, ob.reshape(b * s, WIDTH), p.reshape(depth, b * s, -1),
                  w_out, ple_norm_gain[i], w_ple_gate, w_ple, i, final_norm_gain, final=i == depth - 1, batch=b)
    return h.reshape(b, s, d)
```
